```python
import math
import jax, jax.numpy as jnp
from jax import lax
import numpy as np

D_MODEL = 1024
BATCH = 2
SEQ = 8192
DEPTH = 1

CONV_CH = D_MODEL // 2
CONV_WIDTH = 31
MLA_HEADS = 8
QK_NOPE = D_MODEL // 16
QK_ROPE = D_MODEL // 32
V_DIM = D_MODEL // 16
Q_LORA = 3 * D_MODEL // 8
KV_LORA = D_MODEL // 4
N_BRANCH = 2
IN_COLS = 2 * CONV_CH + Q_LORA + KV_LORA + QK_ROPE + N_BRANCH * D_MODEL
MEM_LEN = 256
X_HEADS = 4
X_HEAD_DIM = D_MODEL // 8
D_FF = 4 * D_MODEL
Q_BLOCK = 128
ROPE_THETA = 10000.0
EPS = 1e-6

kernel_name = "hybrid_conformer_mla_gated_block"


def rms_norm(x, g):
    xf = x.astype(jnp.float32)
    y = xf * lax.rsqrt(jnp.mean(xf * xf, axis=-1, keepdims=True) + EPS)
    return (y * g.astype(jnp.float32)).astype(x.dtype)


def layer_norm(x, g, b):
    xf = x.astype(jnp.float32)
    mu = jnp.mean(xf, axis=-1, keepdims=True)
    var = jnp.mean(jnp.square(xf - mu), axis=-1, keepdims=True)
    y = (xf - mu) * lax.rsqrt(var + EPS)
    return (y * g.astype(jnp.float32) + b.astype(jnp.float32)).astype(x.dtype)


def rope_tables(positions):
    half = QK_ROPE // 2
    inv_freq = ROPE_THETA ** (-jnp.arange(half, dtype=jnp.float32) / half)
    ang = positions.astype(jnp.float32)[..., None] * inv_freq
    return jnp.cos(ang), jnp.sin(ang)


def apply_rope(t, cos, sin):
    half = t.shape[-1] // 2
    t1, t2 = t[..., :half], t[..., half:]
    c, s = cos.astype(t.dtype), sin.astype(t.dtype)
    return jnp.concatenate([t1 * c - t2 * s, t2 * c + t1 * s], axis=-1)


def conformer_conv(conv_in, conv_w, conv_b, ln_g, ln_b, w_conv_out):
    a, gt = jnp.split(conv_in, 2, axis=-1)
    z = a * jax.nn.sigmoid(gt)
    rhs = conv_w.astype(z.dtype).reshape(CONV_WIDTH, 1, CONV_CH)
    z = lax.conv_general_dilated(
        z, rhs, window_strides=(1,), padding=[(CONV_WIDTH - 1, 0)],
        dimension_numbers=("NWC", "WIO", "NWC"), feature_group_count=CONV_CH)
    z = z + conv_b
    z = layer_norm(z, ln_g, ln_b)
    z = jax.nn.silu(z)
    return z @ w_conv_out


def mla_attention(c_q, c_kv, k_rope_raw, cos, sin, q_norm_g, w_uq, kv_norm_g, w_ukv, w_mla_out):
    B, S, _ = c_q.shape
    q = rms_norm(c_q, q_norm_g) @ w_uq
    q = q.reshape(B, S, MLA_HEADS, QK_NOPE + QK_ROPE)
    q_nope, q_rope = q[..., :QK_NOPE], q[..., QK_NOPE:]
    q_rope = apply_rope(q_rope, cos[:, :, None, :], sin[:, :, None, :])
    kv = rms_norm(c_kv, kv_norm_g) @ w_ukv
    kv = kv.reshape(B, S, MLA_HEADS, QK_NOPE + V_DIM)
    k_nope, v = kv[..., :QK_NOPE], kv[..., QK_NOPE:]
    k_rope = apply_rope(k_rope_raw, cos, sin)

    scale = (QK_NOPE + QK_ROPE) ** -0.5
    n_blk = S // Q_BLOCK
    qn = (q_nope * scale).reshape(B, n_blk, Q_BLOCK, MLA_HEADS, QK_NOPE).transpose(1, 0, 2, 3, 4)
    qr = (q_rope * scale).reshape(B, n_blk, Q_BLOCK, MLA_HEADS, QK_ROPE).transpose(1, 0, 2, 3, 4)
    key_idx = jnp.arange(S)
    neg = jnp.finfo(jnp.float32).min

    def attend(args):
        qn_b, qr_b, blk = args
        s = jnp.einsum("bqhd,bkhd->bhqk", qn_b, k_nope, preferred_element_type=jnp.float32)
        s = s + jnp.einsum("bqhr,bkr->bhqk", qr_b, k_rope, preferred_element_type=jnp.float32)
        q_idx = blk * Q_BLOCK + jnp.arange(Q_BLOCK)
        mask = key_idx[None, :] <= q_idx[:, None]
        s = jnp.where(mask[None, None], s, neg)
        p = jax.nn.softmax(s, axis=-1).astype(v.dtype)
        return jnp.einsum("bhqk,bkhd->bqhd", p, v)

    o = lax.map(attend, (qn, qr, jnp.arange(n_blk)))
    o = o.transpose(1, 0, 2, 3, 4).reshape(B, S, MLA_HEADS * V_DIM)
    return o @ w_mla_out


def memory_cross_attention(u, mem_n, w_xq, w_xkv, w_xo):
    B, S, _ = u.shape
    q = (u @ w_xq).reshape(B, S, X_HEADS, X_HEAD_DIM) * (X_HEAD_DIM ** -0.5)
    kv = (mem_n @ w_xkv).reshape(B, MEM_LEN, 2, X_HEADS, X_HEAD_DIM)
    k, v = kv[:, :, 0], kv[:, :, 1]
    s = jnp.einsum("bqhd,bkhd->bhqk", q, k, preferred_element_type=jnp.float32)
    p = jax.nn.softmax(s, axis=-1).astype(v.dtype)
    o = jnp.einsum("bhqk,bkhd->bqhd", p, v).reshape(B, S, X_HEADS * X_HEAD_DIM)
    return o @ w_xo


def setup_inputs(seed: int = 0) -> dict:
    key = jax.random.key(seed)
    ks = jax.random.split(key, 32)
    f32 = jnp.float32

    def w(k, shape, fan_in):
        return jax.random.normal(k, shape, f32) * (fan_in ** -0.5)

    def gain(k, shape):
        return 1.0 + 0.02 * jax.random.normal(k, shape, f32)

    L = DEPTH
    x = jax.random.normal(ks[0], (BATCH, SEQ, D_MODEL), f32)
    mem = jax.random.normal(ks[1], (BATCH, MEM_LEN, D_MODEL), f32)
    offsets = jax.random.randint(ks[2], (BATCH, 1), 0, 4096, dtype=jnp.int32)
    positions = offsets + jnp.arange(SEQ, dtype=jnp.int32)[None, :]
    return {
        "x": x,
        "mem": mem,
        "positions": positions,
        "norm_mix_g": gain(ks[3], (L, D_MODEL)),
        "w_in": w(ks[4], (L, D_MODEL, IN_COLS), D_MODEL),
        "conv_w": w(ks[5], (L, CONV_WIDTH, CONV_CH), CONV_WIDTH),
        "conv_b": 0.02 * jax.random.normal(ks[6], (L, CONV_CH), f32),
        "conv_ln_g": gain(ks[7], (L, CONV_CH)),
        "conv_ln_b": 0.02 * jax.random.normal(ks[8], (L, CONV_CH), f32),
        "w_conv_out": w(ks[9], (L, CONV_CH, D_MODEL), CONV_CH),
        "q_norm_g": gain(ks[10], (L, Q_LORA)),
        "w_uq": w(ks[11], (L, Q_LORA, MLA_HEADS * (QK_NOPE + QK_ROPE)), Q_LORA),
        "kv_norm_g": gain(ks[12], (L, KV_LORA)),
        "w_ukv": w(ks[13], (L, KV_LORA, MLA_HEADS * (QK_NOPE + V_DIM)), KV_LORA),
        "w_mla_out": w(ks[14], (L, MLA_HEADS * V_DIM, D_MODEL), MLA_HEADS * V_DIM),
        "w_out": w(ks[15], (L, D_MODEL, D_MODEL), D_MODEL),
        "norm_xattn_g": gain(ks[16], (L, D_MODEL)),
        "norm_mem_g": gain(ks[17], (L, D_MODEL)),
        "w_xq": w(ks[18], (L, D_MODEL, X_HEADS * X_HEAD_DIM), D_MODEL),
        "w_xkv": w(ks[19], (L, D_MODEL, 2 * X_HEADS * X_HEAD_DIM), D_MODEL),
        "w_xo": w(ks[20], (L, X_HEADS * X_HEAD_DIM, D_MODEL), X_HEADS * X_HEAD_DIM),
        "norm_mlp_g": gain(ks[21], (L, D_MODEL)),
        "w_mlp1": w(ks[22], (L, D_MODEL, D_FF), D_MODEL),
        "w_mlp2": w(ks[23], (L, D_FF, D_MODEL), D_FF),
        "final_norm_g": gain(ks[24], (D_MODEL,)),
    }


def reference(x, mem, positions, norm_mix_g, w_in, conv_w, conv_b, conv_ln_g, conv_ln_b,
              w_conv_out, q_norm_g, w_uq, kv_norm_g, w_ukv, w_mla_out, w_out,
              norm_xattn_g, norm_mem_g, w_xq, w_xkv, w_xo, norm_mlp_g, w_mlp1, w_mlp2,
              final_norm_g):
    cos, sin = rope_tables(positions)
    B, S, _ = x.shape
    cut = np.cumsum([2 * CONV_CH, Q_LORA, KV_LORA, QK_ROPE]).tolist()
    h = x
    for l in range(DEPTH):
        u = rms_norm(h, norm_mix_g[l])
        proj = u @ w_in[l]
        conv_in, c_q, c_kv, k_rope_raw, gate_logits = jnp.split(proj, cut, axis=-1)
        conv_out = conformer_conv(conv_in, conv_w[l], conv_b[l], conv_ln_g[l], conv_ln_b[l],
                                  w_conv_out[l])
        mla_out = mla_attention(c_q, c_kv, k_rope_raw, cos, sin, q_norm_g[l], w_uq[l],
                                kv_norm_g[l], w_ukv[l], w_mla_out[l])
        gates = jax.nn.sigmoid(gate_logits).reshape(B, S, N_BRANCH, D_MODEL)
        merged = gates[:, :, 0] * conv_out + gates[:, :, 1] * mla_out
        h = h + merged @ w_out[l]
        u = rms_norm(h, norm_xattn_g[l])
        mem_n = rms_norm(mem, norm_mem_g[l])
        h = h + memory_cross_attention(u, mem_n, w_xq[l], w_xkv[l], w_xo[l])
        u = rms_norm(h, norm_mlp_g[l])
        h = h + jnp.square(jax.nn.relu(u @ w_mlp1[l])) @ w_mlp2[l]
    return rms_norm(h, final_norm_g)
```

```python
import functools

import numpy as np
import jax
import jax.numpy as jnp
from jax import lax
from jax.experimental import pallas as pl
from jax.experimental.pallas import tpu as pltpu

D_MODEL = 1024
CONV_CH = D_MODEL // 2
CONV_WIDTH = 31
MLA_HEADS = 8
QK_NOPE = D_MODEL // 16
QK_ROPE = D_MODEL // 32
V_DIM = D_MODEL // 16
Q_LORA = 3 * D_MODEL // 8
KV_LORA = D_MODEL // 4
X_HEADS = 4
X_HEAD_DIM = D_MODEL // 8
D_FF = 4 * D_MODEL
ROPE_THETA = 10000.0
EPS = 1e-6

HEAD_PAD = 128
ROPE_HALF = QK_ROPE // 2
HALO = 32
TOKEN_TILE = 256
ROPE_TILE = 2048
MASK_VALUE = -1e30
VMEM_LIMIT = 56 * 1024 * 1024

F32 = jnp.float32
BF16 = jnp.bfloat16


def _rms(x, g):
    return x * lax.rsqrt(jnp.mean(x * x, axis=-1, keepdims=True) + EPS) * g


def _sigmoid(x):
    return 1.0 / (1.0 + jnp.exp(-x))


def _dot(a, b):
    return jnp.dot(a, b, preferred_element_type=F32)


def _dot_nt(a, b):
    return lax.dot_general(a, b, (((1,), (1,)), ((), ())), preferred_element_type=F32)


def _rope_kernel(pos_ref, invf_ref, cos_ref, sin_ref, nsin_ref):
    ang = pos_ref[...].astype(F32) * invf_ref[...]
    c = jnp.cos(ang)
    s = jnp.sin(ang)
    cos_ref[...] = c
    sin_ref[...] = s
    nsin_ref[...] = -s


def _rope_tables(pos_row, inv_freq_col):
    t = pos_row.shape[1]
    out = jax.ShapeDtypeStruct((ROPE_HALF, t), F32)
    spec = pl.BlockSpec((ROPE_HALF, ROPE_TILE), lambda i: (0, i))
    return pl.pallas_call(
        _rope_kernel,
        grid=(t // ROPE_TILE,),
        in_specs=[pl.BlockSpec((1, ROPE_TILE), lambda i: (0, i)),
                  pl.BlockSpec((ROPE_HALF, 1), lambda i: (0, 0))],
        out_specs=[spec, spec, spec],
        out_shape=[out, out, out],
        name="rope_tables",
    )(pos_row, inv_freq_col)


def _in_proj_kernel(x_ref, g_ref, wa_ref, wg_ref, wgate_ref, wcq_ref, wckv_ref, wkr_ref,
                    qg_ref, kvg_ref, wqt_ref, wk_ref, wvt_ref, place_ref,
                    cos_ref, sin_ref, cs_ref,
                    z_ref, gates_ref, qt_ref, k_ref, vt_ref):
    u = _rms(x_ref[...], g_ref[...]).astype(BF16)
    a = _dot(u, wa_ref[...])
    gt = _dot(u, wg_ref[...])
    z_ref[...] = (a * _sigmoid(gt)).astype(BF16)
    gates_ref[...] = _sigmoid(_dot(u, wgate_ref[...])).astype(BF16)

    cqn = _rms(_dot(u, wcq_ref[...]), qg_ref[...]).astype(BF16)
    qt = _dot_nt(wqt_ref[...], cqn)
    scale = (QK_NOPE + QK_ROPE) ** -0.5
    c = cos_ref[...]
    s = sin_ref[...]
    for h in range(MLA_HEADS):
        b = h * HEAD_PAD
        r1 = b + QK_NOPE
        r2 = r1 + ROPE_HALF
        r3 = r2 + ROPE_HALF
        t1 = qt[r1:r2]
        t2 = qt[r2:r3]
        qt_ref[b:r1, :] = (qt[b:r1] * scale).astype(BF16)
        qt_ref[r1:r2, :] = ((t1 * c - t2 * s) * scale).astype(BF16)
        qt_ref[r2:r3, :] = ((t2 * c + t1 * s) * scale).astype(BF16)
        qt_ref[r3:b + HEAD_PAD, :] = (qt[r3:b + HEAD_PAD] * scale).astype(BF16)

    ckvn = _rms(_dot(u, wckv_ref[...]), kvg_ref[...]).astype(BF16)
    t = _dot(u, wkr_ref[...]) * cs_ref[...]
    rot = t + pltpu.roll(t, HEAD_PAD - QK_ROPE, 1)
    k_ref[...] = (_dot(ckvn, wk_ref[...]) + _dot(rot.astype(BF16), place_ref[...])).astype(BF16)
    vt_ref[0] = _dot_nt(wvt_ref[...], ckvn).astype(BF16)


def _in_proj(x2, g, wa, wg, wgate, wcq, wckv, wkr, qg, kvg, wqt, wk, wvt, place, cos_t, sin_t, cs):
    t = x2.shape[0]
    tm = TOKEN_TILE
    n = t // tm
    full = lambda arr: pl.BlockSpec(arr.shape, lambda i: (0,) * arr.ndim)
    row = lambda w: pl.BlockSpec((tm, w), lambda i: (i, 0))
    in_specs = [row(D_MODEL), full(g), full(wa), full(wg), full(wgate), full(wcq), full(wckv),
                full(wkr), full(qg), full(kvg), full(wqt), full(wk), full(wvt), full(place),
                pl.BlockSpec((ROPE_HALF, tm), lambda i: (0, i)),
                pl.BlockSpec((ROPE_HALF, tm), lambda i: (0, i)),
                row(HEAD_PAD)]
    out_shape = [jax.ShapeDtypeStruct((t, CONV_CH), BF16),
                 jax.ShapeDtypeStruct((t, 2 * D_MODEL), BF16),
                 jax.ShapeDtypeStruct((MLA_HEADS * HEAD_PAD, t), BF16),
                 jax.ShapeDtypeStruct((t, MLA_HEADS * HEAD_PAD), BF16),
                 jax.ShapeDtypeStruct((n, MLA_HEADS * V_DIM, tm), BF16)]
    out_specs = [row(CONV_CH), row(2 * D_MODEL),
                 pl.BlockSpec((MLA_HEADS * HEAD_PAD, tm), lambda i: (0, i)),
                 row(MLA_HEADS * HEAD_PAD),
                 pl.BlockSpec((1, MLA_HEADS * V_DIM, tm), lambda i: (i, 0, 0))]
    return pl.pallas_call(
        _in_proj_kernel,
        grid=(n,),
        in_specs=in_specs,
        out_specs=out_specs,
        out_shape=out_shape,
        compiler_params=pltpu.CompilerParams(dimension_semantics=("parallel",),
                                             vmem_limit_bytes=VMEM_LIMIT),
        name="in_proj",
    )(x2, g, wa, wg, wgate, wcq, wckv, wkr, qg, kvg, wqt, wk, wvt, place, cos_t, sin_t, cs)


def _attn_kernel(qt_ref, k_ref, vt_ref, o_ref):
    qi = pl.program_id(2)
    tk = k_ref.shape[1]
    tq = qt_ref.shape[1]
    key_idx = lax.broadcasted_iota(jnp.int32, (tk, tq), 0)
    qry_idx = lax.broadcasted_iota(jnp.int32, (tk, tq), 1)
    causal = key_idx <= qry_idx

    outs = []
    for h in range(2):
        qt = qt_ref[h * HEAD_PAD:(h + 1) * HEAD_PAD, :]

        def step(kc, carry, masked):
            m, l, acc = carry
            k = k_ref[kc, :, h * HEAD_PAD:(h + 1) * HEAD_PAD]
            s = _dot(k, qt)
            if masked:
                s = jnp.where(causal, s, MASK_VALUE)
            m_new = jnp.maximum(m, jnp.max(s, axis=0, keepdims=True))
            alpha = jnp.exp(m - m_new)
            p = jnp.exp(s - m_new)
            l = alpha * l + jnp.sum(p, axis=0, keepdims=True)
            v = vt_ref[kc, h * V_DIM:(h + 1) * V_DIM, :]
            acc = alpha * acc + _dot(v, p.astype(BF16))
            return m_new, l, acc

        init = (jnp.full((1, tq), MASK_VALUE, F32), jnp.zeros((1, tq), F32),
                jnp.zeros((V_DIM, tq), F32))
        carry = step(qi, init, True)
        m, l, acc = lax.fori_loop(0, qi, lambda kc, c: step(kc, c, False), carry)
        outs.append(acc / l)
    o_ref[...] = jnp.concatenate(outs, axis=0).T.astype(BF16)


def _attention(qt, k3, vt3, batch, seq):
    tq = TOKEN_TILE
    nq = seq // tq
    nk = seq // TOKEN_TILE
    t = batch * seq
    pair = 2 * HEAD_PAD
    return pl.pallas_call(
        _attn_kernel,
        grid=(batch, MLA_HEADS // 2, nq),
        in_specs=[pl.BlockSpec((pair, tq), lambda b, hp, qi: (hp, b * nq + qi)),
                  pl.BlockSpec((nk, TOKEN_TILE, pair), lambda b, hp, qi: (b, 0, hp)),
                  pl.BlockSpec((nk, 2 * V_DIM, TOKEN_TILE), lambda b, hp, qi: (b, hp, 0))],
        out_specs=pl.BlockSpec((tq, 2 * V_DIM), lambda b, hp, qi: (b * nq + qi, hp)),
        out_shape=jax.ShapeDtypeStruct((t, MLA_HEADS * V_DIM), BF16),
        compiler_params=pltpu.CompilerParams(
            dimension_semantics=("parallel", "parallel", "arbitrary"),
            vmem_limit_bytes=VMEM_LIMIT),
        name="mla_attention",
    )(qt, k3, vt3)


def _mem_kernel(mem_ref, g_ref, wkt_ref, wv_ref, kt_ref, v_ref):
    mn = _rms(mem_ref[...], g_ref[...]).astype(BF16)
    kt_ref[0] = _dot_nt(wkt_ref[...], mn).astype(BF16)
    v_ref[0] = _dot(mn, wv_ref[...]).astype(BF16)


def _mem_kv(mem2, g, wkt, wv, batch, mem_len):
    xw = X_HEADS * X_HEAD_DIM
    full = lambda arr: pl.BlockSpec(arr.shape, lambda b: (0,) * arr.ndim)
    return pl.pallas_call(
        _mem_kernel,
        grid=(batch,),
        in_specs=[pl.BlockSpec((mem_len, D_MODEL), lambda b: (b, 0)), full(g), full(wkt), full(wv)],
        out_specs=[pl.BlockSpec((1, xw, mem_len), lambda b: (b, 0, 0)),
                   pl.BlockSpec((1, mem_len, xw), lambda b: (b, 0, 0))],
        out_shape=[jax.ShapeDtypeStruct((batch, xw, mem_len), BF16),
                   jax.ShapeDtypeStruct((batch, mem_len, xw), BF16)],
        compiler_params=pltpu.CompilerParams(dimension_semantics=("parallel",),
                                             vmem_limit_bytes=VMEM_LIMIT),
        name="mem_kv",
    )(mem2, g, wkt, wv)


def _mixer_kernel(tiles_per_seq, x_ref, z_ref, halo_ref, gates_ref, o_ref,
                  cw_ref, cb_ref, lng_ref, lnb_ref, wco_ref, wmo_ref, wout_ref,
                  xg_ref, wxq_ref, mkt_ref, mv_ref, wxo_ref,
                  h_ref, zext_ref, conv_ref):
    tm = x_ref.shape[0]
    first = (pl.program_id(0) % tiles_per_seq) == 0
    zext_ref[0:HALO, :] = jnp.where(first, 0.0, halo_ref[...].astype(F32))
    zext_ref[HALO:, :] = z_ref[...].astype(F32)

    rows = 64
    off = HALO - (CONV_WIDTH - 1)
    for c in range(CONV_CH // 128):
        cs = slice(c * 128, (c + 1) * 128)
        for r in range(0, tm, rows):
            acc = jnp.broadcast_to(cb_ref[:, cs], (rows, 128))
            for w in range(CONV_WIDTH):
                acc = acc + zext_ref[r + off + w:r + off + w + rows, cs] * cw_ref[w:w + 1, cs]
            conv_ref[r:r + rows, cs] = acc

    y = conv_ref[...]
    mu = jnp.mean(y, axis=-1, keepdims=True)
    yc = y - mu
    var = jnp.mean(yc * yc, axis=-1, keepdims=True)
    y = yc * lax.rsqrt(var + EPS) * lng_ref[...] + lnb_ref[...]
    y = y * _sigmoid(y)
    conv_out = _dot(y.astype(BF16), wco_ref[...])
    mla_out = _dot(o_ref[...], wmo_ref[...])
    gates = gates_ref[...].astype(F32)
    merged = gates[:, :D_MODEL] * conv_out + gates[:, D_MODEL:] * mla_out
    h1 = x_ref[...] + _dot(merged.astype(BF16), wout_ref[...])

    u = _rms(h1, xg_ref[...]).astype(BF16)
    q = (_dot(u, wxq_ref[...]) * (X_HEAD_DIM ** -0.5)).astype(BF16)
    heads = []
    for h in range(X_HEADS):
        hs = slice(h * X_HEAD_DIM, (h + 1) * X_HEAD_DIM)
        s = _dot(q[:, hs], mkt_ref[0, hs, :])
        p = jnp.exp(s - jnp.max(s, axis=-1, keepdims=True))
        p = p / jnp.sum(p, axis=-1, keepdims=True)
        heads.append(_dot(p.astype(BF16), mv_ref[0, :, hs]))
    xo = jnp.concatenate(heads, axis=1).astype(BF16)
    h_ref[...] = h1 + _dot(xo, wxo_ref[...])


def _mixer(x2, z, gates, o, cw, cb, lng, lnb, wco, wmo, wout, xg, wxq, mkt, mv, wxo, seq):
    t = x2.shape[0]
    tm = TOKEN_TILE
    tiles_per_seq = seq // tm
    halo_per_tile = tm // HALO
    mem_len = mv.shape[1]
    xw = X_HEADS * X_HEAD_DIM
    full = lambda arr: pl.BlockSpec(arr.shape, lambda i: (0,) * arr.ndim)
    row = lambda w: pl.BlockSpec((tm, w), lambda i: (i, 0))
    in_specs = [row(D_MODEL), row(CONV_CH),
                pl.BlockSpec((HALO, CONV_CH), lambda i: (jnp.maximum(i * halo_per_tile - 1, 0), 0)),
                row(2 * D_MODEL), row(MLA_HEADS * V_DIM),
                full(cw), full(cb), full(lng), full(lnb), full(wco), full(wmo), full(wout),
                full(xg), full(wxq),
                pl.BlockSpec((1, xw, mem_len), lambda i: (i // tiles_per_seq, 0, 0)),
                pl.BlockSpec((1, mem_len, xw), lambda i: (i // tiles_per_seq, 0, 0)),
                full(wxo)]
    return pl.pallas_call(
        functools.partial(_mixer_kernel, tiles_per_seq),
        grid=(t // tm,),
        in_specs=in_specs,
        out_specs=row(D_MODEL),
        out_shape=jax.ShapeDtypeStruct((t, D_MODEL), F32),
        scratch_shapes=[pltpu.VMEM((tm + HALO, CONV_CH), F32), pltpu.VMEM((tm, CONV_CH), F32)],
        compiler_params=pltpu.CompilerParams(dimension_semantics=("parallel",),
                                             vmem_limit_bytes=VMEM_LIMIT),
        name="mixer",
    )(x2, z, z, gates, o, cw, cb, lng, lnb, wco, wmo, wout, xg, wxq, mkt, mv, wxo)


def _mlp_kernel(h_ref, g_ref, w1_ref, w2_ref, fg_ref, y_ref):
    h = h_ref[...]
    u = _rms(h, g_ref[...]).astype(BF16)
    chunk = 1024
    acc = h
    for c in range(0, D_FF, chunk):
        a = jnp.maximum(_dot(u, w1_ref[:, c:c + chunk]), 0.0)
        acc = acc + _dot((a * a).astype(BF16), w2_ref[c:c + chunk, :])
    y_ref[...] = _rms(acc, fg_ref[...])


def _mlp(h, g, w1, w2, fg):
    t = h.shape[0]
    tm = TOKEN_TILE
    full = lambda arr: pl.BlockSpec(arr.shape, lambda i: (0,) * arr.ndim)
    row = pl.BlockSpec((tm, D_MODEL), lambda i: (i, 0))
    return pl.pallas_call(
        _mlp_kernel,
        grid=(t // tm,),
        in_specs=[row, full(g), full(w1), full(w2), full(fg)],
        out_specs=row,
        out_shape=jax.ShapeDtypeStruct((t, D_MODEL), F32),
        compiler_params=pltpu.CompilerParams(dimension_semantics=("parallel",),
                                             vmem_limit_bytes=VMEM_LIMIT),
        name="mlp",
    )(h, g, w1, w2, fg)


def _rope_placement():
    e = np.zeros((HEAD_PAD, MLA_HEADS * HEAD_PAD), np.float32)
    for h in range(MLA_HEADS):
        for j in range(QK_ROPE):
            e[j, h * HEAD_PAD + QK_NOPE + j] = 1.0
    return jnp.asarray(e, BF16)


def kernel(x, mem, positions, norm_mix_g, w_in, conv_w, conv_b, conv_ln_g, conv_ln_b, w_conv_out, q_norm_g, w_uq, kv_norm_g, w_ukv, w_mla_out, w_out, norm_xattn_g, norm_mem_g, w_xq, w_xkv, w_xo, norm_mlp_g, w_mlp1, w_mlp2, final_norm_g):
    batch, seq, _ = x.shape
    mem_len = mem.shape[1]
    depth = w_in.shape[0]
    t = batch * seq
    assert seq % TOKEN_TILE == 0 and t % ROPE_TILE == 0

    inv_freq = ROPE_THETA ** (-jnp.arange(ROPE_HALF, dtype=F32) / ROPE_HALF)
    cos_t, sin_t, nsin_t = _rope_tables(positions.reshape(1, t), inv_freq.reshape(ROPE_HALF, 1))
    cs = jnp.concatenate([cos_t, cos_t, nsin_t, sin_t,
                          jnp.zeros((HEAD_PAD - 2 * QK_ROPE, t), F32)], axis=0).T
    place = _rope_placement()
    row = lambda v: v.reshape(1, -1)

    h = x.reshape(t, D_MODEL)
    for l in range(depth):
        wi = w_in[l].astype(BF16)
        c0 = 2 * CONV_CH
        c1 = c0 + Q_LORA
        c2 = c1 + KV_LORA
        c3 = c2 + QK_ROPE
        wkr = jnp.concatenate([wi[:, c2:c3], wi[:, c2 + ROPE_HALF:c3], wi[:, c2:c2 + ROPE_HALF],
                               jnp.zeros((D_MODEL, HEAD_PAD - 2 * QK_ROPE), BF16)], axis=1)
        wuq = w_uq[l].astype(BF16).reshape(Q_LORA, MLA_HEADS, QK_NOPE + QK_ROPE)
        wqt = jnp.pad(wuq, ((0, 0), (0, 0), (0, HEAD_PAD - QK_NOPE - QK_ROPE))).reshape(Q_LORA, -1).T
        wukv = w_ukv[l].astype(BF16).reshape(KV_LORA, MLA_HEADS, QK_NOPE + V_DIM)
        wk = jnp.pad(wukv[:, :, :QK_NOPE], ((0, 0), (0, 0), (0, HEAD_PAD - QK_NOPE))).reshape(KV_LORA, -1)
        wvt = wukv[:, :, QK_NOPE:].reshape(KV_LORA, -1).T

        z, gates, qt, kp, vt3 = _in_proj(
            h, row(norm_mix_g[l]), wi[:, :CONV_CH], wi[:, CONV_CH:c0], wi[:, c3:], wi[:, c0:c1],
            wi[:, c1:c2], wkr, row(q_norm_g[l]), row(kv_norm_g[l]), wqt, wk, wvt, place,
            cos_t, sin_t, cs)

        k3 = kp.reshape(t // TOKEN_TILE, TOKEN_TILE, MLA_HEADS * HEAD_PAD)
        o = _attention(qt, k3, vt3, batch, seq)

        xw = X_HEADS * X_HEAD_DIM
        wxkv = w_xkv[l].astype(BF16)
        mkt, mv = _mem_kv(mem.reshape(batch * mem_len, D_MODEL), row(norm_mem_g[l]),
                          wxkv[:, :xw].T, wxkv[:, xw:], batch, mem_len)

        cw = jnp.pad(conv_w[l], ((0, HALO - CONV_WIDTH), (0, 0)))
        h = _mixer(h, z, gates, o, cw, row(conv_b[l]), row(conv_ln_g[l]), row(conv_ln_b[l]),
                   w_conv_out[l].astype(BF16), w_mla_out[l].astype(BF16), w_out[l].astype(BF16),
                   row(norm_xattn_g[l]), w_xq[l].astype(BF16), mkt, mv, w_xo[l].astype(BF16), seq)
        fg = row(final_norm_g) if l == depth - 1 else None
        assert fg is not None, "multi-layer stacks need an un-normalised MLP output"
        h = _mlp(h, row(norm_mlp_g[l]), w_mlp1[l].astype(BF16), w_mlp2[l].astype(BF16), fg)
    return h.reshape(batch, seq, D_MODEL)
```

```python
import functools

import numpy as np
import jax
import jax.numpy as jnp
from jax import lax
from jax.experimental import pallas as pl
from jax.experimental.pallas import tpu as pltpu

D_MODEL = 1024
CONV_CH = D_MODEL // 2
CONV_WIDTH = 31
MLA_HEADS = 8
QK_NOPE = D_MODEL // 16
QK_ROPE = D_MODEL // 32
V_DIM = D_MODEL // 16
Q_LORA = 3 * D_MODEL // 8
KV_LORA = D_MODEL // 4
X_HEADS = 4
X_HEAD_DIM = D_MODEL // 8
D_FF = 4 * D_MODEL
ROPE_THETA = 10000.0
EPS = 1e-6

HEAD_PAD = 128
ROPE_HALF = QK_ROPE // 2
HALO = 32
TOKEN_TILE = 256
Q_TILE = 256
KV_CHUNK = 2 * Q_TILE
ROPE_TILE = 2048
LOG2E = 1.4426950408889634
MASK_VALUE = -1e30
VMEM_LIMIT = 56 * 1024 * 1024

F32 = jnp.float32
BF16 = jnp.bfloat16


def _rms(x, g):
    return x * lax.rsqrt(jnp.mean(x * x, axis=-1, keepdims=True) + EPS) * g


def _sigmoid(x):
    return 1.0 / (1.0 + jnp.exp(-x))


def _dot(a, b):
    return jnp.dot(a, b, preferred_element_type=F32)


def _dot_nt(a, b):
    return lax.dot_general(a, b, (((1,), (1,)), ((), ())), preferred_element_type=F32)


def _rope_kernel(pos_ref, invf_ref, cos_ref, sin_ref, nsin_ref):
    ang = pos_ref[...].astype(F32) * invf_ref[...]
    c = jnp.cos(ang)
    s = jnp.sin(ang)
    cos_ref[...] = c
    sin_ref[...] = s
    nsin_ref[...] = -s


def _rope_tables(pos_row, inv_freq_col):
    t = pos_row.shape[1]
    out = jax.ShapeDtypeStruct((ROPE_HALF, t), F32)
    spec = pl.BlockSpec((ROPE_HALF, ROPE_TILE), lambda i: (0, i))
    return pl.pallas_call(
        _rope_kernel,
        grid=(t // ROPE_TILE,),
        in_specs=[pl.BlockSpec((1, ROPE_TILE), lambda i: (0, i)),
                  pl.BlockSpec((ROPE_HALF, 1), lambda i: (0, 0))],
        out_specs=[spec, spec, spec],
        out_shape=[out, out, out],
        name="rope_tables",
    )(pos_row, inv_freq_col)


def _in_proj_kernel(x_ref, g_ref, wa_ref, wg_ref, wgate_ref, wcq_ref, wckv_ref, wkr_ref,
                    qg_ref, kvg_ref, wqt_ref, wk_ref, wvt_ref, place_ref,
                    cos_ref, sin_ref, cs_ref,
                    z_ref, gates_ref, qt_ref, k_ref, vt_ref):
    u = _rms(x_ref[...], g_ref[...]).astype(BF16)
    a = _dot(u, wa_ref[...])
    gt = _dot(u, wg_ref[...])
    z_ref[...] = (a * _sigmoid(gt)).astype(BF16)
    gates_ref[...] = _sigmoid(_dot(u, wgate_ref[...])).astype(BF16)

    cqn = _rms(_dot(u, wcq_ref[...]), qg_ref[...]).astype(BF16)
    qt = _dot_nt(wqt_ref[...], cqn)
    scale = (QK_NOPE + QK_ROPE) ** -0.5 * LOG2E
    c = cos_ref[...]
    s = sin_ref[...]
    for h in range(MLA_HEADS):
        b = h * HEAD_PAD
        r1 = b + QK_NOPE
        r2 = r1 + ROPE_HALF
        r3 = r2 + ROPE_HALF
        t1 = qt[r1:r2]
        t2 = qt[r2:r3]
        qt_ref[b:r1, :] = (qt[b:r1] * scale).astype(BF16)
        qt_ref[r1:r2, :] = ((t1 * c - t2 * s) * scale).astype(BF16)
        qt_ref[r2:r3, :] = ((t2 * c + t1 * s) * scale).astype(BF16)
        qt_ref[r3:b + HEAD_PAD, :] = (qt[r3:b + HEAD_PAD] * scale).astype(BF16)

    ckvn = _rms(_dot(u, wckv_ref[...]), kvg_ref[...]).astype(BF16)
    t = _dot(u, wkr_ref[...]) * cs_ref[...]
    rot = t + pltpu.roll(t, HEAD_PAD - QK_ROPE, 1)
    k_ref[...] = (_dot(ckvn, wk_ref[...]) + _dot(rot.astype(BF16), place_ref[...])).astype(BF16)
    vt_ref[0] = _dot_nt(wvt_ref[...], ckvn).astype(BF16)


def _in_proj(x2, g, wa, wg, wgate, wcq, wckv, wkr, qg, kvg, wqt, wk, wvt, place, cos_t, sin_t, cs):
    t = x2.shape[0]
    tm = KV_CHUNK
    n = t // tm
    full = lambda arr: pl.BlockSpec(arr.shape, lambda i: (0,) * arr.ndim)
    row = lambda w: pl.BlockSpec((tm, w), lambda i: (i, 0))
    in_specs = [row(D_MODEL), full(g), full(wa), full(wg), full(wgate), full(wcq), full(wckv),
                full(wkr), full(qg), full(kvg), full(wqt), full(wk), full(wvt), full(place),
                pl.BlockSpec((ROPE_HALF, tm), lambda i: (0, i)),
                pl.BlockSpec((ROPE_HALF, tm), lambda i: (0, i)),
                row(HEAD_PAD)]
    out_shape = [jax.ShapeDtypeStruct((t, CONV_CH), BF16),
                 jax.ShapeDtypeStruct((t, 2 * D_MODEL), BF16),
                 jax.ShapeDtypeStruct((MLA_HEADS * HEAD_PAD, t), BF16),
                 jax.ShapeDtypeStruct((t, MLA_HEADS * HEAD_PAD), BF16),
                 jax.ShapeDtypeStruct((n, MLA_HEADS * V_DIM, tm), BF16)]
    out_specs = [row(CONV_CH), row(2 * D_MODEL),
                 pl.BlockSpec((MLA_HEADS * HEAD_PAD, tm), lambda i: (0, i)),
                 row(MLA_HEADS * HEAD_PAD),
                 pl.BlockSpec((1, MLA_HEADS * V_DIM, tm), lambda i: (i, 0, 0))]
    return pl.pallas_call(
        _in_proj_kernel,
        grid=(n,),
        in_specs=in_specs,
        out_specs=out_specs,
        out_shape=out_shape,
        compiler_params=pltpu.CompilerParams(dimension_semantics=("parallel",),
                                             vmem_limit_bytes=VMEM_LIMIT),
        name="in_proj",
    )(x2, g, wa, wg, wgate, wcq, wckv, wkr, qg, kvg, wqt, wk, wvt, place, cos_t, sin_t, cs)


def _attn_kernel(qt_ref, k_ref, vt_ref, o_ref):
    qi = pl.program_id(2)
    tk = k_ref.shape[1]
    tq = qt_ref.shape[1]
    chunks_per_q = tk // tq
    diag = qi // chunks_per_q
    key_idx = lax.broadcasted_iota(jnp.int32, (tk, tq), 0)
    qry_idx = lax.broadcasted_iota(jnp.int32, (tk, tq), 1) + (qi % chunks_per_q) * tq
    causal = key_idx <= qry_idx
    qts = [qt_ref[h * HEAD_PAD:(h + 1) * HEAD_PAD, :] for h in range(2)]

    def scores(kc):
        return tuple(_dot(k_ref[kc, :, h * HEAD_PAD:(h + 1) * HEAD_PAD], qts[h]) for h in range(2))

    def local_softmax(s, masked):
        out = []
        for h in range(2):
            sh = jnp.where(causal, s[h], MASK_VALUE) if masked else s[h]
            m = jnp.max(sh, axis=0, keepdims=True)
            p = jnp.exp2(sh - m)
            out.append((m, jnp.sum(p, axis=0, keepdims=True), p.astype(BF16)))
        return tuple(out)

    def accumulate(state, kc, local):
        new = []
        for h in range(2):
            m, l, acc = state[h]
            mj, lj, pj = local[h]
            pv = _dot(vt_ref[kc, h * V_DIM:(h + 1) * V_DIM, :], pj)
            m_new = jnp.maximum(m, mj)
            a = jnp.exp2(m - m_new)
            b = jnp.exp2(mj - m_new)
            new.append((m_new, a * l + b * lj, a * acc + b * pv))
        return tuple(new)

    def body(j, carry):
        state, pending, pending_kc, s_next = carry
        state = accumulate(state, pending_kc, pending)
        local = local_softmax(s_next, False)
        s_next = scores(jnp.minimum(j + 1, jnp.maximum(diag - 1, 0)))
        return state, local, j, s_next

    init = (jnp.full((1, tq), MASK_VALUE, F32), jnp.zeros((1, tq), F32),
            jnp.zeros((V_DIM, tq), F32))
    carry = ((init, init), local_softmax(scores(diag), True), diag, scores(0))
    state, pending, pending_kc, _ = lax.fori_loop(0, diag, body, carry)
    state = accumulate(state, pending_kc, pending)
    outs = [acc / l for (_, l, acc) in state]
    o_ref[...] = jnp.concatenate(outs, axis=0).T.astype(BF16)


def _attention(qt, k3, vt3, batch, seq):
    tq = Q_TILE
    nq = seq // tq
    nk = seq // KV_CHUNK
    t = batch * seq
    pair = 2 * HEAD_PAD
    return pl.pallas_call(
        _attn_kernel,
        grid=(batch, MLA_HEADS // 2, nq),
        in_specs=[pl.BlockSpec((pair, tq), lambda b, hp, qi: (hp, b * nq + qi)),
                  pl.BlockSpec((nk, KV_CHUNK, pair), lambda b, hp, qi: (b, 0, hp)),
                  pl.BlockSpec((nk, 2 * V_DIM, KV_CHUNK), lambda b, hp, qi: (b, hp, 0))],
        out_specs=pl.BlockSpec((tq, 2 * V_DIM), lambda b, hp, qi: (b * nq + qi, hp)),
        out_shape=jax.ShapeDtypeStruct((t, MLA_HEADS * V_DIM), BF16),
        compiler_params=pltpu.CompilerParams(
            dimension_semantics=("parallel", "parallel", "arbitrary"),
            vmem_limit_bytes=VMEM_LIMIT),
        name="mla_attention",
    )(qt, k3, vt3)


def _mem_kernel(mem_ref, g_ref, wkt_ref, wv_ref, kt_ref, v_ref):
    mn = _rms(mem_ref[...], g_ref[...]).astype(BF16)
    kt_ref[0] = _dot_nt(wkt_ref[...], mn).astype(BF16)
    v_ref[0] = _dot(mn, wv_ref[...]).astype(BF16)


def _mem_kv(mem2, g, wkt, wv, batch, mem_len):
    xw = X_HEADS * X_HEAD_DIM
    full = lambda arr: pl.BlockSpec(arr.shape, lambda b: (0,) * arr.ndim)
    return pl.pallas_call(
        _mem_kernel,
        grid=(batch,),
        in_specs=[pl.BlockSpec((mem_len, D_MODEL), lambda b: (b, 0)), full(g), full(wkt), full(wv)],
        out_specs=[pl.BlockSpec((1, xw, mem_len), lambda b: (b, 0, 0)),
                   pl.BlockSpec((1, mem_len, xw), lambda b: (b, 0, 0))],
        out_shape=[jax.ShapeDtypeStruct((batch, xw, mem_len), BF16),
                   jax.ShapeDtypeStruct((batch, mem_len, xw), BF16)],
        compiler_params=pltpu.CompilerParams(dimension_semantics=("parallel",),
                                             vmem_limit_bytes=VMEM_LIMIT),
        name="mem_kv",
    )(mem2, g, wkt, wv)


def _mixer_kernel(tiles_per_seq, x_ref, z_ref, halo_ref, gates_ref, o_ref,
                  cw_ref, cb_ref, lng_ref, lnb_ref, wco_ref, wmo_ref, wout_ref,
                  xg_ref, wxq_ref, mkt_ref, mv_ref, wxo_ref,
                  h_ref, zext_ref, conv_ref):
    tm = x_ref.shape[0]
    first = (pl.program_id(0) % tiles_per_seq) == 0
    zext_ref[0:HALO, :] = jnp.where(first, 0.0, halo_ref[...].astype(F32))
    zext_ref[HALO:, :] = z_ref[...].astype(F32)

    rows = 64
    off = HALO - (CONV_WIDTH - 1)
    for c in range(CONV_CH // 128):
        cs = slice(c * 128, (c + 1) * 128)
        for r in range(0, tm, rows):
            acc = jnp.broadcast_to(cb_ref[:, cs], (rows, 128))
            for w in range(CONV_WIDTH):
                acc = acc + zext_ref[r + off + w:r + off + w + rows, cs] * cw_ref[w:w + 1, cs]
            conv_ref[r:r + rows, cs] = acc

    y = conv_ref[...]
    mu = jnp.mean(y, axis=-1, keepdims=True)
    yc = y - mu
    var = jnp.mean(yc * yc, axis=-1, keepdims=True)
    y = yc * lax.rsqrt(var + EPS) * lng_ref[...] + lnb_ref[...]
    y = y * _sigmoid(y)
    conv_out = _dot(y.astype(BF16), wco_ref[...])
    mla_out = _dot(o_ref[...], wmo_ref[...])
    gates = gates_ref[...].astype(F32)
    merged = gates[:, :D_MODEL] * conv_out + gates[:, D_MODEL:] * mla_out
    h1 = x_ref[...] + _dot(merged.astype(BF16), wout_ref[...])

    u = _rms(h1, xg_ref[...]).astype(BF16)
    q = (_dot(u, wxq_ref[...]) * (X_HEAD_DIM ** -0.5)).astype(BF16)
    heads = []
    for h in range(X_HEADS):
        hs = slice(h * X_HEAD_DIM, (h + 1) * X_HEAD_DIM)
        s = _dot(q[:, hs], mkt_ref[0, hs, :])
        p = jnp.exp(s - jnp.max(s, axis=-1, keepdims=True))
        p = p / jnp.sum(p, axis=-1, keepdims=True)
        heads.append(_dot(p.astype(BF16), mv_ref[0, :, hs]))
    xo = jnp.concatenate(heads, axis=1).astype(BF16)
    h_ref[...] = h1 + _dot(xo, wxo_ref[...])


def _mixer(x2, z, gates, o, cw, cb, lng, lnb, wco, wmo, wout, xg, wxq, mkt, mv, wxo, seq):
    t = x2.shape[0]
    tm = TOKEN_TILE
    tiles_per_seq = seq // tm
    halo_per_tile = tm // HALO
    mem_len = mv.shape[1]
    xw = X_HEADS * X_HEAD_DIM
    full = lambda arr: pl.BlockSpec(arr.shape, lambda i: (0,) * arr.ndim)
    row = lambda w: pl.BlockSpec((tm, w), lambda i: (i, 0))
    in_specs = [row(D_MODEL), row(CONV_CH),
                pl.BlockSpec((HALO, CONV_CH), lambda i: (jnp.maximum(i * halo_per_tile - 1, 0), 0)),
                row(2 * D_MODEL), row(MLA_HEADS * V_DIM),
                full(cw), full(cb), full(lng), full(lnb), full(wco), full(wmo), full(wout),
                full(xg), full(wxq),
                pl.BlockSpec((1, xw, mem_len), lambda i: (i // tiles_per_seq, 0, 0)),
                pl.BlockSpec((1, mem_len, xw), lambda i: (i // tiles_per_seq, 0, 0)),
                full(wxo)]
    return pl.pallas_call(
        functools.partial(_mixer_kernel, tiles_per_seq),
        grid=(t // tm,),
        in_specs=in_specs,
        out_specs=row(D_MODEL),
        out_shape=jax.ShapeDtypeStruct((t, D_MODEL), F32),
        scratch_shapes=[pltpu.VMEM((tm + HALO, CONV_CH), F32), pltpu.VMEM((tm, CONV_CH), F32)],
        compiler_params=pltpu.CompilerParams(dimension_semantics=("parallel",),
                                             vmem_limit_bytes=VMEM_LIMIT),
        name="mixer",
    )(x2, z, z, gates, o, cw, cb, lng, lnb, wco, wmo, wout, xg, wxq, mkt, mv, wxo)


def _mlp_kernel(h_ref, g_ref, w1_ref, w2_ref, fg_ref, y_ref):
    h = h_ref[...]
    u = _rms(h, g_ref[...]).astype(BF16)
    chunk = 1024
    acc = h
    for c in range(0, D_FF, chunk):
        a = jnp.maximum(_dot(u, w1_ref[:, c:c + chunk]), 0.0)
        acc = acc + _dot((a * a).astype(BF16), w2_ref[c:c + chunk, :])
    y_ref[...] = _rms(acc, fg_ref[...])


def _mlp(h, g, w1, w2, fg):
    t = h.shape[0]
    tm = TOKEN_TILE
    full = lambda arr: pl.BlockSpec(arr.shape, lambda i: (0,) * arr.ndim)
    row = pl.BlockSpec((tm, D_MODEL), lambda i: (i, 0))
    return pl.pallas_call(
        _mlp_kernel,
        grid=(t // tm,),
        in_specs=[row, full(g), full(w1), full(w2), full(fg)],
        out_specs=row,
        out_shape=jax.ShapeDtypeStruct((t, D_MODEL), F32),
        compiler_params=pltpu.CompilerParams(dimension_semantics=("parallel",),
                                             vmem_limit_bytes=VMEM_LIMIT),
        name="mlp",
    )(h, g, w1, w2, fg)


def _rope_placement():
    e = np.zeros((HEAD_PAD, MLA_HEADS * HEAD_PAD), np.float32)
    for h in range(MLA_HEADS):
        for j in range(QK_ROPE):
            e[j, h * HEAD_PAD + QK_NOPE + j] = 1.0
    return jnp.asarray(e, BF16)


def kernel(x, mem, positions, norm_mix_g, w_in, conv_w, conv_b, conv_ln_g, conv_ln_b, w_conv_out, q_norm_g, w_uq, kv_norm_g, w_ukv, w_mla_out, w_out, norm_xattn_g, norm_mem_g, w_xq, w_xkv, w_xo, norm_mlp_g, w_mlp1, w_mlp2, final_norm_g):
    batch, seq, _ = x.shape
    mem_len = mem.shape[1]
    t = batch * seq
    assert w_in.shape[0] == 1, "single-layer block (the final rmsnorm is fused into the MLP kernel)"
    assert seq % KV_CHUNK == 0 and seq % TOKEN_TILE == 0 and t % ROPE_TILE == 0

    inv_freq = ROPE_THETA ** (-jnp.arange(ROPE_HALF, dtype=F32) / ROPE_HALF)
    cos_t, sin_t, nsin_t = _rope_tables(positions.reshape(1, t), inv_freq.reshape(ROPE_HALF, 1))
    cs = jnp.concatenate([cos_t, cos_t, nsin_t, sin_t,
                          jnp.zeros((HEAD_PAD - 2 * QK_ROPE, t), F32)], axis=0).T
    place = _rope_placement()
    row = lambda v: v.reshape(1, -1)

    h = x.reshape(t, D_MODEL)
    wi = w_in[0].astype(BF16)
    c0 = 2 * CONV_CH
    c1 = c0 + Q_LORA
    c2 = c1 + KV_LORA
    c3 = c2 + QK_ROPE
    wkr = jnp.concatenate([wi[:, c2:c3], wi[:, c2 + ROPE_HALF:c3], wi[:, c2:c2 + ROPE_HALF],
                           jnp.zeros((D_MODEL, HEAD_PAD - 2 * QK_ROPE), BF16)], axis=1)
    wuq = w_uq[0].astype(BF16).reshape(Q_LORA, MLA_HEADS, QK_NOPE + QK_ROPE)
    wqt = jnp.pad(wuq, ((0, 0), (0, 0), (0, HEAD_PAD - QK_NOPE - QK_ROPE))).reshape(Q_LORA, -1).T
    wukv = w_ukv[0].astype(BF16).reshape(KV_LORA, MLA_HEADS, QK_NOPE + V_DIM)
    wk = jnp.pad(wukv[:, :, :QK_NOPE], ((0, 0), (0, 0), (0, HEAD_PAD - QK_NOPE))).reshape(KV_LORA, -1)
    wvt = wukv[:, :, QK_NOPE:].reshape(KV_LORA, -1).T

    z, gates, qt, kp, vt3 = _in_proj(
        h, row(norm_mix_g[0]), wi[:, :CONV_CH], wi[:, CONV_CH:c0], wi[:, c3:], wi[:, c0:c1],
        wi[:, c1:c2], wkr, row(q_norm_g[0]), row(kv_norm_g[0]), wqt, wk, wvt, place,
        cos_t, sin_t, cs)

    k3 = kp.reshape(t // KV_CHUNK, KV_CHUNK, MLA_HEADS * HEAD_PAD)
    o = _attention(qt, k3, vt3, batch, seq)

    xw = X_HEADS * X_HEAD_DIM
    wxkv = w_xkv[0].astype(BF16)
    mkt, mv = _mem_kv(mem.reshape(batch * mem_len, D_MODEL), row(norm_mem_g[0]),
                      wxkv[:, :xw].T, wxkv[:, xw:], batch, mem_len)

    cw = jnp.pad(conv_w[0], ((0, HALO - CONV_WIDTH), (0, 0)))
    h = _mixer(h, z, gates, o, cw, row(conv_b[0]), row(conv_ln_g[0]), row(conv_ln_b[0]),
               w_conv_out[0].astype(BF16), w_mla_out[0].astype(BF16), w_out[0].astype(BF16),
               row(norm_xattn_g[0]), w_xq[0].astype(BF16), mkt, mv, w_xo[0].astype(BF16), seq)
    h = _mlp(h, row(norm_mlp_g[0]), w_mlp1[0].astype(BF16), w_mlp2[0].astype(BF16),
             row(final_norm_g))
    return h.reshape(batch, seq, D_MODEL)
```

```python
import functools

import numpy as np
import jax
import jax.numpy as jnp
from jax import lax
from jax.experimental import pallas as pl
from jax.experimental.pallas import tpu as pltpu

D_MODEL = 1024
CONV_CH = D_MODEL // 2
CONV_WIDTH = 31
MLA_HEADS = 8
QK_NOPE = D_MODEL // 16
QK_ROPE = D_MODEL // 32
V_DIM = D_MODEL // 16
Q_LORA = 3 * D_MODEL // 8
KV_LORA = D_MODEL // 4
X_HEADS = 4
X_HEAD_DIM = D_MODEL // 8
D_FF = 4 * D_MODEL
ROPE_THETA = 10000.0
EPS = 1e-6

HEAD_PAD = 128
V_ROWS = V_DIM + 16
ATTN_HEADS = 4
ROPE_HALF = QK_ROPE // 2
HALO = 32
TOKEN_TILE = 256
Q_TILE = 256
KV_CHUNK = 2 * Q_TILE
ROPE_TILE = 2048
LOG2E = 1.4426950408889634
MASK_VALUE = -1e30
VMEM_LIMIT = 56 * 1024 * 1024

F32 = jnp.float32
BF16 = jnp.bfloat16


def _rms(x, g):
    return x * lax.rsqrt(jnp.mean(x * x, axis=-1, keepdims=True) + EPS) * g


def _sigmoid(x):
    return 1.0 / (1.0 + jnp.exp(-x))


def _dot(a, b):
    return jnp.dot(a, b, preferred_element_type=F32)


def _dot_nt(a, b):
    return lax.dot_general(a, b, (((1,), (1,)), ((), ())), preferred_element_type=F32)


def _rope_kernel(pos_ref, invf_ref, cos_ref, sin_ref, nsin_ref):
    ang = pos_ref[...].astype(F32) * invf_ref[...]
    c = jnp.cos(ang)
    s = jnp.sin(ang)
    cos_ref[...] = c
    sin_ref[...] = s
    nsin_ref[...] = -s


def _rope_tables(pos_row, inv_freq_col):
    t = pos_row.shape[1]
    out = jax.ShapeDtypeStruct((ROPE_HALF, t), F32)
    spec = pl.BlockSpec((ROPE_HALF, ROPE_TILE), lambda i: (0, i))
    return pl.pallas_call(
        _rope_kernel,
        grid=(t // ROPE_TILE,),
        in_specs=[pl.BlockSpec((1, ROPE_TILE), lambda i: (0, i)),
                  pl.BlockSpec((ROPE_HALF, 1), lambda i: (0, 0))],
        out_specs=[spec, spec, spec],
        out_shape=[out, out, out],
        name="rope_tables",
    )(pos_row, inv_freq_col)


def _in_proj_kernel(x_ref, g_ref, wa_ref, wg_ref, wgate_ref, wcq_ref, wckv_ref, wkr_ref,
                    qg_ref, kvg_ref, wqt_ref, wk_ref, wvt_ref, ones_ref, place_ref,
                    cos_ref, sin_ref, cs_ref,
                    z_ref, gates_ref, qt_ref, k_ref, vt_ref):
    u = _rms(x_ref[...], g_ref[...]).astype(BF16)
    a = _dot(u, wa_ref[...])
    gt = _dot(u, wg_ref[...])
    z_ref[...] = (a * _sigmoid(gt)).astype(BF16)
    gates_ref[...] = _sigmoid(_dot(u, wgate_ref[...])).astype(BF16)

    cqn = _rms(_dot(u, wcq_ref[...]), qg_ref[...]).astype(BF16)
    qt = _dot_nt(wqt_ref[...], cqn)
    scale = (QK_NOPE + QK_ROPE) ** -0.5 * LOG2E
    c = cos_ref[...]
    s = sin_ref[...]
    for h in range(MLA_HEADS):
        b = h * HEAD_PAD
        r1 = b + QK_NOPE
        r2 = r1 + ROPE_HALF
        r3 = r2 + ROPE_HALF
        t1 = qt[r1:r2]
        t2 = qt[r2:r3]
        qt_ref[b:r1, :] = (qt[b:r1] * scale).astype(BF16)
        qt_ref[r1:r2, :] = ((t1 * c - t2 * s) * scale).astype(BF16)
        qt_ref[r2:r3, :] = ((t2 * c + t1 * s) * scale).astype(BF16)
        qt_ref[r3:b + HEAD_PAD, :] = (qt[r3:b + HEAD_PAD] * scale).astype(BF16)

    ckvn = _rms(_dot(u, wckv_ref[...]), kvg_ref[...]).astype(BF16)
    t = _dot(u, wkr_ref[...]) * cs_ref[...]
    rot = t + pltpu.roll(t, HEAD_PAD - QK_ROPE, 1)
    k_ref[...] = (_dot(ckvn, wk_ref[...]) + _dot(rot.astype(BF16), place_ref[...])).astype(BF16)
    vt_ref[0] = (_dot_nt(wvt_ref[...], ckvn) + ones_ref[...]).astype(BF16)


def _in_proj(x2, g, wa, wg, wgate, wcq, wckv, wkr, qg, kvg, wqt, wk, wvt, ones, place, cos_t, sin_t, cs):
    t = x2.shape[0]
    tm = KV_CHUNK
    n = t // tm
    full = lambda arr: pl.BlockSpec(arr.shape, lambda i: (0,) * arr.ndim)
    row = lambda w: pl.BlockSpec((tm, w), lambda i: (i, 0))
    in_specs = [row(D_MODEL), full(g), full(wa), full(wg), full(wgate), full(wcq), full(wckv),
                full(wkr), full(qg), full(kvg), full(wqt), full(wk), full(wvt), full(ones), full(place),
                pl.BlockSpec((ROPE_HALF, tm), lambda i: (0, i)),
                pl.BlockSpec((ROPE_HALF, tm), lambda i: (0, i)),
                row(HEAD_PAD)]
    out_shape = [jax.ShapeDtypeStruct((t, CONV_CH), BF16),
                 jax.ShapeDtypeStruct((t, 2 * D_MODEL), BF16),
                 jax.ShapeDtypeStruct((MLA_HEADS * HEAD_PAD, t), BF16),
                 jax.ShapeDtypeStruct((t, MLA_HEADS * HEAD_PAD), BF16),
                 jax.ShapeDtypeStruct((n, MLA_HEADS * V_ROWS, tm), BF16)]
    out_specs = [row(CONV_CH), row(2 * D_MODEL),
                 pl.BlockSpec((MLA_HEADS * HEAD_PAD, tm), lambda i: (0, i)),
                 row(MLA_HEADS * HEAD_PAD),
                 pl.BlockSpec((1, MLA_HEADS * V_ROWS, tm), lambda i: (i, 0, 0))]
    return pl.pallas_call(
        _in_proj_kernel,
        grid=(n,),
        in_specs=in_specs,
        out_specs=out_specs,
        out_shape=out_shape,
        compiler_params=pltpu.CompilerParams(dimension_semantics=("parallel",),
                                             vmem_limit_bytes=VMEM_LIMIT),
        name="in_proj",
    )(x2, g, wa, wg, wgate, wcq, wckv, wkr, qg, kvg, wqt, wk, wvt, ones, place, cos_t, sin_t, cs)


def _attn_kernel(qt_ref, k_ref, vt_ref, o_ref, s_scr, p_scr, acc_scr):
    qi = pl.program_id(2)
    tk = k_ref.shape[1]
    tq = qt_ref.shape[1]
    heads = range(ATTN_HEADS)
    chunks_per_q = tk // tq
    diag = qi // chunks_per_q
    key_idx = lax.broadcasted_iota(jnp.int32, (tk, tq), 0)
    qry_idx = lax.broadcasted_iota(jnp.int32, (tk, tq), 1) + (qi % chunks_per_q) * tq
    causal = key_idx <= qry_idx
    qts = [qt_ref[h * HEAD_PAD:(h + 1) * HEAD_PAD, :] for h in heads]

    def scores_to(slot, kc, masked):
        ms = []
        for h in heads:
            s = _dot(k_ref[kc, :, h * HEAD_PAD:(h + 1) * HEAD_PAD], qts[h])
            if masked:
                s = jnp.where(causal, s, MASK_VALUE)
            s_scr[slot, h] = s
            ms.append(jnp.max(s, axis=0, keepdims=True))
        return tuple(ms)

    def softmax_to(src, dst, ms):
        for h in heads:
            p_scr[dst, h] = jnp.exp2(s_scr[src, h] - ms[h]).astype(BF16)

    def accumulate(m_run, slot, kc, ms):
        new = []
        for h in heads:
            pv = _dot(vt_ref[kc, h * V_ROWS:(h + 1) * V_ROWS, :], p_scr[slot, h])
            m_new = jnp.maximum(m_run[h], ms[h])
            a = jnp.exp2(m_run[h] - m_new)
            b = jnp.exp2(ms[h] - m_new)
            acc_scr[h] = a * acc_scr[h] + b * pv
            new.append(m_new)
        return tuple(new)

    def step(cur, j, carry):
        m_run, pend_kc, pend_m, next_m = carry
        m_run = accumulate(m_run, 1 - cur, pend_kc, pend_m)
        softmax_to(1 - cur, cur, next_m)
        ahead_m = scores_to(cur, jnp.minimum(j + 1, jnp.maximum(diag - 1, 0)), False)
        return m_run, j, next_m, ahead_m

    def body(j, carry):
        return lax.cond(j % 2 == 0, functools.partial(step, 0), functools.partial(step, 1), j, carry)

    acc_scr[...] = jnp.zeros(acc_scr.shape, F32)
    m_init = tuple(jnp.full((1, tq), MASK_VALUE, F32) for _ in heads)
    diag_m = scores_to(0, diag, True)
    softmax_to(0, 1, diag_m)
    carry = (m_init, diag, diag_m, scores_to(1, 0, False))
    m_run, pend_kc, pend_m, _ = lax.fori_loop(0, diag, body, carry)
    accumulate(m_run, (diag + 1) % 2, pend_kc, pend_m)
    outs = [acc_scr[h, :V_DIM, :] / acc_scr[h, V_DIM:V_DIM + 1, :] for h in heads]
    o_ref[...] = jnp.concatenate(outs, axis=0).T.astype(BF16)


def _attention(qt, k3, vt3, batch, seq):
    tq = Q_TILE
    nq = seq // tq
    nk = seq // KV_CHUNK
    t = batch * seq
    return pl.pallas_call(
        _attn_kernel,
        grid=(batch, MLA_HEADS // ATTN_HEADS, nq),
        in_specs=[pl.BlockSpec((ATTN_HEADS * HEAD_PAD, tq), lambda b, hg, qi: (hg, b * nq + qi)),
                  pl.BlockSpec((nk, KV_CHUNK, ATTN_HEADS * HEAD_PAD), lambda b, hg, qi: (b, 0, hg)),
                  pl.BlockSpec((nk, ATTN_HEADS * V_ROWS, KV_CHUNK), lambda b, hg, qi: (b, hg, 0))],
        out_specs=pl.BlockSpec((tq, ATTN_HEADS * V_DIM), lambda b, hg, qi: (b * nq + qi, hg)),
        out_shape=jax.ShapeDtypeStruct((t, MLA_HEADS * V_DIM), BF16),
        scratch_shapes=[pltpu.VMEM((2, ATTN_HEADS, KV_CHUNK, tq), F32),
                        pltpu.VMEM((2, ATTN_HEADS, KV_CHUNK, tq), BF16),
                        pltpu.VMEM((ATTN_HEADS, V_ROWS, tq), F32)],
        compiler_params=pltpu.CompilerParams(
            dimension_semantics=("parallel", "parallel", "arbitrary"),
            vmem_limit_bytes=VMEM_LIMIT),
        name="mla_attention",
    )(qt, k3, vt3)


def _mem_kernel(mem_ref, g_ref, wkt_ref, wv_ref, kt_ref, v_ref):
    mn = _rms(mem_ref[...], g_ref[...]).astype(BF16)
    kt_ref[0] = _dot_nt(wkt_ref[...], mn).astype(BF16)
    v_ref[0] = _dot(mn, wv_ref[...]).astype(BF16)


def _mem_kv(mem2, g, wkt, wv, batch, mem_len):
    xw = X_HEADS * X_HEAD_DIM
    full = lambda arr: pl.BlockSpec(arr.shape, lambda b: (0,) * arr.ndim)
    return pl.pallas_call(
        _mem_kernel,
        grid=(batch,),
        in_specs=[pl.BlockSpec((mem_len, D_MODEL), lambda b: (b, 0)), full(g), full(wkt), full(wv)],
        out_specs=[pl.BlockSpec((1, xw, mem_len), lambda b: (b, 0, 0)),
                   pl.BlockSpec((1, mem_len, xw), lambda b: (b, 0, 0))],
        out_shape=[jax.ShapeDtypeStruct((batch, xw, mem_len), BF16),
                   jax.ShapeDtypeStruct((batch, mem_len, xw), BF16)],
        compiler_params=pltpu.CompilerParams(dimension_semantics=("parallel",),
                                             vmem_limit_bytes=VMEM_LIMIT),
        name="mem_kv",
    )(mem2, g, wkt, wv)


def _mixer_kernel(tiles_per_seq, x_ref, z_ref, halo_ref, gates_ref, o_ref,
                  cw_ref, cb_ref, lng_ref, lnb_ref, wco_ref, wmo_ref, wout_ref,
                  xg_ref, wxq_ref, mkt_ref, mv_ref, wxo_ref,
                  h_ref, zext_ref, conv_ref):
    tm = x_ref.shape[0]
    first = (pl.program_id(0) % tiles_per_seq) == 0
    zext_ref[0:HALO, :] = jnp.where(first, 0.0, halo_ref[...].astype(F32))
    zext_ref[HALO:, :] = z_ref[...].astype(F32)

    rows = 64
    off = HALO - (CONV_WIDTH - 1)
    for c in range(CONV_CH // 128):
        cs = slice(c * 128, (c + 1) * 128)
        for r in range(0, tm, rows):
            acc = jnp.broadcast_to(cb_ref[:, cs], (rows, 128))
            for w in range(CONV_WIDTH):
                acc = acc + zext_ref[r + off + w:r + off + w + rows, cs] * cw_ref[w:w + 1, cs]
            conv_ref[r:r + rows, cs] = acc

    y = conv_ref[...]
    mu = jnp.mean(y, axis=-1, keepdims=True)
    yc = y - mu
    var = jnp.mean(yc * yc, axis=-1, keepdims=True)
    y = yc * lax.rsqrt(var + EPS) * lng_ref[...] + lnb_ref[...]
    y = y * _sigmoid(y)
    conv_out = _dot(y.astype(BF16), wco_ref[...])
    mla_out = _dot(o_ref[...], wmo_ref[...])
    gates = gates_ref[...].astype(F32)
    merged = gates[:, :D_MODEL] * conv_out + gates[:, D_MODEL:] * mla_out
    h1 = x_ref[...] + _dot(merged.astype(BF16), wout_ref[...])

    u = _rms(h1, xg_ref[...]).astype(BF16)
    q = (_dot(u, wxq_ref[...]) * (X_HEAD_DIM ** -0.5)).astype(BF16)
    heads = []
    for h in range(X_HEADS):
        hs = slice(h * X_HEAD_DIM, (h + 1) * X_HEAD_DIM)
        s = _dot(q[:, hs], mkt_ref[0, hs, :])
        p = jnp.exp(s - jnp.max(s, axis=-1, keepdims=True))
        p = p / jnp.sum(p, axis=-1, keepdims=True)
        heads.append(_dot(p.astype(BF16), mv_ref[0, :, hs]))
    xo = jnp.concatenate(heads, axis=1).astype(BF16)
    h_ref[...] = h1 + _dot(xo, wxo_ref[...])


def _mixer(x2, z, gates, o, cw, cb, lng, lnb, wco, wmo, wout, xg, wxq, mkt, mv, wxo, seq):
    t = x2.shape[0]
    tm = TOKEN_TILE
    tiles_per_seq = seq // tm
    halo_per_tile = tm // HALO
    mem_len = mv.shape[1]
    xw = X_HEADS * X_HEAD_DIM
    full = lambda arr: pl.BlockSpec(arr.shape, lambda i: (0,) * arr.ndim)
    row = lambda w: pl.BlockSpec((tm, w), lambda i: (i, 0))
    in_specs = [row(D_MODEL), row(CONV_CH),
                pl.BlockSpec((HALO, CONV_CH), lambda i: (jnp.maximum(i * halo_per_tile - 1, 0), 0)),
                row(2 * D_MODEL), row(MLA_HEADS * V_DIM),
                full(cw), full(cb), full(lng), full(lnb), full(wco), full(wmo), full(wout),
                full(xg), full(wxq),
                pl.BlockSpec((1, xw, mem_len), lambda i: (i // tiles_per_seq, 0, 0)),
                pl.BlockSpec((1, mem_len, xw), lambda i: (i // tiles_per_seq, 0, 0)),
                full(wxo)]
    return pl.pallas_call(
        functools.partial(_mixer_kernel, tiles_per_seq),
        grid=(t // tm,),
        in_specs=in_specs,
        out_specs=row(D_MODEL),
        out_shape=jax.ShapeDtypeStruct((t, D_MODEL), F32),
        scratch_shapes=[pltpu.VMEM((tm + HALO, CONV_CH), F32), pltpu.VMEM((tm, CONV_CH), F32)],
        compiler_params=pltpu.CompilerParams(dimension_semantics=("parallel",),
                                             vmem_limit_bytes=VMEM_LIMIT),
        name="mixer",
    )(x2, z, z, gates, o, cw, cb, lng, lnb, wco, wmo, wout, xg, wxq, mkt, mv, wxo)


def _mlp_kernel(h_ref, g_ref, w1_ref, w2_ref, fg_ref, y_ref):
    h = h_ref[...]
    u = _rms(h, g_ref[...]).astype(BF16)
    chunk = 1024
    acc = h
    for c in range(0, D_FF, chunk):
        a = jnp.maximum(_dot(u, w1_ref[:, c:c + chunk]), 0.0)
        acc = acc + _dot((a * a).astype(BF16), w2_ref[c:c + chunk, :])
    y_ref[...] = _rms(acc, fg_ref[...])


def _mlp(h, g, w1, w2, fg):
    t = h.shape[0]
    tm = TOKEN_TILE
    full = lambda arr: pl.BlockSpec(arr.shape, lambda i: (0,) * arr.ndim)
    row = pl.BlockSpec((tm, D_MODEL), lambda i: (i, 0))
    return pl.pallas_call(
        _mlp_kernel,
        grid=(t // tm,),
        in_specs=[row, full(g), full(w1), full(w2), full(fg)],
        out_specs=row,
        out_shape=jax.ShapeDtypeStruct((t, D_MODEL), F32),
        compiler_params=pltpu.CompilerParams(dimension_semantics=("parallel",),
                                             vmem_limit_bytes=VMEM_LIMIT),
        name="mlp",
    )(h, g, w1, w2, fg)


def _rope_placement():
    e = np.zeros((HEAD_PAD, MLA_HEADS * HEAD_PAD), np.float32)
    for h in range(MLA_HEADS):
        for j in range(QK_ROPE):
            e[j, h * HEAD_PAD + QK_NOPE + j] = 1.0
    return jnp.asarray(e, BF16)


def kernel(x, mem, positions, norm_mix_g, w_in, conv_w, conv_b, conv_ln_g, conv_ln_b, w_conv_out, q_norm_g, w_uq, kv_norm_g, w_ukv, w_mla_out, w_out, norm_xattn_g, norm_mem_g, w_xq, w_xkv, w_xo, norm_mlp_g, w_mlp1, w_mlp2, final_norm_g):
    batch, seq, _ = x.shape
    mem_len = mem.shape[1]
    t = batch * seq
    assert w_in.shape[0] == 1, "single-layer block (the final rmsnorm is fused into the MLP kernel)"
    assert seq % KV_CHUNK == 0 and seq % TOKEN_TILE == 0 and t % ROPE_TILE == 0

    inv_freq = ROPE_THETA ** (-jnp.arange(ROPE_HALF, dtype=F32) / ROPE_HALF)
    cos_t, sin_t, nsin_t = _rope_tables(positions.reshape(1, t), inv_freq.reshape(ROPE_HALF, 1))
    cs = jnp.concatenate([cos_t, cos_t, nsin_t, sin_t,
                          jnp.zeros((HEAD_PAD - 2 * QK_ROPE, t), F32)], axis=0).T
    place = _rope_placement()
    row = lambda v: v.reshape(1, -1)

    h = x.reshape(t, D_MODEL)
    wi = w_in[0].astype(BF16)
    c0 = 2 * CONV_CH
    c1 = c0 + Q_LORA
    c2 = c1 + KV_LORA
    c3 = c2 + QK_ROPE
    wkr = jnp.concatenate([wi[:, c2:c3], wi[:, c2 + ROPE_HALF:c3], wi[:, c2:c2 + ROPE_HALF],
                           jnp.zeros((D_MODEL, HEAD_PAD - 2 * QK_ROPE), BF16)], axis=1)
    wuq = w_uq[0].astype(BF16).reshape(Q_LORA, MLA_HEADS, QK_NOPE + QK_ROPE)
    wqt = jnp.pad(wuq, ((0, 0), (0, 0), (0, HEAD_PAD - QK_NOPE - QK_ROPE))).reshape(Q_LORA, -1).T
    wukv = w_ukv[0].astype(BF16).reshape(KV_LORA, MLA_HEADS, QK_NOPE + V_DIM)
    wk = jnp.pad(wukv[:, :, :QK_NOPE], ((0, 0), (0, 0), (0, HEAD_PAD - QK_NOPE))).reshape(KV_LORA, -1)
    wvt = jnp.pad(wukv[:, :, QK_NOPE:], ((0, 0), (0, 0), (0, V_ROWS - V_DIM))).reshape(KV_LORA, -1).T
    ones = np.zeros((MLA_HEADS, V_ROWS, 1), np.float32)
    ones[:, V_DIM] = 1.0
    ones = jnp.asarray(ones.reshape(MLA_HEADS * V_ROWS, 1))

    z, gates, qt, kp, vt3 = _in_proj(
        h, row(norm_mix_g[0]), wi[:, :CONV_CH], wi[:, CONV_CH:c0], wi[:, c3:], wi[:, c0:c1],
        wi[:, c1:c2], wkr, row(q_norm_g[0]), row(kv_norm_g[0]), wqt, wk, wvt, ones, place,
        cos_t, sin_t, cs)

    k3 = kp.reshape(t // KV_CHUNK, KV_CHUNK, MLA_HEADS * HEAD_PAD)
    o = _attention(qt, k3, vt3, batch, seq)

    xw = X_HEADS * X_HEAD_DIM
    wxkv = w_xkv[0].astype(BF16)
    mkt, mv = _mem_kv(mem.reshape(batch * mem_len, D_MODEL), row(norm_mem_g[0]),
                      wxkv[:, :xw].T, wxkv[:, xw:], batch, mem_len)

    cw = jnp.pad(conv_w[0], ((0, HALO - CONV_WIDTH), (0, 0)))
    h = _mixer(h, z, gates, o, cw, row(conv_b[0]), row(conv_ln_g[0]), row(conv_ln_b[0]),
               w_conv_out[0].astype(BF16), w_mla_out[0].astype(BF16), w_out[0].astype(BF16),
               row(norm_xattn_g[0]), w_xq[0].astype(BF16), mkt, mv, w_xo[0].astype(BF16), seq)
    h = _mlp(h, row(norm_mlp_g[0]), w_mlp1[0].astype(BF16), w_mlp2[0].astype(BF16),
             row(final_norm_g))
    return h.reshape(batch, seq, D_MODEL)
```

```python
import functools

import numpy as np
import jax
import jax.numpy as jnp
from jax import lax
from jax.experimental import pallas as pl
from jax.experimental.pallas import tpu as pltpu

D_MODEL = 1024
CONV_CH = D_MODEL // 2
CONV_WIDTH = 31
MLA_HEADS = 8
QK_NOPE = D_MODEL // 16
QK_ROPE = D_MODEL // 32
V_DIM = D_MODEL // 16
Q_LORA = 3 * D_MODEL // 8
KV_LORA = D_MODEL // 4
X_HEADS = 4
X_HEAD_DIM = D_MODEL // 8
D_FF = 4 * D_MODEL
ROPE_THETA = 10000.0
EPS = 1e-6

HEAD_PAD = 128
V_ROWS = V_DIM + 16
ATTN_HEADS = 4
ROPE_HALF = QK_ROPE // 2
SUBLANES = 8
HALO = 32
TOKEN_TILE = 256
Q_TILE = 256
KV_CHUNK = 2 * Q_TILE
ROPE_TILE = 2048
LOG2E = 1.4426950408889634


def _column_ranges(widths):
    out, start = {}, 0
    for name, width in widths:
        out[name] = (start, start + width)
        start += width
    return out


IN_COLS = _column_ranges([("glu_a", CONV_CH), ("glu_g", CONV_CH), ("gates", 2 * D_MODEL),
                          ("c_q", Q_LORA), ("c_kv", KV_LORA), ("k_rope", HEAD_PAD)])
MASK_VALUE = -1e30
VMEM_LIMIT = 56 * 1024 * 1024

F32 = jnp.float32
BF16 = jnp.bfloat16


def _rms(x, g):
    return x * lax.rsqrt(jnp.mean(x * x, axis=-1, keepdims=True) + EPS) * g


def _sigmoid(x):
    return 1.0 / (1.0 + jnp.exp(-x))


def _dot(a, b):
    return jnp.dot(a, b, preferred_element_type=F32)


def _dot_nt(a, b):
    return lax.dot_general(a, b, (((1,), (1,)), ((), ())), preferred_element_type=F32)


def _rope_kernel(pos_ref, invf_ref, cos_ref, sin_ref, cs_ref):
    ang = pos_ref[...].astype(F32) * invf_ref[...]
    c = jnp.cos(ang)
    s = jnp.sin(ang)
    cos_ref[...] = c
    sin_ref[...] = s
    pad = jnp.zeros((HEAD_PAD - 2 * QK_ROPE, ang.shape[1]), F32)
    cs_ref[...] = jnp.concatenate([c, c, -s, s, pad], axis=0).T


def _rope_tables(pos_row, inv_freq_col):
    t = pos_row.shape[1]
    out = jax.ShapeDtypeStruct((ROPE_HALF, t), F32)
    spec = pl.BlockSpec((ROPE_HALF, ROPE_TILE), lambda i: (0, i))
    return pl.pallas_call(
        _rope_kernel,
        grid=(t // ROPE_TILE,),
        in_specs=[pl.BlockSpec((1, ROPE_TILE), lambda i: (0, i)),
                  pl.BlockSpec((ROPE_HALF, 1), lambda i: (0, 0))],
        out_specs=[spec, spec, pl.BlockSpec((ROPE_TILE, HEAD_PAD), lambda i: (i, 0))],
        out_shape=[out, out, jax.ShapeDtypeStruct((t, HEAD_PAD), F32)],
        name="rope_tables",
    )(pos_row, inv_freq_col)


def _in_proj_kernel(x_ref, g_ref, w_ref, qg_ref, kvg_ref, wqt_ref, wk_ref, wvt_ref, ones_ref, place_ref,
                    cos_ref, sin_ref, cs_ref,
                    z_ref, gates_ref, qt_ref, k_ref, vt_ref):
    u = _rms(x_ref[...], g_ref[...]).astype(BF16)
    proj = lambda name: _dot(u, w_ref[:, IN_COLS[name][0]:IN_COLS[name][1]])
    z_ref[...] = (proj("glu_a") * _sigmoid(proj("glu_g"))).astype(BF16)
    gates_ref[...] = _sigmoid(proj("gates")).astype(BF16)

    cqn = _rms(proj("c_q"), qg_ref[...]).astype(BF16)
    qt = _dot_nt(wqt_ref[...], cqn)
    scale = (QK_NOPE + QK_ROPE) ** -0.5 * LOG2E
    c = cos_ref[...]
    s = sin_ref[...]
    for h in range(MLA_HEADS):
        b = h * HEAD_PAD
        r1 = b + QK_NOPE
        r2 = r1 + ROPE_HALF
        r3 = r2 + ROPE_HALF
        t1 = qt[r1:r2]
        t2 = qt[r2:r3]
        qt_ref[b:r1, :] = (qt[b:r1] * scale).astype(BF16)
        qt_ref[r1:r2, :] = ((t1 * c - t2 * s) * scale).astype(BF16)
        qt_ref[r2:r3, :] = ((t2 * c + t1 * s) * scale).astype(BF16)
        qt_ref[r3:b + HEAD_PAD, :] = (qt[r3:b + HEAD_PAD] * scale).astype(BF16)

    ckvn = _rms(proj("c_kv"), kvg_ref[...]).astype(BF16)
    t = proj("k_rope") * cs_ref[...]
    rot = t + pltpu.roll(t, HEAD_PAD - QK_ROPE, 1)
    k_ref[...] = (_dot(ckvn, wk_ref[...]) + _dot(rot.astype(BF16), place_ref[...])).astype(BF16)
    vt_ref[0] = (_dot_nt(wvt_ref[...], ckvn) + ones_ref[...]).astype(BF16)


def _in_proj(x2, g, w, qg, kvg, wqt, wk, wvt, ones, place, cos_t, sin_t, cs):
    t = x2.shape[0]
    tm = KV_CHUNK
    n = t // tm
    full = lambda arr: pl.BlockSpec(arr.shape, lambda i: (0,) * arr.ndim)
    row = lambda w: pl.BlockSpec((tm, w), lambda i: (i, 0))
    in_specs = [row(D_MODEL), full(g), full(w), full(qg), full(kvg), full(wqt), full(wk), full(wvt),
                full(ones), full(place),
                pl.BlockSpec((ROPE_HALF, tm), lambda i: (0, i)),
                pl.BlockSpec((ROPE_HALF, tm), lambda i: (0, i)),
                row(HEAD_PAD)]
    out_shape = [jax.ShapeDtypeStruct((t, CONV_CH), BF16),
                 jax.ShapeDtypeStruct((t, 2 * D_MODEL), BF16),
                 jax.ShapeDtypeStruct((MLA_HEADS * HEAD_PAD, t), BF16),
                 jax.ShapeDtypeStruct((t, MLA_HEADS * HEAD_PAD), BF16),
                 jax.ShapeDtypeStruct((n, MLA_HEADS * V_ROWS, tm), BF16)]
    out_specs = [row(CONV_CH), row(2 * D_MODEL),
                 pl.BlockSpec((MLA_HEADS * HEAD_PAD, tm), lambda i: (0, i)),
                 row(MLA_HEADS * HEAD_PAD),
                 pl.BlockSpec((1, MLA_HEADS * V_ROWS, tm), lambda i: (i, 0, 0))]
    return pl.pallas_call(
        _in_proj_kernel,
        grid=(n,),
        in_specs=in_specs,
        out_specs=out_specs,
        out_shape=out_shape,
        compiler_params=pltpu.CompilerParams(dimension_semantics=("parallel",),
                                             vmem_limit_bytes=VMEM_LIMIT),
        name="in_proj",
    )(x2, g, w, qg, kvg, wqt, wk, wvt, ones, place, cos_t, sin_t, cs)


def _attn_kernel(qt_ref, k_ref, vt_ref, o_ref, s_scr, p_scr, acc_scr):
    qi = pl.program_id(2)
    tk = k_ref.shape[1]
    tq = qt_ref.shape[1]
    heads = range(ATTN_HEADS)
    chunks_per_q = tk // tq
    diag = qi // chunks_per_q
    key_idx = lax.broadcasted_iota(jnp.int32, (tk, tq), 0)
    qry_idx = lax.broadcasted_iota(jnp.int32, (tk, tq), 1) + (qi % chunks_per_q) * tq
    causal = key_idx <= qry_idx
    qts = [qt_ref[h * HEAD_PAD:(h + 1) * HEAD_PAD, :] for h in heads]

    def scores_to(slot, kc, masked):
        ms = []
        for h in heads:
            s = _dot(k_ref[kc, :, h * HEAD_PAD:(h + 1) * HEAD_PAD], qts[h])
            if masked:
                s = jnp.where(causal, s, MASK_VALUE)
            s_scr[slot, h] = s
            ms.append(jnp.max(s, axis=0, keepdims=True))
        return tuple(ms)

    def softmax_to(src, dst, ms):
        for h in heads:
            p_scr[dst, h] = jnp.exp2(s_scr[src, h] - ms[h]).astype(BF16)

    def accumulate(m_run, slot, kc, ms):
        new = []
        for h in heads:
            pv = _dot(vt_ref[kc, h * V_ROWS:(h + 1) * V_ROWS, :], p_scr[slot, h])
            m_new = jnp.maximum(m_run[h], ms[h])
            a = jnp.exp2(m_run[h] - m_new)
            b = jnp.exp2(ms[h] - m_new)
            acc_scr[h] = a * acc_scr[h] + b * pv
            new.append(m_new)
        return tuple(new)

    def step(cur, j, carry):
        m_run, pend_kc, pend_m, next_m = carry
        m_run = accumulate(m_run, 1 - cur, pend_kc, pend_m)
        softmax_to(1 - cur, cur, next_m)
        ahead_m = scores_to(cur, jnp.minimum(j + 1, jnp.maximum(diag - 1, 0)), False)
        return m_run, j, next_m, ahead_m

    def body(j, carry):
        return lax.cond(j % 2 == 0, functools.partial(step, 0), functools.partial(step, 1), j, carry)

    acc_scr[...] = jnp.zeros(acc_scr.shape, F32)
    m_init = tuple(jnp.full((1, tq), MASK_VALUE, F32) for _ in heads)
    diag_m = scores_to(0, diag, True)
    softmax_to(0, 1, diag_m)
    carry = (m_init, diag, diag_m, scores_to(1, 0, False))
    m_run, pend_kc, pend_m, _ = lax.fori_loop(0, diag, body, carry)
    accumulate(m_run, (diag + 1) % 2, pend_kc, pend_m)
    outs = [acc_scr[h, :V_DIM, :] / acc_scr[h, V_DIM:V_DIM + 1, :] for h in heads]
    o_ref[...] = jnp.concatenate(outs, axis=0).T.astype(BF16)


def _attention(qt, k3, vt3, batch, seq):
    tq = Q_TILE
    nq = seq // tq
    nk = seq // KV_CHUNK
    t = batch * seq
    return pl.pallas_call(
        _attn_kernel,
        grid=(batch, MLA_HEADS // ATTN_HEADS, nq),
        in_specs=[pl.BlockSpec((ATTN_HEADS * HEAD_PAD, tq), lambda b, hg, qi: (hg, b * nq + qi)),
                  pl.BlockSpec((nk, KV_CHUNK, ATTN_HEADS * HEAD_PAD), lambda b, hg, qi: (b, 0, hg)),
                  pl.BlockSpec((nk, ATTN_HEADS * V_ROWS, KV_CHUNK), lambda b, hg, qi: (b, hg, 0))],
        out_specs=pl.BlockSpec((tq, ATTN_HEADS * V_DIM), lambda b, hg, qi: (b * nq + qi, hg)),
        out_shape=jax.ShapeDtypeStruct((t, MLA_HEADS * V_DIM), BF16),
        scratch_shapes=[pltpu.VMEM((2, ATTN_HEADS, KV_CHUNK, tq), F32),
                        pltpu.VMEM((2, ATTN_HEADS, KV_CHUNK, tq), BF16),
                        pltpu.VMEM((ATTN_HEADS, V_ROWS, tq), F32)],
        compiler_params=pltpu.CompilerParams(
            dimension_semantics=("parallel", "parallel", "arbitrary"),
            vmem_limit_bytes=VMEM_LIMIT),
        name="mla_attention",
    )(qt, k3, vt3)


def _mem_kernel(mem_ref, g_ref, wkt_ref, wv_ref, kt_ref, v_ref):
    mn = _rms(mem_ref[...], g_ref[...]).astype(BF16)
    kt_ref[0] = _dot_nt(wkt_ref[...], mn).astype(BF16)
    v_ref[0] = _dot(mn, wv_ref[...]).astype(BF16)


def _mem_kv(mem2, g, wkt, wv, batch, mem_len):
    xw = X_HEADS * X_HEAD_DIM
    full = lambda arr: pl.BlockSpec(arr.shape, lambda b: (0,) * arr.ndim)
    return pl.pallas_call(
        _mem_kernel,
        grid=(batch,),
        in_specs=[pl.BlockSpec((mem_len, D_MODEL), lambda b: (b, 0)), full(g), full(wkt), full(wv)],
        out_specs=[pl.BlockSpec((1, xw, mem_len), lambda b: (b, 0, 0)),
                   pl.BlockSpec((1, mem_len, xw), lambda b: (b, 0, 0))],
        out_shape=[jax.ShapeDtypeStruct((batch, xw, mem_len), BF16),
                   jax.ShapeDtypeStruct((batch, mem_len, xw), BF16)],
        compiler_params=pltpu.CompilerParams(dimension_semantics=("parallel",),
                                             vmem_limit_bytes=VMEM_LIMIT),
        name="mem_kv",
    )(mem2, g, wkt, wv)


def _mixer_kernel(tiles_per_seq, x_ref, z_ref, halo_ref, gates_ref, o_ref,
                  cw_ref, cb_ref, lng_ref, lnb_ref, wco_ref, wmo_ref, wout_ref,
                  xg_ref, wxq_ref, mkt_ref, mv_ref, wxo_ref,
                  h_ref, zext_ref, conv_ref):
    tm = x_ref.shape[0]
    first = (pl.program_id(0) % tiles_per_seq) == 0
    ext = tm + HALO
    zext_ref[0, 0:HALO, :] = jnp.where(first, 0.0, halo_ref[...].astype(F32))
    zext_ref[0, HALO:, :] = z_ref[...].astype(F32)
    for s in range(1, SUBLANES):
        zext_ref[s, 0:ext - SUBLANES, :] = zext_ref[0, s:s + ext - SUBLANES, :]

    rows = 64
    off = HALO - (CONV_WIDTH - 1)
    for c in range(CONV_CH // 128):
        cs = slice(c * 128, (c + 1) * 128)
        for r in range(0, tm, rows):
            acc = jnp.broadcast_to(cb_ref[:, cs], (rows, 128))
            for w in range(CONV_WIDTH):
                shift, base = (off + w) % SUBLANES, (off + w) // SUBLANES * SUBLANES
                acc = acc + zext_ref[shift, r + base:r + base + rows, cs] * cw_ref[w:w + 1, cs]
            conv_ref[r:r + rows, cs] = acc

    y = conv_ref[...]
    mu = jnp.mean(y, axis=-1, keepdims=True)
    yc = y - mu
    var = jnp.mean(yc * yc, axis=-1, keepdims=True)
    y = yc * lax.rsqrt(var + EPS) * lng_ref[...] + lnb_ref[...]
    y = y * _sigmoid(y)
    conv_out = _dot(y.astype(BF16), wco_ref[...])
    mla_out = _dot(o_ref[...], wmo_ref[...])
    gates = gates_ref[...].astype(F32)
    merged = gates[:, :D_MODEL] * conv_out + gates[:, D_MODEL:] * mla_out
    h1 = x_ref[...] + _dot(merged.astype(BF16), wout_ref[...])

    u = _rms(h1, xg_ref[...]).astype(BF16)
    q = (_dot(u, wxq_ref[...]) * (X_HEAD_DIM ** -0.5)).astype(BF16)
    heads = []
    for h in range(X_HEADS):
        hs = slice(h * X_HEAD_DIM, (h + 1) * X_HEAD_DIM)
        s = _dot(q[:, hs], mkt_ref[0, hs, :])
        p = jnp.exp(s - jnp.max(s, axis=-1, keepdims=True))
        p = p / jnp.sum(p, axis=-1, keepdims=True)
        heads.append(_dot(p.astype(BF16), mv_ref[0, :, hs]))
    xo = jnp.concatenate(heads, axis=1).astype(BF16)
    h_ref[...] = h1 + _dot(xo, wxo_ref[...])


def _mixer(x2, z, gates, o, cw, cb, lng, lnb, wco, wmo, wout, xg, wxq, mkt, mv, wxo, seq):
    t = x2.shape[0]
    tm = TOKEN_TILE
    tiles_per_seq = seq // tm
    halo_per_tile = tm // HALO
    mem_len = mv.shape[1]
    xw = X_HEADS * X_HEAD_DIM
    full = lambda arr: pl.BlockSpec(arr.shape, lambda i: (0,) * arr.ndim)
    row = lambda w: pl.BlockSpec((tm, w), lambda i: (i, 0))
    in_specs = [row(D_MODEL), row(CONV_CH),
                pl.BlockSpec((HALO, CONV_CH), lambda i: (jnp.maximum(i * halo_per_tile - 1, 0), 0)),
                row(2 * D_MODEL), row(MLA_HEADS * V_DIM),
                full(cw), full(cb), full(lng), full(lnb), full(wco), full(wmo), full(wout),
                full(xg), full(wxq),
                pl.BlockSpec((1, xw, mem_len), lambda i: (i // tiles_per_seq, 0, 0)),
                pl.BlockSpec((1, mem_len, xw), lambda i: (i // tiles_per_seq, 0, 0)),
                full(wxo)]
    return pl.pallas_call(
        functools.partial(_mixer_kernel, tiles_per_seq),
        grid=(t // tm,),
        in_specs=in_specs,
        out_specs=row(D_MODEL),
        out_shape=jax.ShapeDtypeStruct((t, D_MODEL), F32),
        scratch_shapes=[pltpu.VMEM((SUBLANES, tm + HALO, CONV_CH), F32), pltpu.VMEM((tm, CONV_CH), F32)],
        compiler_params=pltpu.CompilerParams(dimension_semantics=("parallel",),
                                             vmem_limit_bytes=VMEM_LIMIT),
        name="mixer",
    )(x2, z, z, gates, o, cw, cb, lng, lnb, wco, wmo, wout, xg, wxq, mkt, mv, wxo)


def _mlp_kernel(h_ref, g_ref, w1_ref, w2_ref, fg_ref, y_ref):
    h = h_ref[...]
    u = _rms(h, g_ref[...]).astype(BF16)
    chunk = 1024
    acc = h
    for c in range(0, D_FF, chunk):
        a = jnp.maximum(_dot(u, w1_ref[:, c:c + chunk]), 0.0)
        acc = acc + _dot((a * a).astype(BF16), w2_ref[c:c + chunk, :])
    y_ref[...] = _rms(acc, fg_ref[...])


def _mlp(h, g, w1, w2, fg):
    t = h.shape[0]
    tm = TOKEN_TILE
    full = lambda arr: pl.BlockSpec(arr.shape, lambda i: (0,) * arr.ndim)
    row = pl.BlockSpec((tm, D_MODEL), lambda i: (i, 0))
    return pl.pallas_call(
        _mlp_kernel,
        grid=(t // tm,),
        in_specs=[row, full(g), full(w1), full(w2), full(fg)],
        out_specs=row,
        out_shape=jax.ShapeDtypeStruct((t, D_MODEL), F32),
        compiler_params=pltpu.CompilerParams(dimension_semantics=("parallel",),
                                             vmem_limit_bytes=VMEM_LIMIT),
        name="mlp",
    )(h, g, w1, w2, fg)


def _rope_placement():
    e = np.zeros((HEAD_PAD, MLA_HEADS * HEAD_PAD), np.float32)
    for h in range(MLA_HEADS):
        for j in range(QK_ROPE):
            e[j, h * HEAD_PAD + QK_NOPE + j] = 1.0
    return jnp.asarray(e, BF16)


def kernel(x, mem, positions, norm_mix_g, w_in, conv_w, conv_b, conv_ln_g, conv_ln_b, w_conv_out, q_norm_g, w_uq, kv_norm_g, w_ukv, w_mla_out, w_out, norm_xattn_g, norm_mem_g, w_xq, w_xkv, w_xo, norm_mlp_g, w_mlp1, w_mlp2, final_norm_g):
    batch, seq, _ = x.shape
    mem_len = mem.shape[1]
    t = batch * seq
    assert w_in.shape[0] == 1, "single-layer block (the final rmsnorm is fused into the MLP kernel)"
    assert seq % KV_CHUNK == 0 and seq % TOKEN_TILE == 0 and t % ROPE_TILE == 0

    inv_freq = ROPE_THETA ** (-jnp.arange(ROPE_HALF, dtype=F32) / ROPE_HALF)
    cos_t, sin_t, cs = _rope_tables(positions.reshape(1, t), inv_freq.reshape(ROPE_HALF, 1))
    place = _rope_placement()
    row = lambda v: v.reshape(1, -1)

    h = x.reshape(t, D_MODEL)
    wi = w_in[0]
    c0 = 2 * CONV_CH
    c1 = c0 + Q_LORA
    c2 = c1 + KV_LORA
    c3 = c2 + QK_ROPE
    wi = jnp.concatenate([wi[:, :c0], wi[:, c3:], wi[:, c0:c2], wi[:, c2:c3], wi[:, c2 + ROPE_HALF:c3],
                          wi[:, c2:c2 + ROPE_HALF], jnp.zeros((D_MODEL, HEAD_PAD - 2 * QK_ROPE), F32)],
                         axis=1).astype(BF16)
    assert wi.shape[1] == IN_COLS["k_rope"][1]
    wuq = w_uq[0].astype(BF16).reshape(Q_LORA, MLA_HEADS, QK_NOPE + QK_ROPE)
    wqt = jnp.pad(wuq, ((0, 0), (0, 0), (0, HEAD_PAD - QK_NOPE - QK_ROPE))).reshape(Q_LORA, -1).T
    wukv = w_ukv[0].astype(BF16).reshape(KV_LORA, MLA_HEADS, QK_NOPE + V_DIM)
    wk = jnp.pad(wukv[:, :, :QK_NOPE], ((0, 0), (0, 0), (0, HEAD_PAD - QK_NOPE))).reshape(KV_LORA, -1)
    wvt = jnp.pad(wukv[:, :, QK_NOPE:], ((0, 0), (0, 0), (0, V_ROWS - V_DIM))).reshape(KV_LORA, -1).T
    ones = np.zeros((MLA_HEADS, V_ROWS, 1), np.float32)
    ones[:, V_DIM] = 1.0
    ones = jnp.asarray(ones.reshape(MLA_HEADS * V_ROWS, 1))

    z, gates, qt, kp, vt3 = _in_proj(
        h, row(norm_mix_g[0]), wi, row(q_norm_g[0]), row(kv_norm_g[0]), wqt, wk, wvt, ones, place,
        cos_t, sin_t, cs)

    k3 = kp.reshape(t // KV_CHUNK, KV_CHUNK, MLA_HEADS * HEAD_PAD)
    o = _attention(qt, k3, vt3, batch, seq)

    xw = X_HEADS * X_HEAD_DIM
    wxkv = w_xkv[0].astype(BF16)
    mkt, mv = _mem_kv(mem.reshape(batch * mem_len, D_MODEL), row(norm_mem_g[0]),
                      wxkv[:, :xw].T, wxkv[:, xw:], batch, mem_len)

    cw = jnp.pad(conv_w[0], ((0, HALO - CONV_WIDTH), (0, 0)))
    h = _mixer(h, z, gates, o, cw, row(conv_b[0]), row(conv_ln_g[0]), row(conv_ln_b[0]),
               w_conv_out[0].astype(BF16), w_mla_out[0].astype(BF16), w_out[0].astype(BF16),
               row(norm_xattn_g[0]), w_xq[0].astype(BF16), mkt, mv, w_xo[0].astype(BF16), seq)
    h = _mlp(h, row(norm_mlp_g[0]), w_mlp1[0].astype(BF16), w_mlp2[0].astype(BF16),
             row(final_norm_g))
    return h.reshape(batch, seq, D_MODEL)
```

```python
import functools

import numpy as np
import jax
import jax.numpy as jnp
from jax import lax
from jax.experimental import pallas as pl
from jax.experimental.pallas import tpu as pltpu

D_MODEL = 1024
CONV_CH = D_MODEL // 2
CONV_WIDTH = 31
MLA_HEADS = 8
QK_NOPE = D_MODEL // 16
QK_ROPE = D_MODEL // 32
V_DIM = D_MODEL // 16
Q_LORA = 3 * D_MODEL // 8
KV_LORA = D_MODEL // 4
X_HEADS = 4
X_HEAD_DIM = D_MODEL // 8
D_FF = 4 * D_MODEL
ROPE_THETA = 10000.0
EPS = 1e-6

HEAD_PAD = 128
V_ROWS = V_DIM + 16
ATTN_HEADS = 4
ROPE_HALF = QK_ROPE // 2
SUBLANES = 8
HALO = 32
TOKEN_TILE = 256
Q_TILE = 512
KV_CHUNK = 512
ROPE_TILE = 2048
LOG2E = 1.4426950408889634


def _column_ranges(widths):
    out, start = {}, 0
    for name, width in widths:
        out[name] = (start, start + width)
        start += width
    return out


IN_COLS = _column_ranges([("glu_a", CONV_CH), ("glu_g", CONV_CH), ("gates", 2 * D_MODEL),
                          ("c_q", Q_LORA), ("c_kv", KV_LORA), ("k_rope", HEAD_PAD)])
MASK_VALUE = -1e30
VMEM_LIMIT = 56 * 1024 * 1024

F32 = jnp.float32
BF16 = jnp.bfloat16


def _rms(x, g):
    return x * lax.rsqrt(jnp.mean(x * x, axis=-1, keepdims=True) + EPS) * g


def _sigmoid(x):
    return 1.0 / (1.0 + jnp.exp(-x))


def _dot(a, b):
    return jnp.dot(a, b, preferred_element_type=F32)


def _dot_nt(a, b):
    return lax.dot_general(a, b, (((1,), (1,)), ((), ())), preferred_element_type=F32)


def _rope_kernel(pos_ref, invf_ref, cos_ref, sin_ref, cs_ref):
    ang = pos_ref[...].astype(F32) * invf_ref[...]
    c = jnp.cos(ang)
    s = jnp.sin(ang)
    cos_ref[...] = c
    sin_ref[...] = s
    pad = jnp.zeros((HEAD_PAD - 2 * QK_ROPE, ang.shape[1]), F32)
    cs_ref[...] = jnp.concatenate([c, c, -s, s, pad], axis=0).T


def _rope_tables(pos_row, inv_freq_col):
    t = pos_row.shape[1]
    out = jax.ShapeDtypeStruct((ROPE_HALF, t), F32)
    spec = pl.BlockSpec((ROPE_HALF, ROPE_TILE), lambda i: (0, i))
    return pl.pallas_call(
        _rope_kernel,
        grid=(t // ROPE_TILE,),
        in_specs=[pl.BlockSpec((1, ROPE_TILE), lambda i: (0, i)),
                  pl.BlockSpec((ROPE_HALF, 1), lambda i: (0, 0))],
        out_specs=[spec, spec, pl.BlockSpec((ROPE_TILE, HEAD_PAD), lambda i: (i, 0))],
        out_shape=[out, out, jax.ShapeDtypeStruct((t, HEAD_PAD), F32)],
        name="rope_tables",
    )(pos_row, inv_freq_col)


def _in_proj_kernel(x_ref, g_ref, w_ref, qg_ref, kvg_ref, wqt_ref, wk_ref, wvt_ref, ones_ref, place_ref,
                    cos_ref, sin_ref, cs_ref,
                    z_ref, gates_ref, qt_ref, k_ref, vt_ref):
    u = _rms(x_ref[...], g_ref[...]).astype(BF16)
    proj = lambda name: _dot(u, w_ref[:, IN_COLS[name][0]:IN_COLS[name][1]])
    z_ref[...] = (proj("glu_a") * _sigmoid(proj("glu_g"))).astype(BF16)
    gates_ref[...] = _sigmoid(proj("gates")).astype(BF16)

    cqn = _rms(proj("c_q"), qg_ref[...]).astype(BF16)
    qt = _dot_nt(wqt_ref[...], cqn)
    scale = (QK_NOPE + QK_ROPE) ** -0.5 * LOG2E
    c = cos_ref[...]
    s = sin_ref[...]
    for h in range(MLA_HEADS):
        b = h * HEAD_PAD
        r1 = b + QK_NOPE
        r2 = r1 + ROPE_HALF
        r3 = r2 + ROPE_HALF
        t1 = qt[r1:r2]
        t2 = qt[r2:r3]
        qt_ref[b:r1, :] = (qt[b:r1] * scale).astype(BF16)
        qt_ref[r1:r2, :] = ((t1 * c - t2 * s) * scale).astype(BF16)
        qt_ref[r2:r3, :] = ((t2 * c + t1 * s) * scale).astype(BF16)
        qt_ref[r3:b + HEAD_PAD, :] = (qt[r3:b + HEAD_PAD] * scale).astype(BF16)

    ckvn = _rms(proj("c_kv"), kvg_ref[...]).astype(BF16)
    t = proj("k_rope") * cs_ref[...]
    rot = t + pltpu.roll(t, HEAD_PAD - QK_ROPE, 1)
    k_ref[...] = (_dot(ckvn, wk_ref[...]) + _dot(rot.astype(BF16), place_ref[...])).astype(BF16)
    vt_ref[0] = (_dot_nt(wvt_ref[...], ckvn) + ones_ref[...]).astype(BF16)


def _in_proj(x2, g, w, qg, kvg, wqt, wk, wvt, ones, place, cos_t, sin_t, cs):
    t = x2.shape[0]
    tm = KV_CHUNK
    n = t // tm
    full = lambda arr: pl.BlockSpec(arr.shape, lambda i: (0,) * arr.ndim)
    row = lambda w: pl.BlockSpec((tm, w), lambda i: (i, 0))
    in_specs = [row(D_MODEL), full(g), full(w), full(qg), full(kvg), full(wqt), full(wk), full(wvt),
                full(ones), full(place),
                pl.BlockSpec((ROPE_HALF, tm), lambda i: (0, i)),
                pl.BlockSpec((ROPE_HALF, tm), lambda i: (0, i)),
                row(HEAD_PAD)]
    out_shape = [jax.ShapeDtypeStruct((t, CONV_CH), BF16),
                 jax.ShapeDtypeStruct((t, 2 * D_MODEL), BF16),
                 jax.ShapeDtypeStruct((MLA_HEADS * HEAD_PAD, t), BF16),
                 jax.ShapeDtypeStruct((t, MLA_HEADS * HEAD_PAD), BF16),
                 jax.ShapeDtypeStruct((n, MLA_HEADS * V_ROWS, tm), BF16)]
    out_specs = [row(CONV_CH), row(2 * D_MODEL),
                 pl.BlockSpec((MLA_HEADS * HEAD_PAD, tm), lambda i: (0, i)),
                 row(MLA_HEADS * HEAD_PAD),
                 pl.BlockSpec((1, MLA_HEADS * V_ROWS, tm), lambda i: (i, 0, 0))]
    return pl.pallas_call(
        _in_proj_kernel,
        grid=(n,),
        in_specs=in_specs,
        out_specs=out_specs,
        out_shape=out_shape,
        compiler_params=pltpu.CompilerParams(dimension_semantics=("parallel",),
                                             vmem_limit_bytes=VMEM_LIMIT),
        name="in_proj",
    )(x2, g, w, qg, kvg, wqt, wk, wvt, ones, place, cos_t, sin_t, cs)


def _attn_kernel(qt_ref, k_ref, vt_ref, o_ref, s_scr, p_scr, acc_scr):
    qi = pl.program_id(2)
    tk = k_ref.shape[1]
    tq = qt_ref.shape[1]
    heads = range(ATTN_HEADS)
    chunks_per_q = tk // tq
    diag = qi // chunks_per_q
    key_idx = lax.broadcasted_iota(jnp.int32, (tk, tq), 0)
    qry_idx = lax.broadcasted_iota(jnp.int32, (tk, tq), 1) + (qi % chunks_per_q) * tq
    causal = key_idx <= qry_idx
    qts = [qt_ref[h * HEAD_PAD:(h + 1) * HEAD_PAD, :] for h in heads]

    def scores_to(slot, kc, masked, hs=heads):
        ms = []
        for h in hs:
            s = _dot(k_ref[kc, :, h * HEAD_PAD:(h + 1) * HEAD_PAD], qts[h])
            if masked:
                s = jnp.where(causal, s, MASK_VALUE)
            s_scr[slot, h] = s
            ms.append(jnp.max(s, axis=0, keepdims=True))
        return tuple(ms)

    def softmax_to(src, dst, ms, hs=heads):
        for h, m in zip(hs, ms):
            p_scr[dst, h] = jnp.exp2(s_scr[src, h] - m).astype(BF16)

    def accumulate(m_run, slot, kc, ms, hs=heads):
        new = []
        for h, m_old, m_chunk in zip(hs, m_run, ms):
            pv = _dot(vt_ref[kc, h * V_ROWS:(h + 1) * V_ROWS, :], p_scr[slot, h])
            m_new = jnp.maximum(m_old, m_chunk)
            a = jnp.exp2(m_old - m_new)
            b = jnp.exp2(m_chunk - m_new)
            acc_scr[h] = a * acc_scr[h] + b * pv
            new.append(m_new)
        return tuple(new)

    def step(cur, j, carry):
        m_run, pend_kc, pend_m, next_m = carry
        ahead_kc = jnp.minimum(j + 1, jnp.maximum(diag - 1, 0))
        new_run, ahead_m = [], []
        for h in heads:
            new_run += accumulate(m_run[h:h + 1], 1 - cur, pend_kc, pend_m[h:h + 1], (h,))
            softmax_to(1 - cur, cur, next_m[h:h + 1], (h,))
            ahead_m += scores_to(cur, ahead_kc, False, (h,))
        return tuple(new_run), j, next_m, tuple(ahead_m)

    def body(j, carry):
        return lax.cond(j % 2 == 0, functools.partial(step, 0), functools.partial(step, 1), j, carry)

    acc_scr[...] = jnp.zeros(acc_scr.shape, F32)
    m_init = tuple(jnp.full((1, tq), MASK_VALUE, F32) for _ in heads)
    diag_m = scores_to(0, diag, True)
    softmax_to(0, 1, diag_m)
    carry = (m_init, diag, diag_m, scores_to(1, 0, False))
    m_run, pend_kc, pend_m, _ = lax.fori_loop(0, diag, body, carry)
    accumulate(m_run, (diag + 1) % 2, pend_kc, pend_m)
    outs = [acc_scr[h, :V_DIM, :] / acc_scr[h, V_DIM:V_DIM + 1, :] for h in heads]
    o_ref[...] = jnp.concatenate(outs, axis=0).T.astype(BF16)


def _attention(qt, k3, vt3, batch, seq):
    tq = Q_TILE
    nq = seq // tq
    nk = seq // KV_CHUNK
    t = batch * seq
    return pl.pallas_call(
        _attn_kernel,
        grid=(batch, MLA_HEADS // ATTN_HEADS, nq),
        in_specs=[pl.BlockSpec((ATTN_HEADS * HEAD_PAD, tq), lambda b, hg, qi: (hg, b * nq + qi)),
                  pl.BlockSpec((nk, KV_CHUNK, ATTN_HEADS * HEAD_PAD), lambda b, hg, qi: (b, 0, hg)),
                  pl.BlockSpec((nk, ATTN_HEADS * V_ROWS, KV_CHUNK), lambda b, hg, qi: (b, hg, 0))],
        out_specs=pl.BlockSpec((tq, ATTN_HEADS * V_DIM), lambda b, hg, qi: (b * nq + qi, hg)),
        out_shape=jax.ShapeDtypeStruct((t, MLA_HEADS * V_DIM), BF16),
        scratch_shapes=[pltpu.VMEM((2, ATTN_HEADS, KV_CHUNK, tq), F32),
                        pltpu.VMEM((2, ATTN_HEADS, KV_CHUNK, tq), BF16),
                        pltpu.VMEM((ATTN_HEADS, V_ROWS, tq), F32)],
        compiler_params=pltpu.CompilerParams(
            dimension_semantics=("parallel", "parallel", "arbitrary"),
            vmem_limit_bytes=VMEM_LIMIT),
        name="mla_attention",
    )(qt, k3, vt3)


def _mem_kernel(mem_ref, g_ref, wkt_ref, wv_ref, kt_ref, v_ref):
    mn = _rms(mem_ref[...], g_ref[...]).astype(BF16)
    kt_ref[0] = _dot_nt(wkt_ref[...], mn).astype(BF16)
    v_ref[0] = _dot(mn, wv_ref[...]).astype(BF16)


def _mem_kv(mem2, g, wkt, wv, batch, mem_len):
    xw = X_HEADS * X_HEAD_DIM
    full = lambda arr: pl.BlockSpec(arr.shape, lambda b: (0,) * arr.ndim)
    return pl.pallas_call(
        _mem_kernel,
        grid=(batch,),
        in_specs=[pl.BlockSpec((mem_len, D_MODEL), lambda b: (b, 0)), full(g), full(wkt), full(wv)],
        out_specs=[pl.BlockSpec((1, xw, mem_len), lambda b: (b, 0, 0)),
                   pl.BlockSpec((1, mem_len, xw), lambda b: (b, 0, 0))],
        out_shape=[jax.ShapeDtypeStruct((batch, xw, mem_len), BF16),
                   jax.ShapeDtypeStruct((batch, mem_len, xw), BF16)],
        compiler_params=pltpu.CompilerParams(dimension_semantics=("parallel",),
                                             vmem_limit_bytes=VMEM_LIMIT),
        name="mem_kv",
    )(mem2, g, wkt, wv)


def _mixer_kernel(tiles_per_seq, x_ref, z_ref, halo_ref, gates_ref, o_ref,
                  cw_ref, cb_ref, lng_ref, lnb_ref, wco_ref, wmo_ref, wout_ref,
                  xg_ref, wxq_ref, mkt_ref, mv_ref, wxo_ref,
                  h_ref, zext_ref, conv_ref):
    tm = x_ref.shape[0]
    first = (pl.program_id(0) % tiles_per_seq) == 0
    ext = tm + HALO
    zext_ref[0, 0:HALO, :] = jnp.where(first, 0.0, halo_ref[...].astype(F32))
    zext_ref[0, HALO:, :] = z_ref[...].astype(F32)
    for s in range(1, SUBLANES):
        zext_ref[s, 0:ext - SUBLANES, :] = zext_ref[0, s:s + ext - SUBLANES, :]

    rows = 64
    off = HALO - (CONV_WIDTH - 1)
    for c in range(CONV_CH // 128):
        cs = slice(c * 128, (c + 1) * 128)
        for r in range(0, tm, rows):
            acc = jnp.broadcast_to(cb_ref[:, cs], (rows, 128))
            for w in range(CONV_WIDTH):
                shift, base = (off + w) % SUBLANES, (off + w) // SUBLANES * SUBLANES
                acc = acc + zext_ref[shift, r + base:r + base + rows, cs] * cw_ref[w:w + 1, cs]
            conv_ref[r:r + rows, cs] = acc

    y = conv_ref[...]
    mu = jnp.mean(y, axis=-1, keepdims=True)
    yc = y - mu
    var = jnp.mean(yc * yc, axis=-1, keepdims=True)
    y = yc * lax.rsqrt(var + EPS) * lng_ref[...] + lnb_ref[...]
    y = y * _sigmoid(y)
    conv_out = _dot(y.astype(BF16), wco_ref[...])
    mla_out = _dot(o_ref[...], wmo_ref[...])
    gates = gates_ref[...].astype(F32)
    merged = gates[:, :D_MODEL] * conv_out + gates[:, D_MODEL:] * mla_out
    h1 = x_ref[...] + _dot(merged.astype(BF16), wout_ref[...])

    u = _rms(h1, xg_ref[...]).astype(BF16)
    q = (_dot(u, wxq_ref[...]) * (X_HEAD_DIM ** -0.5)).astype(BF16)
    heads = []
    for h in range(X_HEADS):
        hs = slice(h * X_HEAD_DIM, (h + 1) * X_HEAD_DIM)
        s = _dot(q[:, hs], mkt_ref[0, hs, :])
        p = jnp.exp(s - jnp.max(s, axis=-1, keepdims=True))
        p = p / jnp.sum(p, axis=-1, keepdims=True)
        heads.append(_dot(p.astype(BF16), mv_ref[0, :, hs]))
    xo = jnp.concatenate(heads, axis=1).astype(BF16)
    h_ref[...] = h1 + _dot(xo, wxo_ref[...])


def _mixer(x2, z, gates, o, cw, cb, lng, lnb, wco, wmo, wout, xg, wxq, mkt, mv, wxo, seq):
    t = x2.shape[0]
    tm = TOKEN_TILE
    tiles_per_seq = seq // tm
    halo_per_tile = tm // HALO
    mem_len = mv.shape[1]
    xw = X_HEADS * X_HEAD_DIM
    full = lambda arr: pl.BlockSpec(arr.shape, lambda i: (0,) * arr.ndim)
    row = lambda w: pl.BlockSpec((tm, w), lambda i: (i, 0))
    in_specs = [row(D_MODEL), row(CONV_CH),
                pl.BlockSpec((HALO, CONV_CH), lambda i: (jnp.maximum(i * halo_per_tile - 1, 0), 0)),
                row(2 * D_MODEL), row(MLA_HEADS * V_DIM),
                full(cw), full(cb), full(lng), full(lnb), full(wco), full(wmo), full(wout),
                full(xg), full(wxq),
                pl.BlockSpec((1, xw, mem_len), lambda i: (i // tiles_per_seq, 0, 0)),
                pl.BlockSpec((1, mem_len, xw), lambda i: (i // tiles_per_seq, 0, 0)),
                full(wxo)]
    return pl.pallas_call(
        functools.partial(_mixer_kernel, tiles_per_seq),
        grid=(t // tm,),
        in_specs=in_specs,
        out_specs=row(D_MODEL),
        out_shape=jax.ShapeDtypeStruct((t, D_MODEL), F32),
        scratch_shapes=[pltpu.VMEM((SUBLANES, tm + HALO, CONV_CH), F32), pltpu.VMEM((tm, CONV_CH), F32)],
        compiler_params=pltpu.CompilerParams(dimension_semantics=("parallel",),
                                             vmem_limit_bytes=VMEM_LIMIT),
        name="mixer",
    )(x2, z, z, gates, o, cw, cb, lng, lnb, wco, wmo, wout, xg, wxq, mkt, mv, wxo)


def _mlp_kernel(h_ref, g_ref, w1_ref, w2_ref, fg_ref, y_ref):
    h = h_ref[...]
    u = _rms(h, g_ref[...]).astype(BF16)
    chunk = 1024
    acc = h
    for c in range(0, D_FF, chunk):
        a = jnp.maximum(_dot(u, w1_ref[:, c:c + chunk]), 0.0)
        acc = acc + _dot((a * a).astype(BF16), w2_ref[c:c + chunk, :])
    y_ref[...] = _rms(acc, fg_ref[...])


def _mlp(h, g, w1, w2, fg):
    t = h.shape[0]
    tm = TOKEN_TILE
    full = lambda arr: pl.BlockSpec(arr.shape, lambda i: (0,) * arr.ndim)
    row = pl.BlockSpec((tm, D_MODEL), lambda i: (i, 0))
    return pl.pallas_call(
        _mlp_kernel,
        grid=(t // tm,),
        in_specs=[row, full(g), full(w1), full(w2), full(fg)],
        out_specs=row,
        out_shape=jax.ShapeDtypeStruct((t, D_MODEL), F32),
        compiler_params=pltpu.CompilerParams(dimension_semantics=("parallel",),
                                             vmem_limit_bytes=VMEM_LIMIT),
        name="mlp",
    )(h, g, w1, w2, fg)


def _rope_placement():
    e = np.zeros((HEAD_PAD, MLA_HEADS * HEAD_PAD), np.float32)
    for h in range(MLA_HEADS):
        for j in range(QK_ROPE):
            e[j, h * HEAD_PAD + QK_NOPE + j] = 1.0
    return jnp.asarray(e, BF16)


def kernel(x, mem, positions, norm_mix_g, w_in, conv_w, conv_b, conv_ln_g, conv_ln_b, w_conv_out, q_norm_g, w_uq, kv_norm_g, w_ukv, w_mla_out, w_out, norm_xattn_g, norm_mem_g, w_xq, w_xkv, w_xo, norm_mlp_g, w_mlp1, w_mlp2, final_norm_g):
    batch, seq, _ = x.shape
    mem_len = mem.shape[1]
    t = batch * seq
    assert w_in.shape[0] == 1, "single-layer block (the final rmsnorm is fused into the MLP kernel)"
    assert seq % KV_CHUNK == 0 and seq % TOKEN_TILE == 0 and t % ROPE_TILE == 0

    inv_freq = ROPE_THETA ** (-jnp.arange(ROPE_HALF, dtype=F32) / ROPE_HALF)
    cos_t, sin_t, cs = _rope_tables(positions.reshape(1, t), inv_freq.reshape(ROPE_HALF, 1))
    place = _rope_placement()
    row = lambda v: v.reshape(1, -1)

    h = x.reshape(t, D_MODEL)
    wi = w_in[0]
    c0 = 2 * CONV_CH
    c1 = c0 + Q_LORA
    c2 = c1 + KV_LORA
    c3 = c2 + QK_ROPE
    wi = jnp.concatenate([wi[:, :c0], wi[:, c3:], wi[:, c0:c2], wi[:, c2:c3], wi[:, c2 + ROPE_HALF:c3],
                          wi[:, c2:c2 + ROPE_HALF], jnp.zeros((D_MODEL, HEAD_PAD - 2 * QK_ROPE), F32)],
                         axis=1).astype(BF16)
    assert wi.shape[1] == IN_COLS["k_rope"][1]
    wuq = w_uq[0].astype(BF16).reshape(Q_LORA, MLA_HEADS, QK_NOPE + QK_ROPE)
    wqt = jnp.pad(wuq, ((0, 0), (0, 0), (0, HEAD_PAD - QK_NOPE - QK_ROPE))).reshape(Q_LORA, -1).T
    wukv = w_ukv[0].astype(BF16).reshape(KV_LORA, MLA_HEADS, QK_NOPE + V_DIM)
    wk = jnp.pad(wukv[:, :, :QK_NOPE], ((0, 0), (0, 0), (0, HEAD_PAD - QK_NOPE))).reshape(KV_LORA, -1)
    wvt = jnp.pad(wukv[:, :, QK_NOPE:], ((0, 0), (0, 0), (0, V_ROWS - V_DIM))).reshape(KV_LORA, -1).T
    ones = np.zeros((MLA_HEADS, V_ROWS, 1), np.float32)
    ones[:, V_DIM] = 1.0
    ones = jnp.asarray(ones.reshape(MLA_HEADS * V_ROWS, 1))

    z, gates, qt, kp, vt3 = _in_proj(
        h, row(norm_mix_g[0]), wi, row(q_norm_g[0]), row(kv_norm_g[0]), wqt, wk, wvt, ones, place,
        cos_t, sin_t, cs)

    k3 = kp.reshape(t // KV_CHUNK, KV_CHUNK, MLA_HEADS * HEAD_PAD)
    o = _attention(qt, k3, vt3, batch, seq)

    xw = X_HEADS * X_HEAD_DIM
    wxkv = w_xkv[0].astype(BF16)
    mkt, mv = _mem_kv(mem.reshape(batch * mem_len, D_MODEL), row(norm_mem_g[0]),
                      wxkv[:, :xw].T, wxkv[:, xw:], batch, mem_len)

    cw = jnp.pad(conv_w[0], ((0, HALO - CONV_WIDTH), (0, 0)))
    h = _mixer(h, z, gates, o, cw, row(conv_b[0]), row(conv_ln_g[0]), row(conv_ln_b[0]),
               w_conv_out[0].astype(BF16), w_mla_out[0].astype(BF16), w_out[0].astype(BF16),
               row(norm_xattn_g[0]), w_xq[0].astype(BF16), mkt, mv, w_xo[0].astype(BF16), seq)
    h = _mlp(h, row(norm_mlp_g[0]), w_mlp1[0].astype(BF16), w_mlp2[0].astype(BF16),
             row(final_norm_g))
    return h.reshape(batch, seq, D_MODEL)
```

```python
import functools

import numpy as np
import jax
import jax.numpy as jnp
from jax import lax
from jax.experimental import pallas as pl
from jax.experimental.pallas import tpu as pltpu

D_MODEL = 1024
CONV_CH = D_MODEL // 2
CONV_WIDTH = 31
MLA_HEADS = 8
QK_NOPE = D_MODEL // 16
QK_ROPE = D_MODEL // 32
V_DIM = D_MODEL // 16
Q_LORA = 3 * D_MODEL // 8
KV_LORA = D_MODEL // 4
X_HEADS = 4
X_HEAD_DIM = D_MODEL // 8
D_FF = 4 * D_MODEL
ROPE_THETA = 10000.0
EPS = 1e-6

HEAD_PAD = 128
V_ROWS = V_DIM + 16
ATTN_HEADS = 4
ROPE_HALF = QK_ROPE // 2
SUBLANES = 8
HALO = 32
TOKEN_TILE = 256
Q_TILE = 512
KV_CHUNK = 512
ROPE_TILE = 2048
LOG2E = 1.4426950408889634


def _column_ranges(widths):
    out, start = {}, 0
    for name, width in widths:
        out[name] = (start, start + width)
        start += width
    return out


IN_COLS = _column_ranges([("glu_a", CONV_CH), ("glu_g", CONV_CH), ("gates", 2 * D_MODEL),
                          ("c_q", Q_LORA), ("c_kv", KV_LORA), ("k_rope", HEAD_PAD)])
MASK_VALUE = -1e30
VMEM_LIMIT = 56 * 1024 * 1024

F32 = jnp.float32
BF16 = jnp.bfloat16


def _rms(x, g):
    return x * lax.rsqrt(jnp.mean(x * x, axis=-1, keepdims=True) + EPS) * g


def _sigmoid(x):
    return 1.0 / (1.0 + jnp.exp(-x))


def _dot(a, b):
    return jnp.dot(a, b, preferred_element_type=F32)


def _dot_nt(a, b):
    return lax.dot_general(a, b, (((1,), (1,)), ((), ())), preferred_element_type=F32)


def _rope_kernel(pos_ref, invf_ref, cos_ref, sin_ref, cs_ref):
    ang = pos_ref[...].astype(F32) * invf_ref[...]
    c = jnp.cos(ang)
    s = jnp.sin(ang)
    cos_ref[...] = c
    sin_ref[...] = s
    pad = jnp.zeros((HEAD_PAD - 2 * QK_ROPE, ang.shape[1]), F32)
    cs_ref[...] = jnp.concatenate([c, c, -s, s, pad], axis=0).T


def _rope_tables(pos_row, inv_freq_col):
    t = pos_row.shape[1]
    out = jax.ShapeDtypeStruct((ROPE_HALF, t), F32)
    spec = pl.BlockSpec((ROPE_HALF, ROPE_TILE), lambda i: (0, i))
    return pl.pallas_call(
        _rope_kernel,
        grid=(t // ROPE_TILE,),
        in_specs=[pl.BlockSpec((1, ROPE_TILE), lambda i: (0, i)),
                  pl.BlockSpec((ROPE_HALF, 1), lambda i: (0, 0))],
        out_specs=[spec, spec, pl.BlockSpec((ROPE_TILE, HEAD_PAD), lambda i: (i, 0))],
        out_shape=[out, out, jax.ShapeDtypeStruct((t, HEAD_PAD), F32)],
        name="rope_tables",
    )(pos_row, inv_freq_col)


def _in_proj_kernel(x_ref, g_ref, w_ref, qg_ref, kvg_ref, wqt_ref, wk_ref, wvt_ref, ones_ref, place_ref,
                    cos_ref, sin_ref, cs_ref,
                    z_ref, gates_ref, qt_ref, k_ref, vt_ref):
    u = _rms(x_ref[...], g_ref[...]).astype(BF16)
    proj = lambda name: _dot(u, w_ref[:, IN_COLS[name][0]:IN_COLS[name][1]])
    z_ref[...] = (proj("glu_a") * _sigmoid(proj("glu_g"))).astype(BF16)
    gates_ref[...] = _sigmoid(proj("gates")).astype(BF16)

    cqn = _rms(proj("c_q"), qg_ref[...]).astype(BF16)
    qt = _dot_nt(wqt_ref[...], cqn)
    scale = (QK_NOPE + QK_ROPE) ** -0.5 * LOG2E
    c = cos_ref[...]
    s = sin_ref[...]
    for h in range(MLA_HEADS):
        b = h * HEAD_PAD
        r1 = b + QK_NOPE
        r2 = r1 + ROPE_HALF
        r3 = r2 + ROPE_HALF
        t1 = qt[r1:r2]
        t2 = qt[r2:r3]
        qt_ref[b:r1, :] = (qt[b:r1] * scale).astype(BF16)
        qt_ref[r1:r2, :] = ((t1 * c - t2 * s) * scale).astype(BF16)
        qt_ref[r2:r3, :] = ((t2 * c + t1 * s) * scale).astype(BF16)
        qt_ref[r3:b + HEAD_PAD, :] = (qt[r3:b + HEAD_PAD] * scale).astype(BF16)

    ckvn = _rms(proj("c_kv"), kvg_ref[...]).astype(BF16)
    t = proj("k_rope") * cs_ref[...]
    rot = t + pltpu.roll(t, HEAD_PAD - QK_ROPE, 1)
    k_ref[...] = (_dot(ckvn, wk_ref[...]) + _dot(rot.astype(BF16), place_ref[...])).astype(BF16)
    vt_ref[0] = (_dot_nt(wvt_ref[...], ckvn) + ones_ref[...]).astype(BF16)


def _in_proj(x2, g, w, qg, kvg, wqt, wk, wvt, ones, place, cos_t, sin_t, cs):
    t = x2.shape[0]
    tm = KV_CHUNK
    n = t // tm
    full = lambda arr: pl.BlockSpec(arr.shape, lambda i: (0,) * arr.ndim)
    row = lambda w: pl.BlockSpec((tm, w), lambda i: (i, 0))
    in_specs = [row(D_MODEL), full(g), full(w), full(qg), full(kvg), full(wqt), full(wk), full(wvt),
                full(ones), full(place),
                pl.BlockSpec((ROPE_HALF, tm), lambda i: (0, i)),
                pl.BlockSpec((ROPE_HALF, tm), lambda i: (0, i)),
                row(HEAD_PAD)]
    out_shape = [jax.ShapeDtypeStruct((t, CONV_CH), BF16),
                 jax.ShapeDtypeStruct((t, 2 * D_MODEL), BF16),
                 jax.ShapeDtypeStruct((MLA_HEADS * HEAD_PAD, t), BF16),
                 jax.ShapeDtypeStruct((t, MLA_HEADS * HEAD_PAD), BF16),
                 jax.ShapeDtypeStruct((n, MLA_HEADS * V_ROWS, tm), BF16)]
    out_specs = [row(CONV_CH), row(2 * D_MODEL),
                 pl.BlockSpec((MLA_HEADS * HEAD_PAD, tm), lambda i: (0, i)),
                 row(MLA_HEADS * HEAD_PAD),
                 pl.BlockSpec((1, MLA_HEADS * V_ROWS, tm), lambda i: (i, 0, 0))]
    return pl.pallas_call(
        _in_proj_kernel,
        grid=(n,),
        in_specs=in_specs,
        out_specs=out_specs,
        out_shape=out_shape,
        compiler_params=pltpu.CompilerParams(dimension_semantics=("parallel",),
                                             vmem_limit_bytes=VMEM_LIMIT),
        name="in_proj",
    )(x2, g, w, qg, kvg, wqt, wk, wvt, ones, place, cos_t, sin_t, cs)


def _attn_kernel(qt_ref, k_ref, vt_ref, o_ref, s_scr, p_scr, acc_scr):
    qi = pl.program_id(2)
    tk = k_ref.shape[1]
    tq = qt_ref.shape[1]
    heads = range(ATTN_HEADS)
    chunks_per_q = tk // tq
    diag = qi // chunks_per_q
    key_idx = lax.broadcasted_iota(jnp.int32, (tk, tq), 0)
    qry_idx = lax.broadcasted_iota(jnp.int32, (tk, tq), 1) + (qi % chunks_per_q) * tq
    causal = key_idx <= qry_idx
    qts = [qt_ref[h * HEAD_PAD:(h + 1) * HEAD_PAD, :] for h in heads]

    def scores_to(slot, kc, masked, hs=heads):
        ms = []
        for h in hs:
            s = _dot(k_ref[kc, :, h * HEAD_PAD:(h + 1) * HEAD_PAD], qts[h])
            if masked:
                s = jnp.where(causal, s, MASK_VALUE)
            s_scr[slot, h] = s
            ms.append(jnp.max(s, axis=0, keepdims=True))
        return tuple(ms)

    def softmax_to(src, dst, ms, hs=heads):
        for h, m in zip(hs, ms):
            p_scr[dst, h] = jnp.exp2(s_scr[src, h] - m).astype(BF16)

    def accumulate(m_run, slot, kc, ms, hs=heads):
        new = []
        for h, m_old, m_chunk in zip(hs, m_run, ms):
            pv = _dot(vt_ref[kc, h * V_ROWS:(h + 1) * V_ROWS, :], p_scr[slot, h])
            m_new = jnp.maximum(m_old, m_chunk)
            a = jnp.exp2(m_old - m_new)
            b = jnp.exp2(m_chunk - m_new)
            acc_scr[h] = a * acc_scr[h] + b * pv
            new.append(m_new)
        return tuple(new)

    def step(cur, j, carry):
        m_run, pend_kc, pend_m, next_m = carry
        ahead_kc = jnp.minimum(j + 1, jnp.maximum(diag - 1, 0))
        new_run, ahead_m = [], []
        for h in heads:
            new_run += accumulate(m_run[h:h + 1], 1 - cur, pend_kc, pend_m[h:h + 1], (h,))
            softmax_to(1 - cur, cur, next_m[h:h + 1], (h,))
            ahead_m += scores_to(cur, ahead_kc, False, (h,))
        return tuple(new_run), j, next_m, tuple(ahead_m)

    def body(j, carry):
        return lax.cond(j % 2 == 0, functools.partial(step, 0), functools.partial(step, 1), j, carry)

    acc_scr[...] = jnp.zeros(acc_scr.shape, F32)
    m_init = tuple(jnp.full((1, tq), MASK_VALUE, F32) for _ in heads)
    diag_m = scores_to(0, diag, True)
    softmax_to(0, 1, diag_m)
    carry = (m_init, diag, diag_m, scores_to(1, 0, False))
    m_run, pend_kc, pend_m, _ = lax.fori_loop(0, diag, body, carry)
    accumulate(m_run, (diag + 1) % 2, pend_kc, pend_m)
    outs = [acc_scr[h, :V_DIM, :] / acc_scr[h, V_DIM:V_DIM + 1, :] for h in heads]
    o_ref[...] = jnp.concatenate(outs, axis=0).T.astype(BF16)


def _attention(qt, k3, vt3, batch, seq):
    tq = Q_TILE
    nq = seq // tq
    nk = seq // KV_CHUNK
    t = batch * seq
    return pl.pallas_call(
        _attn_kernel,
        grid=(batch, MLA_HEADS // ATTN_HEADS, nq),
        in_specs=[pl.BlockSpec((ATTN_HEADS * HEAD_PAD, tq), lambda b, hg, qi: (hg, b * nq + qi)),
                  pl.BlockSpec((nk, KV_CHUNK, ATTN_HEADS * HEAD_PAD), lambda b, hg, qi: (b, 0, hg)),
                  pl.BlockSpec((nk, ATTN_HEADS * V_ROWS, KV_CHUNK), lambda b, hg, qi: (b, hg, 0))],
        out_specs=pl.BlockSpec((tq, ATTN_HEADS * V_DIM), lambda b, hg, qi: (b * nq + qi, hg)),
        out_shape=jax.ShapeDtypeStruct((t, MLA_HEADS * V_DIM), BF16),
        scratch_shapes=[pltpu.VMEM((2, ATTN_HEADS, KV_CHUNK, tq), F32),
                        pltpu.VMEM((2, ATTN_HEADS, KV_CHUNK, tq), BF16),
                        pltpu.VMEM((ATTN_HEADS, V_ROWS, tq), F32)],
        compiler_params=pltpu.CompilerParams(
            dimension_semantics=("parallel", "parallel", "arbitrary"),
            vmem_limit_bytes=VMEM_LIMIT),
        name="mla_attention",
    )(qt, k3, vt3)


def _mem_kernel(mem_ref, g_ref, wkt_ref, wv_ref, kt_ref, v_ref):
    mn = _rms(mem_ref[...], g_ref[...]).astype(BF16)
    kt_ref[0] = _dot_nt(wkt_ref[...], mn).astype(BF16)
    v_ref[0] = _dot(mn, wv_ref[...]).astype(BF16)


def _mem_kv(mem2, g, wkt, wv, batch, mem_len):
    xw = X_HEADS * X_HEAD_DIM
    full = lambda arr: pl.BlockSpec(arr.shape, lambda b: (0,) * arr.ndim)
    return pl.pallas_call(
        _mem_kernel,
        grid=(batch,),
        in_specs=[pl.BlockSpec((mem_len, D_MODEL), lambda b: (b, 0)), full(g), full(wkt), full(wv)],
        out_specs=[pl.BlockSpec((1, xw, mem_len), lambda b: (b, 0, 0)),
                   pl.BlockSpec((1, mem_len, xw), lambda b: (b, 0, 0))],
        out_shape=[jax.ShapeDtypeStruct((batch, xw, mem_len), BF16),
                   jax.ShapeDtypeStruct((batch, mem_len, xw), BF16)],
        compiler_params=pltpu.CompilerParams(dimension_semantics=("parallel",),
                                             vmem_limit_bytes=VMEM_LIMIT),
        name="mem_kv",
    )(mem2, g, wkt, wv)


def _mixer_mlp_kernel(tiles_per_seq, n_tiles, x_ref, z_ref, halo_ref, gates_ref, o_ref,
                      cw_ref, cb_ref, lng_ref, lnb_ref, wco_ref, wmo_ref, wout_ref,
                      xg_ref, wxq_ref, mkt_ref, mv_ref, wxo_ref,
                      mg_ref, w1_ref, w2_ref, fg_ref,
                      y_ref, zext_ref, conv_ref, h_scr):
    step = pl.program_id(0)
    tm = x_ref.shape[0]

    @pl.when(step == 0)
    def _():
        h_scr[...] = jnp.zeros(h_scr.shape, F32)

    tile = jnp.minimum(step, n_tiles - 1)
    first = (tile % tiles_per_seq) == 0
    ext = tm + HALO
    zext_ref[0, 0:HALO, :] = jnp.where(first, 0.0, halo_ref[...].astype(F32))
    zext_ref[0, HALO:, :] = z_ref[...].astype(F32)
    for s in range(1, SUBLANES):
        zext_ref[s, 0:ext - SUBLANES, :] = zext_ref[0, s:s + ext - SUBLANES, :]

    h = h_scr[...]
    u = _rms(h, mg_ref[...]).astype(BF16)
    mlp_acc = h
    rows = 64
    off = HALO - (CONV_WIDTH - 1)
    conv_units = [(r, c) for c in range(CONV_CH // 128) for r in range(0, tm, rows)]
    ff_chunk = 512
    n_chunks = D_FF // ff_chunk
    per_chunk = len(conv_units) // n_chunks
    for k in range(n_chunks):
        f0 = k * ff_chunk
        a = jnp.maximum(_dot(u, w1_ref[:, f0:f0 + ff_chunk]), 0.0)
        mlp_acc = mlp_acc + _dot((a * a).astype(BF16), w2_ref[f0:f0 + ff_chunk, :])

        for r, c in conv_units[k * per_chunk:(k + 1) * per_chunk]:
            cs = slice(c * 128, (c + 1) * 128)
            acc = jnp.broadcast_to(cb_ref[:, cs], (rows, 128))
            for w in range(CONV_WIDTH):
                shift, base = (off + w) % SUBLANES, (off + w) // SUBLANES * SUBLANES
                acc = acc + zext_ref[shift, r + base:r + base + rows, cs] * cw_ref[w:w + 1, cs]
            conv_ref[r:r + rows, cs] = acc
    y_ref[...] = _rms(mlp_acc, fg_ref[...])

    y = conv_ref[...]
    mu = jnp.mean(y, axis=-1, keepdims=True)
    yc = y - mu
    var = jnp.mean(yc * yc, axis=-1, keepdims=True)
    y = yc * lax.rsqrt(var + EPS) * lng_ref[...] + lnb_ref[...]
    y = y * _sigmoid(y)
    conv_out = _dot(y.astype(BF16), wco_ref[...])
    mla_out = _dot(o_ref[...], wmo_ref[...])
    gates = gates_ref[...].astype(F32)
    merged = gates[:, :D_MODEL] * conv_out + gates[:, D_MODEL:] * mla_out
    h1 = x_ref[...] + _dot(merged.astype(BF16), wout_ref[...])

    u = _rms(h1, xg_ref[...]).astype(BF16)
    q = (_dot(u, wxq_ref[...]) * (X_HEAD_DIM ** -0.5)).astype(BF16)
    heads = []
    for h in range(X_HEADS):
        hs = slice(h * X_HEAD_DIM, (h + 1) * X_HEAD_DIM)
        s = _dot(q[:, hs], mkt_ref[0, hs, :])
        p = jnp.exp(s - jnp.max(s, axis=-1, keepdims=True))
        p = p / jnp.sum(p, axis=-1, keepdims=True)
        heads.append(_dot(p.astype(BF16), mv_ref[0, :, hs]))
    xo = jnp.concatenate(heads, axis=1).astype(BF16)
    h_scr[...] = h1 + _dot(xo, wxo_ref[...])


def _mixer_mlp(x2, z, gates, o, cw, cb, lng, lnb, wco, wmo, wout, xg, wxq, mkt, mv, wxo,
               mg, w1, w2, fg, seq):
    t = x2.shape[0]
    tm = TOKEN_TILE
    n_tiles = t // tm
    tiles_per_seq = seq // tm
    halo_per_tile = tm // HALO
    mem_len = mv.shape[1]
    xw = X_HEADS * X_HEAD_DIM
    tile = lambda i: jnp.minimum(i, n_tiles - 1)
    const = lambda arr: pl.BlockSpec(arr.shape, lambda i: (0,) * arr.ndim, pipeline_mode=pl.Buffered(1))
    row = lambda w: pl.BlockSpec((tm, w), lambda i: (tile(i), 0))
    in_specs = [row(D_MODEL), row(CONV_CH),
                pl.BlockSpec((HALO, CONV_CH), lambda i: (jnp.maximum(tile(i) * halo_per_tile - 1, 0), 0)),
                row(2 * D_MODEL), row(MLA_HEADS * V_DIM),
                const(cw), const(cb), const(lng), const(lnb), const(wco), const(wmo), const(wout),
                const(xg), const(wxq),
                pl.BlockSpec((1, xw, mem_len), lambda i: (tile(i) // tiles_per_seq, 0, 0)),
                pl.BlockSpec((1, mem_len, xw), lambda i: (tile(i) // tiles_per_seq, 0, 0)),
                const(wxo), const(mg), const(w1), const(w2), const(fg)]
    return pl.pallas_call(
        functools.partial(_mixer_mlp_kernel, tiles_per_seq, n_tiles),
        grid=(n_tiles + 1,),
        in_specs=in_specs,
        out_specs=pl.BlockSpec((tm, D_MODEL), lambda i: (jnp.maximum(i - 1, 0), 0)),
        out_shape=jax.ShapeDtypeStruct((t, D_MODEL), F32),
        scratch_shapes=[pltpu.VMEM((SUBLANES, tm + HALO, CONV_CH), F32), pltpu.VMEM((tm, CONV_CH), F32),
                        pltpu.VMEM((tm, D_MODEL), F32)],
        compiler_params=pltpu.CompilerParams(dimension_semantics=("arbitrary",),
                                             vmem_limit_bytes=VMEM_LIMIT),
        name="mixer_mlp",
    )(x2, z, z, gates, o, cw, cb, lng, lnb, wco, wmo, wout, xg, wxq, mkt, mv, wxo, mg, w1, w2, fg)


def _rope_placement():
    e = np.zeros((HEAD_PAD, MLA_HEADS * HEAD_PAD), np.float32)
    for h in range(MLA_HEADS):
        for j in range(QK_ROPE):
            e[j, h * HEAD_PAD + QK_NOPE + j] = 1.0
    return jnp.asarray(e, BF16)


def kernel(x, mem, positions, norm_mix_g, w_in, conv_w, conv_b, conv_ln_g, conv_ln_b, w_conv_out, q_norm_g, w_uq, kv_norm_g, w_ukv, w_mla_out, w_out, norm_xattn_g, norm_mem_g, w_xq, w_xkv, w_xo, norm_mlp_g, w_mlp1, w_mlp2, final_norm_g):
    batch, seq, _ = x.shape
    mem_len = mem.shape[1]
    t = batch * seq
    assert w_in.shape[0] == 1, "single-layer block (the final rmsnorm is fused into the MLP kernel)"
    assert seq % KV_CHUNK == 0 and seq % TOKEN_TILE == 0 and t % ROPE_TILE == 0

    inv_freq = ROPE_THETA ** (-jnp.arange(ROPE_HALF, dtype=F32) / ROPE_HALF)
    cos_t, sin_t, cs = _rope_tables(positions.reshape(1, t), inv_freq.reshape(ROPE_HALF, 1))
    place = _rope_placement()
    row = lambda v: v.reshape(1, -1)

    h = x.reshape(t, D_MODEL)
    wi = w_in[0]
    c0 = 2 * CONV_CH
    c1 = c0 + Q_LORA
    c2 = c1 + KV_LORA
    c3 = c2 + QK_ROPE
    wi = jnp.concatenate([wi[:, :c0], wi[:, c3:], wi[:, c0:c2], wi[:, c2:c3], wi[:, c2 + ROPE_HALF:c3],
                          wi[:, c2:c2 + ROPE_HALF], jnp.zeros((D_MODEL, HEAD_PAD - 2 * QK_ROPE), F32)],
                         axis=1).astype(BF16)
    assert wi.shape[1] == IN_COLS["k_rope"][1]
    wuq = w_uq[0].astype(BF16).reshape(Q_LORA, MLA_HEADS, QK_NOPE + QK_ROPE)
    wqt = jnp.pad(wuq, ((0, 0), (0, 0), (0, HEAD_PAD - QK_NOPE - QK_ROPE))).reshape(Q_LORA, -1).T
    wukv = w_ukv[0].astype(BF16).reshape(KV_LORA, MLA_HEADS, QK_NOPE + V_DIM)
    wk = jnp.pad(wukv[:, :, :QK_NOPE], ((0, 0), (0, 0), (0, HEAD_PAD - QK_NOPE))).reshape(KV_LORA, -1)
    wvt = jnp.pad(wukv[:, :, QK_NOPE:], ((0, 0), (0, 0), (0, V_ROWS - V_DIM))).reshape(KV_LORA, -1).T
    ones = np.zeros((MLA_HEADS, V_ROWS, 1), np.float32)
    ones[:, V_DIM] = 1.0
    ones = jnp.asarray(ones.reshape(MLA_HEADS * V_ROWS, 1))

    z, gates, qt, kp, vt3 = _in_proj(
        h, row(norm_mix_g[0]), wi, row(q_norm_g[0]), row(kv_norm_g[0]), wqt, wk, wvt, ones, place,
        cos_t, sin_t, cs)

    k3 = kp.reshape(t // KV_CHUNK, KV_CHUNK, MLA_HEADS * HEAD_PAD)
    o = _attention(qt, k3, vt3, batch, seq)

    xw = X_HEADS * X_HEAD_DIM
    wxkv = w_xkv[0].astype(BF16)
    mkt, mv = _mem_kv(mem.reshape(batch * mem_len, D_MODEL), row(norm_mem_g[0]),
                      wxkv[:, :xw].T, wxkv[:, xw:], batch, mem_len)

    cw = jnp.pad(conv_w[0], ((0, HALO - CONV_WIDTH), (0, 0)))
    h = _mixer_mlp(h, z, gates, o, cw, row(conv_b[0]), row(conv_ln_g[0]), row(conv_ln_b[0]),
                   w_conv_out[0].astype(BF16), w_mla_out[0].astype(BF16), w_out[0].astype(BF16),
                   row(norm_xattn_g[0]), w_xq[0].astype(BF16), mkt, mv, w_xo[0].astype(BF16),
                   row(norm_mlp_g[0]), w_mlp1[0].astype(BF16), w_mlp2[0].astype(BF16),
                   row(final_norm_g), seq)
    return h.reshape(batch, seq, D_MODEL)
```

```python
import functools

import numpy as np
import jax
import jax.numpy as jnp
from jax import lax
from jax.experimental import pallas as pl
from jax.experimental.pallas import tpu as pltpu

D_MODEL = 1024
CONV_CH = D_MODEL // 2
CONV_WIDTH = 31
MLA_HEADS = 8
QK_NOPE = D_MODEL // 16
QK_ROPE = D_MODEL // 32
V_DIM = D_MODEL // 16
Q_LORA = 3 * D_MODEL // 8
KV_LORA = D_MODEL // 4
X_HEADS = 4
X_HEAD_DIM = D_MODEL // 8
D_FF = 4 * D_MODEL
ROPE_THETA = 10000.0
EPS = 1e-6

HEAD_PAD = 128
V_ROWS = V_DIM + 16
ATTN_HEADS = 4
ROPE_HALF = QK_ROPE // 2
SUBLANES = 8
HALO = 32
TOKEN_TILE = 512
Q_TILE = 512
KV_CHUNK = 512
ROPE_TILE = 2048
LOG2E = 1.4426950408889634


def _column_ranges(widths):
    out, start = {}, 0
    for name, width in widths:
        out[name] = (start, start + width)
        start += width
    return out


IN_COLS = _column_ranges([("glu_a", CONV_CH), ("glu_g", CONV_CH), ("gates", 2 * D_MODEL),
                          ("c_q", Q_LORA), ("c_kv", KV_LORA), ("k_rope", HEAD_PAD)])
MASK_VALUE = -1e30
VMEM_LIMIT = 56 * 1024 * 1024

F32 = jnp.float32
BF16 = jnp.bfloat16


def _rms(x, g):
    return x * lax.rsqrt(jnp.mean(x * x, axis=-1, keepdims=True) + EPS) * g


def _sigmoid(x):
    return 1.0 / (1.0 + jnp.exp(-x))


def _dot(a, b):
    return jnp.dot(a, b, preferred_element_type=F32)


def _dot_nt(a, b):
    return lax.dot_general(a, b, (((1,), (1,)), ((), ())), preferred_element_type=F32)


def _rope_kernel(pos_ref, invf_ref, cos_ref, sin_ref, cs_ref):
    ang = pos_ref[...].astype(F32) * invf_ref[...]
    c = jnp.cos(ang)
    s = jnp.sin(ang)
    cos_ref[...] = c
    sin_ref[...] = s
    pad = jnp.zeros((HEAD_PAD - 2 * QK_ROPE, ang.shape[1]), F32)
    cs_ref[...] = jnp.concatenate([c, c, -s, s, pad], axis=0).T


def _rope_tables(pos_row, inv_freq_col):
    t = pos_row.shape[1]
    out = jax.ShapeDtypeStruct((ROPE_HALF, t), F32)
    spec = pl.BlockSpec((ROPE_HALF, ROPE_TILE), lambda i: (0, i))
    return pl.pallas_call(
        _rope_kernel,
        grid=(t // ROPE_TILE,),
        in_specs=[pl.BlockSpec((1, ROPE_TILE), lambda i: (0, i)),
                  pl.BlockSpec((ROPE_HALF, 1), lambda i: (0, 0))],
        out_specs=[spec, spec, pl.BlockSpec((ROPE_TILE, HEAD_PAD), lambda i: (i, 0))],
        out_shape=[out, out, jax.ShapeDtypeStruct((t, HEAD_PAD), F32)],
        name="rope_tables",
    )(pos_row, inv_freq_col)


def _in_proj_kernel(x_ref, g_ref, w_ref, qg_ref, kvg_ref, wqt_ref, wk_ref, wvt_ref, ones_ref, place_ref,
                    cos_ref, sin_ref, cs_ref,
                    z_ref, gates_ref, qt_ref, k_ref, vt_ref):
    u = _rms(x_ref[...], g_ref[...]).astype(BF16)
    proj = lambda name: _dot(u, w_ref[:, IN_COLS[name][0]:IN_COLS[name][1]])
    z_ref[...] = (proj("glu_a") * _sigmoid(proj("glu_g"))).astype(BF16)
    gates_ref[...] = _sigmoid(proj("gates")).astype(BF16)

    cqn = _rms(proj("c_q"), qg_ref[...]).astype(BF16)
    qt = _dot_nt(wqt_ref[...], cqn)
    scale = (QK_NOPE + QK_ROPE) ** -0.5 * LOG2E
    c = cos_ref[...]
    s = sin_ref[...]
    for h in range(MLA_HEADS):
        b = h * HEAD_PAD
        r1 = b + QK_NOPE
        r2 = r1 + ROPE_HALF
        r3 = r2 + ROPE_HALF
        t1 = qt[r1:r2]
        t2 = qt[r2:r3]
        qt_ref[b:r1, :] = (qt[b:r1] * scale).astype(BF16)
        qt_ref[r1:r2, :] = ((t1 * c - t2 * s) * scale).astype(BF16)
        qt_ref[r2:r3, :] = ((t2 * c + t1 * s) * scale).astype(BF16)
        qt_ref[r3:b + HEAD_PAD, :] = (qt[r3:b + HEAD_PAD] * scale).astype(BF16)

    ckvn = _rms(proj("c_kv"), kvg_ref[...]).astype(BF16)
    t = proj("k_rope") * cs_ref[...]
    rot = t + pltpu.roll(t, HEAD_PAD - QK_ROPE, 1)
    k_ref[...] = (_dot(ckvn, wk_ref[...]) + _dot(rot.astype(BF16), place_ref[...])).astype(BF16)
    vt_ref[0] = (_dot_nt(wvt_ref[...], ckvn) + ones_ref[...]).astype(BF16)


def _in_proj(x2, g, w, qg, kvg, wqt, wk, wvt, ones, place, cos_t, sin_t, cs):
    t = x2.shape[0]
    tm = KV_CHUNK
    n = t // tm
    full = lambda arr: pl.BlockSpec(arr.shape, lambda i: (0,) * arr.ndim)
    row = lambda w: pl.BlockSpec((tm, w), lambda i: (i, 0))
    in_specs = [row(D_MODEL), full(g), full(w), full(qg), full(kvg), full(wqt), full(wk), full(wvt),
                full(ones), full(place),
                pl.BlockSpec((ROPE_HALF, tm), lambda i: (0, i)),
                pl.BlockSpec((ROPE_HALF, tm), lambda i: (0, i)),
                row(HEAD_PAD)]
    out_shape = [jax.ShapeDtypeStruct((t, CONV_CH), BF16),
                 jax.ShapeDtypeStruct((t, 2 * D_MODEL), BF16),
                 jax.ShapeDtypeStruct((MLA_HEADS * HEAD_PAD, t), BF16),
                 jax.ShapeDtypeStruct((t, MLA_HEADS * HEAD_PAD), BF16),
                 jax.ShapeDtypeStruct((n, MLA_HEADS * V_ROWS, tm), BF16)]
    out_specs = [row(CONV_CH), row(2 * D_MODEL),
                 pl.BlockSpec((MLA_HEADS * HEAD_PAD, tm), lambda i: (0, i)),
                 row(MLA_HEADS * HEAD_PAD),
                 pl.BlockSpec((1, MLA_HEADS * V_ROWS, tm), lambda i: (i, 0, 0))]
    return pl.pallas_call(
        _in_proj_kernel,
        grid=(n,),
        in_specs=in_specs,
        out_specs=out_specs,
        out_shape=out_shape,
        compiler_params=pltpu.CompilerParams(dimension_semantics=("parallel",),
                                             vmem_limit_bytes=VMEM_LIMIT),
        name="in_proj",
    )(x2, g, w, qg, kvg, wqt, wk, wvt, ones, place, cos_t, sin_t, cs)


def _attn_kernel(qt_ref, k_ref, vt_ref, o_ref, s_scr, p_scr, acc_scr):
    qi = pl.program_id(2)
    tk = k_ref.shape[1]
    tq = qt_ref.shape[1]
    heads = range(ATTN_HEADS)
    chunks_per_q = tk // tq
    diag = qi // chunks_per_q
    key_idx = lax.broadcasted_iota(jnp.int32, (tk, tq), 0)
    qry_idx = lax.broadcasted_iota(jnp.int32, (tk, tq), 1) + (qi % chunks_per_q) * tq
    causal = key_idx <= qry_idx
    qts = [qt_ref[h * HEAD_PAD:(h + 1) * HEAD_PAD, :] for h in heads]

    def scores_to(slot, kc, masked, hs=heads):
        ms = []
        for h in hs:
            s = _dot(k_ref[kc, :, h * HEAD_PAD:(h + 1) * HEAD_PAD], qts[h])
            if masked:
                s = jnp.where(causal, s, MASK_VALUE)
            s_scr[slot, h] = s
            ms.append(jnp.max(s, axis=0, keepdims=True))
        return tuple(ms)

    def softmax_to(src, dst, ms, hs=heads):
        for h, m in zip(hs, ms):
            p_scr[dst, h] = jnp.exp2(s_scr[src, h] - m).astype(BF16)

    def accumulate(m_run, slot, kc, ms, hs=heads):
        new = []
        for h, m_old, m_chunk in zip(hs, m_run, ms):
            pv = _dot(vt_ref[kc, h * V_ROWS:(h + 1) * V_ROWS, :], p_scr[slot, h])
            m_new = jnp.maximum(m_old, m_chunk)
            a = jnp.exp2(m_old - m_new)
            b = jnp.exp2(m_chunk - m_new)
            acc_scr[h] = a * acc_scr[h] + b * pv
            new.append(m_new)
        return tuple(new)

    def step(cur, j, carry):
        m_run, pend_kc, pend_m, next_m = carry
        ahead_kc = jnp.minimum(j + 1, jnp.maximum(diag - 1, 0))
        new_run, ahead_m = [], []
        for h in heads:
            new_run += accumulate(m_run[h:h + 1], 1 - cur, pend_kc, pend_m[h:h + 1], (h,))
            softmax_to(1 - cur, cur, next_m[h:h + 1], (h,))
            ahead_m += scores_to(cur, ahead_kc, False, (h,))
        return tuple(new_run), j, next_m, tuple(ahead_m)

    def body(j, carry):
        return lax.cond(j % 2 == 0, functools.partial(step, 0), functools.partial(step, 1), j, carry)

    acc_scr[...] = jnp.zeros(acc_scr.shape, F32)
    m_init = tuple(jnp.full((1, tq), MASK_VALUE, F32) for _ in heads)
    diag_m = scores_to(0, diag, True)
    softmax_to(0, 1, diag_m)
    carry = (m_init, diag, diag_m, scores_to(1, 0, False))
    m_run, pend_kc, pend_m, _ = lax.fori_loop(0, diag, body, carry)
    accumulate(m_run, (diag + 1) % 2, pend_kc, pend_m)
    outs = [acc_scr[h, :V_DIM, :] / acc_scr[h, V_DIM:V_DIM + 1, :] for h in heads]
    o_ref[...] = jnp.concatenate(outs, axis=0).T.astype(BF16)


def _attention(qt, k3, vt3, batch, seq):
    tq = Q_TILE
    nq = seq // tq
    nk = seq // KV_CHUNK
    t = batch * seq
    return pl.pallas_call(
        _attn_kernel,
        grid=(batch, MLA_HEADS // ATTN_HEADS, nq),
        in_specs=[pl.BlockSpec((ATTN_HEADS * HEAD_PAD, tq), lambda b, hg, qi: (hg, b * nq + qi)),
                  pl.BlockSpec((nk, KV_CHUNK, ATTN_HEADS * HEAD_PAD), lambda b, hg, qi: (b, 0, hg)),
                  pl.BlockSpec((nk, ATTN_HEADS * V_ROWS, KV_CHUNK), lambda b, hg, qi: (b, hg, 0))],
        out_specs=pl.BlockSpec((tq, ATTN_HEADS * V_DIM), lambda b, hg, qi: (b * nq + qi, hg)),
        out_shape=jax.ShapeDtypeStruct((t, MLA_HEADS * V_DIM), BF16),
        scratch_shapes=[pltpu.VMEM((2, ATTN_HEADS, KV_CHUNK, tq), F32),
                        pltpu.VMEM((2, ATTN_HEADS, KV_CHUNK, tq), BF16),
                        pltpu.VMEM((ATTN_HEADS, V_ROWS, tq), F32)],
        compiler_params=pltpu.CompilerParams(
            dimension_semantics=("parallel", "parallel", "arbitrary"),
            vmem_limit_bytes=VMEM_LIMIT),
        name="mla_attention",
    )(qt, k3, vt3)


def _mem_kernel(mem_ref, g_ref, wkt_ref, wv_ref, kt_ref, v_ref):
    mn = _rms(mem_ref[...], g_ref[...]).astype(BF16)
    kt_ref[0] = _dot_nt(wkt_ref[...], mn).astype(BF16)
    v_ref[0] = _dot(mn, wv_ref[...]).astype(BF16)


def _mem_kv(mem2, g, wkt, wv, batch, mem_len):
    xw = X_HEADS * X_HEAD_DIM
    full = lambda arr: pl.BlockSpec(arr.shape, lambda b: (0,) * arr.ndim)
    return pl.pallas_call(
        _mem_kernel,
        grid=(batch,),
        in_specs=[pl.BlockSpec((mem_len, D_MODEL), lambda b: (b, 0)), full(g), full(wkt), full(wv)],
        out_specs=[pl.BlockSpec((1, xw, mem_len), lambda b: (b, 0, 0)),
                   pl.BlockSpec((1, mem_len, xw), lambda b: (b, 0, 0))],
        out_shape=[jax.ShapeDtypeStruct((batch, xw, mem_len), BF16),
                   jax.ShapeDtypeStruct((batch, mem_len, xw), BF16)],
        compiler_params=pltpu.CompilerParams(dimension_semantics=("parallel",),
                                             vmem_limit_bytes=VMEM_LIMIT),
        name="mem_kv",
    )(mem2, g, wkt, wv)


def _mixer_kernel(tiles_per_seq, x_ref, z_ref, halo_ref, gates_ref, o_ref,
                  cw_ref, cb_ref, lng_ref, lnb_ref, wco_ref, wmo_ref, wout_ref,
                  xg_ref, wxq_ref, mkt_ref, mv_ref, wxo_ref,
                  h_ref, zext_ref, conv_ref):
    tm = x_ref.shape[0]
    first = (pl.program_id(0) % tiles_per_seq) == 0
    ext = tm + HALO
    zext_ref[0, 0:HALO, :] = jnp.where(first, 0.0, halo_ref[...].astype(F32))
    zext_ref[0, HALO:, :] = z_ref[...].astype(F32)
    for s in range(1, SUBLANES):
        zext_ref[s, 0:ext - SUBLANES, :] = zext_ref[0, s:s + ext - SUBLANES, :]

    rows = 64
    off = HALO - (CONV_WIDTH - 1)
    for c in range(CONV_CH // 128):
        cs = slice(c * 128, (c + 1) * 128)
        for r in range(0, tm, rows):
            acc = jnp.broadcast_to(cb_ref[:, cs], (rows, 128))
            for w in range(CONV_WIDTH):
                shift, base = (off + w) % SUBLANES, (off + w) // SUBLANES * SUBLANES
                acc = acc + zext_ref[shift, r + base:r + base + rows, cs] * cw_ref[w:w + 1, cs]
            conv_ref[r:r + rows, cs] = acc

    y = conv_ref[...]
    mu = jnp.mean(y, axis=-1, keepdims=True)
    yc = y - mu
    var = jnp.mean(yc * yc, axis=-1, keepdims=True)
    y = yc * lax.rsqrt(var + EPS) * lng_ref[...] + lnb_ref[...]
    y = y * _sigmoid(y)
    conv_out = _dot(y.astype(BF16), wco_ref[...])
    mla_out = _dot(o_ref[...], wmo_ref[...])
    gates = gates_ref[...].astype(F32)
    merged = gates[:, :D_MODEL] * conv_out + gates[:, D_MODEL:] * mla_out
    h1 = x_ref[...] + _dot(merged.astype(BF16), wout_ref[...])

    u = _rms(h1, xg_ref[...]).astype(BF16)
    q = (_dot(u, wxq_ref[...]) * (X_HEAD_DIM ** -0.5)).astype(BF16)
    heads = []
    for h in range(X_HEADS):
        hs = slice(h * X_HEAD_DIM, (h + 1) * X_HEAD_DIM)
        s = _dot(q[:, hs], mkt_ref[0, hs, :])
        p = jnp.exp(s - jnp.max(s, axis=-1, keepdims=True))
        p = p / jnp.sum(p, axis=-1, keepdims=True)
        heads.append(_dot(p.astype(BF16), mv_ref[0, :, hs]))
    xo = jnp.concatenate(heads, axis=1).astype(BF16)
    h_ref[...] = h1 + _dot(xo, wxo_ref[...])


def _mixer(x2, z, gates, o, cw, cb, lng, lnb, wco, wmo, wout, xg, wxq, mkt, mv, wxo, seq):
    t = x2.shape[0]
    tm = TOKEN_TILE
    tiles_per_seq = seq // tm
    halo_per_tile = tm // HALO
    mem_len = mv.shape[1]
    xw = X_HEADS * X_HEAD_DIM
    full = lambda arr: pl.BlockSpec(arr.shape, lambda i: (0,) * arr.ndim)
    row = lambda w: pl.BlockSpec((tm, w), lambda i: (i, 0))
    in_specs = [row(D_MODEL), row(CONV_CH),
                pl.BlockSpec((HALO, CONV_CH), lambda i: (jnp.maximum(i * halo_per_tile - 1, 0), 0)),
                row(2 * D_MODEL), row(MLA_HEADS * V_DIM),
                full(cw), full(cb), full(lng), full(lnb), full(wco), full(wmo), full(wout),
                full(xg), full(wxq),
                pl.BlockSpec((1, xw, mem_len), lambda i: (i // tiles_per_seq, 0, 0)),
                pl.BlockSpec((1, mem_len, xw), lambda i: (i // tiles_per_seq, 0, 0)),
                full(wxo)]
    return pl.pallas_call(
        functools.partial(_mixer_kernel, tiles_per_seq),
        grid=(t // tm,),
        in_specs=in_specs,
        out_specs=row(D_MODEL),
        out_shape=jax.ShapeDtypeStruct((t, D_MODEL), F32),
        scratch_shapes=[pltpu.VMEM((SUBLANES, tm + HALO, CONV_CH), F32), pltpu.VMEM((tm, CONV_CH), F32)],
        compiler_params=pltpu.CompilerParams(dimension_semantics=("parallel",),
                                             vmem_limit_bytes=VMEM_LIMIT),
        name="mixer",
    )(x2, z, z, gates, o, cw, cb, lng, lnb, wco, wmo, wout, xg, wxq, mkt, mv, wxo)


def _mlp_kernel(h_ref, g_ref, w1_ref, w2_ref, fg_ref, y_ref):
    h = h_ref[...]
    u = _rms(h, g_ref[...]).astype(BF16)
    chunk = 1024
    acc = h
    for c in range(0, D_FF, chunk):
        a = jnp.maximum(_dot(u, w1_ref[:, c:c + chunk]), 0.0)
        acc = acc + _dot((a * a).astype(BF16), w2_ref[c:c + chunk, :])
    y_ref[...] = _rms(acc, fg_ref[...])


def _mlp(h, g, w1, w2, fg):
    t = h.shape[0]
    tm = TOKEN_TILE
    full = lambda arr: pl.BlockSpec(arr.shape, lambda i: (0,) * arr.ndim)
    row = pl.BlockSpec((tm, D_MODEL), lambda i: (i, 0))
    return pl.pallas_call(
        _mlp_kernel,
        grid=(t // tm,),
        in_specs=[row, full(g), full(w1), full(w2), full(fg)],
        out_specs=row,
        out_shape=jax.ShapeDtypeStruct((t, D_MODEL), F32),
        compiler_params=pltpu.CompilerParams(dimension_semantics=("parallel",),
                                             vmem_limit_bytes=VMEM_LIMIT),
        name="mlp",
    )(h, g, w1, w2, fg)


def _rope_placement():
    e = np.zeros((HEAD_PAD, MLA_HEADS * HEAD_PAD), np.float32)
    for h in range(MLA_HEADS):
        for j in range(QK_ROPE):
            e[j, h * HEAD_PAD + QK_NOPE + j] = 1.0
    return jnp.asarray(e, BF16)


def kernel(x, mem, positions, norm_mix_g, w_in, conv_w, conv_b, conv_ln_g, conv_ln_b, w_conv_out, q_norm_g, w_uq, kv_norm_g, w_ukv, w_mla_out, w_out, norm_xattn_g, norm_mem_g, w_xq, w_xkv, w_xo, norm_mlp_g, w_mlp1, w_mlp2, final_norm_g):
    batch, seq, _ = x.shape
    mem_len = mem.shape[1]
    t = batch * seq
    assert w_in.shape[0] == 1, "single-layer block (the final rmsnorm is fused into the MLP kernel)"
    assert seq % KV_CHUNK == 0 and seq % TOKEN_TILE == 0 and t % ROPE_TILE == 0

    inv_freq = ROPE_THETA ** (-jnp.arange(ROPE_HALF, dtype=F32) / ROPE_HALF)
    cos_t, sin_t, cs = _rope_tables(positions.reshape(1, t), inv_freq.reshape(ROPE_HALF, 1))
    place = _rope_placement()
    row = lambda v: v.reshape(1, -1)

    h = x.reshape(t, D_MODEL)
    wi = w_in[0]
    c0 = 2 * CONV_CH
    c1 = c0 + Q_LORA
    c2 = c1 + KV_LORA
    c3 = c2 + QK_ROPE
    wi = jnp.concatenate([wi[:, :c0], wi[:, c3:], wi[:, c0:c2], wi[:, c2:c3], wi[:, c2 + ROPE_HALF:c3],
                          wi[:, c2:c2 + ROPE_HALF], jnp.zeros((D_MODEL, HEAD_PAD - 2 * QK_ROPE), F32)],
                         axis=1).astype(BF16)
    assert wi.shape[1] == IN_COLS["k_rope"][1]
    wuq = w_uq[0].astype(BF16).reshape(Q_LORA, MLA_HEADS, QK_NOPE + QK_ROPE)
    wqt = jnp.pad(wuq, ((0, 0), (0, 0), (0, HEAD_PAD - QK_NOPE - QK_ROPE))).reshape(Q_LORA, -1).T
    wukv = w_ukv[0].astype(BF16).reshape(KV_LORA, MLA_HEADS, QK_NOPE + V_DIM)
    wk = jnp.pad(wukv[:, :, :QK_NOPE], ((0, 0), (0, 0), (0, HEAD_PAD - QK_NOPE))).reshape(KV_LORA, -1)
    wvt = jnp.pad(wukv[:, :, QK_NOPE:], ((0, 0), (0, 0), (0, V_ROWS - V_DIM))).reshape(KV_LORA, -1).T
    ones = np.zeros((MLA_HEADS, V_ROWS, 1), np.float32)
    ones[:, V_DIM] = 1.0
    ones = jnp.asarray(ones.reshape(MLA_HEADS * V_ROWS, 1))

    z, gates, qt, kp, vt3 = _in_proj(
        h, row(norm_mix_g[0]), wi, row(q_norm_g[0]), row(kv_norm_g[0]), wqt, wk, wvt, ones, place,
        cos_t, sin_t, cs)

    k3 = kp.reshape(t // KV_CHUNK, KV_CHUNK, MLA_HEADS * HEAD_PAD)
    o = _attention(qt, k3, vt3, batch, seq)

    xw = X_HEADS * X_HEAD_DIM
    wxkv = w_xkv[0].astype(BF16)
    mkt, mv = _mem_kv(mem.reshape(batch * mem_len, D_MODEL), row(norm_mem_g[0]),
                      wxkv[:, :xw].T, wxkv[:, xw:], batch, mem_len)

    cw = jnp.pad(conv_w[0], ((0, HALO - CONV_WIDTH), (0, 0)))
    h = _mixer(h, z, gates, o, cw, row(conv_b[0]), row(conv_ln_g[0]), row(conv_ln_b[0]),
               w_conv_out[0].astype(BF16), w_mla_out[0].astype(BF16), w_out[0].astype(BF16),
               row(norm_xattn_g[0]), w_xq[0].astype(BF16), mkt, mv, w_xo[0].astype(BF16), seq)
    h = _mlp(h, row(norm_mlp_g[0]), w_mlp1[0].astype(BF16), w_mlp2[0].astype(BF16),
             row(final_norm_g))
    return h.reshape(batch, seq, D_MODEL)
```

```python
import functools

import numpy as np
import jax
import jax.numpy as jnp
from jax import lax
from jax.experimental import pallas as pl
from jax.experimental.pallas import tpu as pltpu

D_MODEL = 1024
CONV_CH = D_MODEL // 2
CONV_WIDTH = 31
MLA_HEADS = 8
QK_NOPE = D_MODEL // 16
QK_ROPE = D_MODEL // 32
V_DIM = D_MODEL // 16
Q_LORA = 3 * D_MODEL // 8
KV_LORA = D_MODEL // 4
X_HEADS = 4
X_HEAD_DIM = D_MODEL // 8
D_FF = 4 * D_MODEL
ROPE_THETA = 10000.0
EPS = 1e-6

HEAD_PAD = 128
V_ROWS = V_DIM + 16
ATTN_HEADS = 4
ROPE_HALF = QK_ROPE // 2
SUBLANES = 8
HALO = 32
TOKEN_TILE = 512
MLP_TILE = 1024
Q_TILE = 512
KV_CHUNK = 512
ROPE_TILE = 2048
LOG2E = 1.4426950408889634


def _column_ranges(widths):
    out, start = {}, 0
    for name, width in widths:
        out[name] = (start, start + width)
        start += width
    return out


IN_COLS = _column_ranges([("glu_a", CONV_CH), ("glu_g", CONV_CH), ("gates", 2 * D_MODEL),
                          ("c_q", Q_LORA), ("c_kv", KV_LORA), ("k_rope", HEAD_PAD)])
MASK_VALUE = -1e30
VMEM_LIMIT = 56 * 1024 * 1024

F32 = jnp.float32
BF16 = jnp.bfloat16


def _rms(x, g):
    return x * lax.rsqrt(jnp.mean(x * x, axis=-1, keepdims=True) + EPS) * g


def _sigmoid(x):
    return 1.0 / (1.0 + jnp.exp(-x))


def _dot(a, b):
    return jnp.dot(a, b, preferred_element_type=F32)


def _dot_nt(a, b):
    return lax.dot_general(a, b, (((1,), (1,)), ((), ())), preferred_element_type=F32)


def _rope_kernel(pos_ref, invf_ref, cos_ref, sin_ref, cs_ref):
    ang = pos_ref[...].astype(F32) * invf_ref[...]
    c = jnp.cos(ang)
    s = jnp.sin(ang)
    cos_ref[...] = c
    sin_ref[...] = s
    pad = jnp.zeros((HEAD_PAD - 2 * QK_ROPE, ang.shape[1]), F32)
    cs_ref[...] = jnp.concatenate([c, c, -s, s, pad], axis=0).T


def _rope_tables(pos_row, inv_freq_col):
    t = pos_row.shape[1]
    out = jax.ShapeDtypeStruct((ROPE_HALF, t), F32)
    spec = pl.BlockSpec((ROPE_HALF, ROPE_TILE), lambda i: (0, i))
    return pl.pallas_call(
        _rope_kernel,
        grid=(t // ROPE_TILE,),
        in_specs=[pl.BlockSpec((1, ROPE_TILE), lambda i: (0, i)),
                  pl.BlockSpec((ROPE_HALF, 1), lambda i: (0, 0))],
        out_specs=[spec, spec, pl.BlockSpec((ROPE_TILE, HEAD_PAD), lambda i: (i, 0))],
        out_shape=[out, out, jax.ShapeDtypeStruct((t, HEAD_PAD), F32)],
        name="rope_tables",
    )(pos_row, inv_freq_col)


def _regroup_kernel(w_ref, o_ref):
    c0 = 2 * CONV_CH
    c2 = c0 + Q_LORA + KV_LORA
    c3 = c2 + QK_ROPE
    rows = w_ref.shape[0]

    def put(name, value):
        o_ref[:, IN_COLS[name][0]:IN_COLS[name][1]] = value.astype(BF16)

    put("glu_a", w_ref[:, :CONV_CH])
    put("glu_g", w_ref[:, CONV_CH:c0])
    put("gates", w_ref[:, c3:])
    put("c_q", w_ref[:, c0:c0 + Q_LORA])
    put("c_kv", w_ref[:, c0 + Q_LORA:c2])
    kr = w_ref[:, c2:c3]
    put("k_rope", jnp.concatenate([kr, kr[:, ROPE_HALF:], kr[:, :ROPE_HALF],
                                   jnp.zeros((rows, HEAD_PAD - 2 * QK_ROPE), F32)], axis=1))


def _regroup_w_in(w):
    rows = 128
    width = IN_COLS["k_rope"][1]
    return pl.pallas_call(
        _regroup_kernel,
        grid=(w.shape[0] // rows,),
        in_specs=[pl.BlockSpec((rows, w.shape[1]), lambda i: (i, 0))],
        out_specs=pl.BlockSpec((rows, width), lambda i: (i, 0)),
        out_shape=jax.ShapeDtypeStruct((w.shape[0], width), BF16),
        name="regroup_w_in",
    )(w)


def _in_proj_kernel(x_ref, g_ref, w_ref, qg_ref, kvg_ref, wqt_ref, wk_ref, wvt_ref, ones_ref, place_ref,
                    cos_ref, sin_ref, cs_ref,
                    z_ref, gates_ref, qt_ref, k_ref, vt_ref):
    u = _rms(x_ref[...], g_ref[...]).astype(BF16)
    proj = lambda name: _dot(u, w_ref[:, IN_COLS[name][0]:IN_COLS[name][1]])
    z_ref[...] = (proj("glu_a") * _sigmoid(proj("glu_g"))).astype(BF16)
    gates_ref[...] = _sigmoid(proj("gates")).astype(BF16)

    cqn = _rms(proj("c_q"), qg_ref[...]).astype(BF16)
    qt = _dot_nt(wqt_ref[...], cqn)
    scale = (QK_NOPE + QK_ROPE) ** -0.5 * LOG2E
    c = cos_ref[...]
    s = sin_ref[...]
    for h in range(MLA_HEADS):
        b = h * HEAD_PAD
        r1 = b + QK_NOPE
        r2 = r1 + ROPE_HALF
        r3 = r2 + ROPE_HALF
        t1 = qt[r1:r2]
        t2 = qt[r2:r3]
        qt_ref[b:r1, :] = (qt[b:r1] * scale).astype(BF16)
        qt_ref[r1:r2, :] = ((t1 * c - t2 * s) * scale).astype(BF16)
        qt_ref[r2:r3, :] = ((t2 * c + t1 * s) * scale).astype(BF16)
        qt_ref[r3:b + HEAD_PAD, :] = (qt[r3:b + HEAD_PAD] * scale).astype(BF16)

    ckvn = _rms(proj("c_kv"), kvg_ref[...]).astype(BF16)
    t = proj("k_rope") * cs_ref[...]
    rot = t + pltpu.roll(t, HEAD_PAD - QK_ROPE, 1)
    k_ref[...] = (_dot(ckvn, wk_ref[...]) + _dot(rot.astype(BF16), place_ref[...])).astype(BF16)
    vt_ref[0] = (_dot_nt(wvt_ref[...], ckvn) + ones_ref[...]).astype(BF16)


def _in_proj(x2, g, w, qg, kvg, wqt, wk, wvt, ones, place, cos_t, sin_t, cs):
    t = x2.shape[0]
    tm = KV_CHUNK
    n = t // tm
    full = lambda arr: pl.BlockSpec(arr.shape, lambda i: (0,) * arr.ndim)
    row = lambda w: pl.BlockSpec((tm, w), lambda i: (i, 0))
    in_specs = [row(D_MODEL), full(g), full(w), full(qg), full(kvg), full(wqt), full(wk), full(wvt),
                full(ones), full(place),
                pl.BlockSpec((ROPE_HALF, tm), lambda i: (0, i)),
                pl.BlockSpec((ROPE_HALF, tm), lambda i: (0, i)),
                row(HEAD_PAD)]
    out_shape = [jax.ShapeDtypeStruct((t, CONV_CH), BF16),
                 jax.ShapeDtypeStruct((t, 2 * D_MODEL), BF16),
                 jax.ShapeDtypeStruct((MLA_HEADS * HEAD_PAD, t), BF16),
                 jax.ShapeDtypeStruct((t, MLA_HEADS * HEAD_PAD), BF16),
                 jax.ShapeDtypeStruct((n, MLA_HEADS * V_ROWS, tm), BF16)]
    out_specs = [row(CONV_CH), row(2 * D_MODEL),
                 pl.BlockSpec((MLA_HEADS * HEAD_PAD, tm), lambda i: (0, i)),
                 row(MLA_HEADS * HEAD_PAD),
                 pl.BlockSpec((1, MLA_HEADS * V_ROWS, tm), lambda i: (i, 0, 0))]
    return pl.pallas_call(
        _in_proj_kernel,
        grid=(n,),
        in_specs=in_specs,
        out_specs=out_specs,
        out_shape=out_shape,
        compiler_params=pltpu.CompilerParams(dimension_semantics=("parallel",),
                                             vmem_limit_bytes=VMEM_LIMIT),
        name="in_proj",
    )(x2, g, w, qg, kvg, wqt, wk, wvt, ones, place, cos_t, sin_t, cs)


def _attn_kernel(qt_ref, k_ref, vt_ref, o_ref, s_scr, p_scr, acc_scr):
    qi = pl.program_id(2)
    tk = k_ref.shape[1]
    tq = qt_ref.shape[1]
    heads = range(ATTN_HEADS)
    chunks_per_q = tk // tq
    diag = qi // chunks_per_q
    key_idx = lax.broadcasted_iota(jnp.int32, (tk, tq), 0)
    qry_idx = lax.broadcasted_iota(jnp.int32, (tk, tq), 1) + (qi % chunks_per_q) * tq
    causal = key_idx <= qry_idx
    qts = [qt_ref[h * HEAD_PAD:(h + 1) * HEAD_PAD, :] for h in heads]

    def scores_to(slot, kc, masked, hs=heads):
        ms = []
        for h in hs:
            s = _dot(k_ref[kc, :, h * HEAD_PAD:(h + 1) * HEAD_PAD], qts[h])
            if masked:
                s = jnp.where(causal, s, MASK_VALUE)
            s_scr[slot, h] = s
            ms.append(jnp.max(s, axis=0, keepdims=True))
        return tuple(ms)

    def softmax_to(src, dst, ms, hs=heads):
        for h, m in zip(hs, ms):
            p_scr[dst, h] = jnp.exp2(s_scr[src, h] - m).astype(BF16)

    def accumulate(m_run, slot, kc, ms, hs=heads):
        new = []
        for h, m_old, m_chunk in zip(hs, m_run, ms):
            pv = _dot(vt_ref[kc, h * V_ROWS:(h + 1) * V_ROWS, :], p_scr[slot, h])
            m_new = jnp.maximum(m_old, m_chunk)
            a = jnp.exp2(m_old - m_new)
            b = jnp.exp2(m_chunk - m_new)
            acc_scr[h] = a * acc_scr[h] + b * pv
            new.append(m_new)
        return tuple(new)

    def step(cur, j, carry):
        m_run, pend_kc, pend_m, next_m = carry
        ahead_kc = jnp.minimum(j + 1, jnp.maximum(diag - 1, 0))
        new_run, ahead_m = [], []
        for h in heads:
            new_run += accumulate(m_run[h:h + 1], 1 - cur, pend_kc, pend_m[h:h + 1], (h,))
            softmax_to(1 - cur, cur, next_m[h:h + 1], (h,))
            ahead_m += scores_to(cur, ahead_kc, False, (h,))
        return tuple(new_run), j, next_m, tuple(ahead_m)

    def body(j, carry):
        return lax.cond(j % 2 == 0, functools.partial(step, 0), functools.partial(step, 1), j, carry)

    acc_scr[...] = jnp.zeros(acc_scr.shape, F32)
    m_init = tuple(jnp.full((1, tq), MASK_VALUE, F32) for _ in heads)
    diag_m, first_m = [], []
    for h in heads:
        diag_m += scores_to(0, diag, True, (h,))
        first_m += scores_to(1, 0, False, (h,))
        softmax_to(0, 1, diag_m[h:h + 1], (h,))
    carry = (m_init, diag, tuple(diag_m), tuple(first_m))
    m_run, pend_kc, pend_m, _ = lax.fori_loop(0, diag, body, carry)
    accumulate(m_run, (diag + 1) % 2, pend_kc, pend_m)
    outs = [acc_scr[h, :V_DIM, :] / acc_scr[h, V_DIM:V_DIM + 1, :] for h in heads]
    o_ref[...] = jnp.concatenate(outs, axis=0).T.astype(BF16)


def _attention(qt, k3, vt3, batch, seq):
    tq = Q_TILE
    nq = seq // tq
    nk = seq // KV_CHUNK
    t = batch * seq
    return pl.pallas_call(
        _attn_kernel,
        grid=(batch, MLA_HEADS // ATTN_HEADS, nq),
        in_specs=[pl.BlockSpec((ATTN_HEADS * HEAD_PAD, tq), lambda b, hg, qi: (hg, b * nq + qi)),
                  pl.BlockSpec((nk, KV_CHUNK, ATTN_HEADS * HEAD_PAD), lambda b, hg, qi: (b, 0, hg)),
                  pl.BlockSpec((nk, ATTN_HEADS * V_ROWS, KV_CHUNK), lambda b, hg, qi: (b, hg, 0))],
        out_specs=pl.BlockSpec((tq, ATTN_HEADS * V_DIM), lambda b, hg, qi: (b * nq + qi, hg)),
        out_shape=jax.ShapeDtypeStruct((t, MLA_HEADS * V_DIM), BF16),
        scratch_shapes=[pltpu.VMEM((2, ATTN_HEADS, KV_CHUNK, tq), F32),
                        pltpu.VMEM((2, ATTN_HEADS, KV_CHUNK, tq), BF16),
                        pltpu.VMEM((ATTN_HEADS, V_ROWS, tq), F32)],
        compiler_params=pltpu.CompilerParams(
            dimension_semantics=("parallel", "parallel", "arbitrary"),
            vmem_limit_bytes=VMEM_LIMIT),
        name="mla_attention",
    )(qt, k3, vt3)


def _mem_kernel(mem_ref, g_ref, wkt_ref, wv_ref, kt_ref, v_ref):
    mn = _rms(mem_ref[...], g_ref[...]).astype(BF16)
    kt_ref[0] = _dot_nt(wkt_ref[...], mn).astype(BF16)
    v_ref[0] = _dot(mn, wv_ref[...]).astype(BF16)


def _mem_kv(mem2, g, wkt, wv, batch, mem_len):
    xw = X_HEADS * X_HEAD_DIM
    full = lambda arr: pl.BlockSpec(arr.shape, lambda b: (0,) * arr.ndim)
    return pl.pallas_call(
        _mem_kernel,
        grid=(batch,),
        in_specs=[pl.BlockSpec((mem_len, D_MODEL), lambda b: (b, 0)), full(g), full(wkt), full(wv)],
        out_specs=[pl.BlockSpec((1, xw, mem_len), lambda b: (b, 0, 0)),
                   pl.BlockSpec((1, mem_len, xw), lambda b: (b, 0, 0))],
        out_shape=[jax.ShapeDtypeStruct((batch, xw, mem_len), BF16),
                   jax.ShapeDtypeStruct((batch, mem_len, xw), BF16)],
        compiler_params=pltpu.CompilerParams(dimension_semantics=("parallel",),
                                             vmem_limit_bytes=VMEM_LIMIT),
        name="mem_kv",
    )(mem2, g, wkt, wv)


def _mixer_kernel(tiles_per_seq, x_ref, z_ref, halo_ref, gates_ref, o_ref,
                  cw_ref, cb_ref, lng_ref, lnb_ref, wco_ref, wmo_ref, wout_ref,
                  xg_ref, wxq_ref, mkt_ref, mv_ref, wxo_ref,
                  h_ref, zext_ref, conv_ref):
    tm = x_ref.shape[0]
    first = (pl.program_id(0) % tiles_per_seq) == 0
    ext = tm + HALO
    zext_ref[0, 0:HALO, :] = jnp.where(first, 0.0, halo_ref[...].astype(F32))
    zext_ref[0, HALO:, :] = z_ref[...].astype(F32)
    for s in range(1, SUBLANES):
        zext_ref[s, 0:ext - SUBLANES, :] = zext_ref[0, s:s + ext - SUBLANES, :]

    rows = 64
    off = HALO - (CONV_WIDTH - 1)
    for c in range(CONV_CH // 128):
        cs = slice(c * 128, (c + 1) * 128)
        for r in range(0, tm, rows):
            acc = jnp.broadcast_to(cb_ref[:, cs], (rows, 128))
            for w in range(CONV_WIDTH):
                shift, base = (off + w) % SUBLANES, (off + w) // SUBLANES * SUBLANES
                acc = acc + zext_ref[shift, r + base:r + base + rows, cs] * cw_ref[w:w + 1, cs]
            conv_ref[r:r + rows, cs] = acc

    y = conv_ref[...]
    mu = jnp.mean(y, axis=-1, keepdims=True)
    yc = y - mu
    var = jnp.mean(yc * yc, axis=-1, keepdims=True)
    y = yc * lax.rsqrt(var + EPS) * lng_ref[...] + lnb_ref[...]
    y = y * _sigmoid(y)
    conv_out = _dot(y.astype(BF16), wco_ref[...])
    mla_out = _dot(o_ref[...], wmo_ref[...])
    gates = gates_ref[...].astype(F32)
    merged = gates[:, :D_MODEL] * conv_out + gates[:, D_MODEL:] * mla_out
    h1 = x_ref[...] + _dot(merged.astype(BF16), wout_ref[...])

    u = _rms(h1, xg_ref[...]).astype(BF16)
    q = (_dot(u, wxq_ref[...]) * (X_HEAD_DIM ** -0.5)).astype(BF16)
    heads = []
    for h in range(X_HEADS):
        hs = slice(h * X_HEAD_DIM, (h + 1) * X_HEAD_DIM)
        s = _dot(q[:, hs], mkt_ref[0, hs, :])
        p = jnp.exp(s - jnp.max(s, axis=-1, keepdims=True))
        p = p / jnp.sum(p, axis=-1, keepdims=True)
        heads.append(_dot(p.astype(BF16), mv_ref[0, :, hs]))
    xo = jnp.concatenate(heads, axis=1).astype(BF16)
    h_ref[...] = h1 + _dot(xo, wxo_ref[...])


def _mixer(x2, z, gates, o, cw, cb, lng, lnb, wco, wmo, wout, xg, wxq, mkt, mv, wxo, seq):
    t = x2.shape[0]
    tm = TOKEN_TILE
    tiles_per_seq = seq // tm
    halo_per_tile = tm // HALO
    mem_len = mv.shape[1]
    xw = X_HEADS * X_HEAD_DIM
    full = lambda arr: pl.BlockSpec(arr.shape, lambda i: (0,) * arr.ndim)
    row = lambda w: pl.BlockSpec((tm, w), lambda i: (i, 0))
    in_specs = [row(D_MODEL), row(CONV_CH),
                pl.BlockSpec((HALO, CONV_CH), lambda i: (jnp.maximum(i * halo_per_tile - 1, 0), 0)),
                row(2 * D_MODEL), row(MLA_HEADS * V_DIM),
                full(cw), full(cb), full(lng), full(lnb), full(wco), full(wmo), full(wout),
                full(xg), full(wxq),
                pl.BlockSpec((1, xw, mem_len), lambda i: (i // tiles_per_seq, 0, 0)),
                pl.BlockSpec((1, mem_len, xw), lambda i: (i // tiles_per_seq, 0, 0)),
                full(wxo)]
    return pl.pallas_call(
        functools.partial(_mixer_kernel, tiles_per_seq),
        grid=(t // tm,),
        in_specs=in_specs,
        out_specs=row(D_MODEL),
        out_shape=jax.ShapeDtypeStruct((t, D_MODEL), F32),
        scratch_shapes=[pltpu.VMEM((SUBLANES, tm + HALO, CONV_CH), F32), pltpu.VMEM((tm, CONV_CH), F32)],
        compiler_params=pltpu.CompilerParams(dimension_semantics=("parallel",),
                                             vmem_limit_bytes=VMEM_LIMIT),
        name="mixer",
    )(x2, z, z, gates, o, cw, cb, lng, lnb, wco, wmo, wout, xg, wxq, mkt, mv, wxo)


def _mlp_kernel(h_ref, g_ref, w1_ref, w2_ref, fg_ref, y_ref):
    h = h_ref[...]
    u = _rms(h, g_ref[...]).astype(BF16)
    chunk = 1024
    acc = h
    for c in range(0, D_FF, chunk):
        a = jnp.maximum(_dot(u, w1_ref[:, c:c + chunk]), 0.0)
        acc = acc + _dot((a * a).astype(BF16), w2_ref[c:c + chunk, :])
    y_ref[...] = _rms(acc, fg_ref[...])


def _mlp(h, g, w1, w2, fg):
    t = h.shape[0]
    tm = MLP_TILE
    full = lambda arr: pl.BlockSpec(arr.shape, lambda i: (0,) * arr.ndim, pipeline_mode=pl.Buffered(1))
    row = pl.BlockSpec((tm, D_MODEL), lambda i: (i, 0))
    return pl.pallas_call(
        _mlp_kernel,
        grid=(t // tm,),
        in_specs=[row, full(g), full(w1), full(w2), full(fg)],
        out_specs=row,
        out_shape=jax.ShapeDtypeStruct((t, D_MODEL), F32),
        compiler_params=pltpu.CompilerParams(dimension_semantics=("parallel",),
                                             vmem_limit_bytes=VMEM_LIMIT),
        name="mlp",
    )(h, g, w1, w2, fg)


def _rope_placement():
    e = np.zeros((HEAD_PAD, MLA_HEADS * HEAD_PAD), np.float32)
    for h in range(MLA_HEADS):
        for j in range(QK_ROPE):
            e[j, h * HEAD_PAD + QK_NOPE + j] = 1.0
    return jnp.asarray(e, BF16)


def kernel(x, mem, positions, norm_mix_g, w_in, conv_w, conv_b, conv_ln_g, conv_ln_b, w_conv_out, q_norm_g, w_uq, kv_norm_g, w_ukv, w_mla_out, w_out, norm_xattn_g, norm_mem_g, w_xq, w_xkv, w_xo, norm_mlp_g, w_mlp1, w_mlp2, final_norm_g):
    batch, seq, _ = x.shape
    mem_len = mem.shape[1]
    t = batch * seq
    assert w_in.shape[0] == 1, "single-layer block (the final rmsnorm is fused into the MLP kernel)"
    assert seq % KV_CHUNK == 0 and seq % TOKEN_TILE == 0 and t % ROPE_TILE == 0 and t % MLP_TILE == 0

    inv_freq = ROPE_THETA ** (-jnp.arange(ROPE_HALF, dtype=F32) / ROPE_HALF)
    cos_t, sin_t, cs = _rope_tables(positions.reshape(1, t), inv_freq.reshape(ROPE_HALF, 1))
    place = _rope_placement()
    row = lambda v: v.reshape(1, -1)

    h = x.reshape(t, D_MODEL)
    wi = _regroup_w_in(w_in[0])
    wuq = w_uq[0].astype(BF16).reshape(Q_LORA, MLA_HEADS, QK_NOPE + QK_ROPE)
    wqt = jnp.pad(wuq, ((0, 0), (0, 0), (0, HEAD_PAD - QK_NOPE - QK_ROPE))).reshape(Q_LORA, -1).T
    wukv = w_ukv[0].astype(BF16).reshape(KV_LORA, MLA_HEADS, QK_NOPE + V_DIM)
    wk = jnp.pad(wukv[:, :, :QK_NOPE], ((0, 0), (0, 0), (0, HEAD_PAD - QK_NOPE))).reshape(KV_LORA, -1)
    wvt = jnp.pad(wukv[:, :, QK_NOPE:], ((0, 0), (0, 0), (0, V_ROWS - V_DIM))).reshape(KV_LORA, -1).T
    ones = np.zeros((MLA_HEADS, V_ROWS, 1), np.float32)
    ones[:, V_DIM] = 1.0
    ones = jnp.asarray(ones.reshape(MLA_HEADS * V_ROWS, 1))

    z, gates, qt, kp, vt3 = _in_proj(
        h, row(norm_mix_g[0]), wi, row(q_norm_g[0]), row(kv_norm_g[0]), wqt, wk, wvt, ones, place,
        cos_t, sin_t, cs)

    k3 = kp.reshape(t // KV_CHUNK, KV_CHUNK, MLA_HEADS * HEAD_PAD)
    o = _attention(qt, k3, vt3, batch, seq)

    xw = X_HEADS * X_HEAD_DIM
    wxkv = w_xkv[0].astype(BF16)
    mkt, mv = _mem_kv(mem.reshape(batch * mem_len, D_MODEL), row(norm_mem_g[0]),
                      wxkv[:, :xw].T, wxkv[:, xw:], batch, mem_len)

    cw = jnp.pad(conv_w[0], ((0, HALO - CONV_WIDTH), (0, 0)))
    h = _mixer(h, z, gates, o, cw, row(conv_b[0]), row(conv_ln_g[0]), row(conv_ln_b[0]),
               w_conv_out[0].astype(BF16), w_mla_out[0].astype(BF16), w_out[0].astype(BF16),
               row(norm_xattn_g[0]), w_xq[0].astype(BF16), mkt, mv, w_xo[0].astype(BF16), seq)
    h = _mlp(h, row(norm_mlp_g[0]), w_mlp1[0].astype(BF16), w_mlp2[0].astype(BF16),
             row(final_norm_g))
    return h.reshape(batch, seq, D_MODEL)
```

```python
import functools

import numpy as np
import jax
import jax.numpy as jnp
from jax import lax
from jax.experimental import pallas as pl
from jax.experimental.pallas import tpu as pltpu

D_MODEL = 1024
CONV_CH = D_MODEL // 2
CONV_WIDTH = 31
MLA_HEADS = 8
QK_NOPE = D_MODEL // 16
QK_ROPE = D_MODEL // 32
V_DIM = D_MODEL // 16
Q_LORA = 3 * D_MODEL // 8
KV_LORA = D_MODEL // 4
X_HEADS = 4
X_HEAD_DIM = D_MODEL // 8
D_FF = 4 * D_MODEL
ROPE_THETA = 10000.0
EPS = 1e-6

HEAD_PAD = 128
V_ROWS = V_DIM + 16
ATTN_HEADS = 4
ROPE_HALF = QK_ROPE // 2
SUBLANES = 8
HALO = 32
TOKEN_TILE = 512
MLP_TILE = 1024
Q_TILE = 512
KV_CHUNK = 512
ROPE_TILE = 2048
LOG2E = 1.4426950408889634


def _column_ranges(widths):
    out, start = {}, 0
    for name, width in widths:
        out[name] = (start, start + width)
        start += width
    return out


IN_COLS = _column_ranges([("glu_a", CONV_CH), ("glu_g", CONV_CH), ("gates", 2 * D_MODEL),
                          ("c_q", Q_LORA), ("c_kv", KV_LORA), ("k_rope", HEAD_PAD)])
MASK_VALUE = -1e30
VMEM_LIMIT = 56 * 1024 * 1024

F32 = jnp.float32
BF16 = jnp.bfloat16


def _rms(x, g):
    return x * lax.rsqrt(jnp.mean(x * x, axis=-1, keepdims=True) + EPS) * g


def _sigmoid(x):
    return 1.0 / (1.0 + jnp.exp(-x))


def _dot(a, b):
    return jnp.dot(a, b, preferred_element_type=F32)


def _dot_nt(a, b):
    return lax.dot_general(a, b, (((1,), (1,)), ((), ())), preferred_element_type=F32)


def _rope_kernel(pos_ref, invf_ref, cos_ref, sin_ref, cs_ref):
    ang = pos_ref[...].astype(F32) * invf_ref[...]
    c = jnp.cos(ang)
    s = jnp.sin(ang)
    cos_ref[...] = c
    sin_ref[...] = s
    pad = jnp.zeros((HEAD_PAD - 2 * QK_ROPE, ang.shape[1]), F32)
    cs_ref[...] = jnp.concatenate([c, c, -s, s, pad], axis=0).T


def _rope_tables(pos_row, inv_freq_col):
    t = pos_row.shape[1]
    out = jax.ShapeDtypeStruct((ROPE_HALF, t), F32)
    spec = pl.BlockSpec((ROPE_HALF, ROPE_TILE), lambda i: (0, i))
    return pl.pallas_call(
        _rope_kernel,
        grid=(t // ROPE_TILE,),
        in_specs=[pl.BlockSpec((1, ROPE_TILE), lambda i: (0, i)),
                  pl.BlockSpec((ROPE_HALF, 1), lambda i: (0, 0))],
        out_specs=[spec, spec, pl.BlockSpec((ROPE_TILE, HEAD_PAD), lambda i: (i, 0))],
        out_shape=[out, out, jax.ShapeDtypeStruct((t, HEAD_PAD), F32)],
        name="rope_tables",
    )(pos_row, inv_freq_col)


def _regroup_kernel(w_ref, o_ref):
    c0 = 2 * CONV_CH
    c2 = c0 + Q_LORA + KV_LORA
    c3 = c2 + QK_ROPE

    def put(name, value):
        o_ref[IN_COLS[name][0]:IN_COLS[name][0] + value.shape[0], :] = value.astype(BF16)

    put("glu_a", w_ref[0, :CONV_CH, :])
    put("glu_g", w_ref[0, CONV_CH:c0, :])
    put("gates", w_ref[0, c3:, :])
    put("c_q", w_ref[0, c0:c0 + Q_LORA, :])
    put("c_kv", w_ref[0, c0 + Q_LORA:c2, :])
    kr = w_ref[0, c2:c3, :]
    zeros = jnp.zeros((HEAD_PAD - 2 * QK_ROPE, kr.shape[1]), F32)
    put("k_rope", jnp.concatenate([kr, kr[ROPE_HALF:], kr[:ROPE_HALF], zeros], axis=0))


def _regroup_w_in(wt):
    lanes = 256
    rows = IN_COLS["k_rope"][1]
    return pl.pallas_call(
        _regroup_kernel,
        grid=(wt.shape[2] // lanes,),
        in_specs=[pl.BlockSpec((1, wt.shape[1], lanes), lambda i: (0, 0, i))],
        out_specs=pl.BlockSpec((rows, lanes), lambda i: (0, i)),
        out_shape=jax.ShapeDtypeStruct((rows, wt.shape[2]), BF16),
        name="regroup_w_in",
    )(wt)


def _in_proj_kernel(x_ref, g_ref, w_ref, qg_ref, kvg_ref, wqt_ref, wk_ref, wvt_ref, ones_ref, place_ref,
                    cos_ref, sin_ref, cs_ref,
                    z_ref, gates_ref, qt_ref, k_ref, vt_ref):
    u = _rms(x_ref[...], g_ref[...]).astype(BF16)
    proj = lambda name: _dot_nt(u, w_ref[IN_COLS[name][0]:IN_COLS[name][1], :])
    z_ref[...] = (proj("glu_a") * _sigmoid(proj("glu_g"))).astype(BF16)
    gates_ref[...] = _sigmoid(proj("gates")).astype(BF16)

    cqn = _rms(proj("c_q"), qg_ref[...]).astype(BF16)
    qt = _dot_nt(wqt_ref[...], cqn)
    scale = (QK_NOPE + QK_ROPE) ** -0.5 * LOG2E
    c = cos_ref[...]
    s = sin_ref[...]
    for h in range(MLA_HEADS):
        b = h * HEAD_PAD
        r1 = b + QK_NOPE
        r2 = r1 + ROPE_HALF
        r3 = r2 + ROPE_HALF
        t1 = qt[r1:r2]
        t2 = qt[r2:r3]
        qt_ref[b:r1, :] = (qt[b:r1] * scale).astype(BF16)
        qt_ref[r1:r2, :] = ((t1 * c - t2 * s) * scale).astype(BF16)
        qt_ref[r2:r3, :] = ((t2 * c + t1 * s) * scale).astype(BF16)
        qt_ref[r3:b + HEAD_PAD, :] = (qt[r3:b + HEAD_PAD] * scale).astype(BF16)

    ckvn = _rms(proj("c_kv"), kvg_ref[...]).astype(BF16)
    t = proj("k_rope") * cs_ref[...]
    rot = t + pltpu.roll(t, HEAD_PAD - QK_ROPE, 1)
    k_ref[...] = (_dot(ckvn, wk_ref[...]) + _dot(rot.astype(BF16), place_ref[...])).astype(BF16)
    vt_ref[0] = (_dot_nt(wvt_ref[...], ckvn) + ones_ref[...]).astype(BF16)


def _in_proj(x2, g, w, qg, kvg, wqt, wk, wvt, ones, place, cos_t, sin_t, cs):
    t = x2.shape[0]
    tm = KV_CHUNK
    n = t // tm
    full = lambda arr: pl.BlockSpec(arr.shape, lambda i: (0,) * arr.ndim)
    row = lambda w: pl.BlockSpec((tm, w), lambda i: (i, 0))
    in_specs = [row(D_MODEL), full(g), full(w), full(qg), full(kvg), full(wqt), full(wk), full(wvt),
                full(ones), full(place),
                pl.BlockSpec((ROPE_HALF, tm), lambda i: (0, i)),
                pl.BlockSpec((ROPE_HALF, tm), lambda i: (0, i)),
                row(HEAD_PAD)]
    out_shape = [jax.ShapeDtypeStruct((t, CONV_CH), BF16),
                 jax.ShapeDtypeStruct((t, 2 * D_MODEL), BF16),
                 jax.ShapeDtypeStruct((MLA_HEADS * HEAD_PAD, t), BF16),
                 jax.ShapeDtypeStruct((t, MLA_HEADS * HEAD_PAD), BF16),
                 jax.ShapeDtypeStruct((n, MLA_HEADS * V_ROWS, tm), BF16)]
    out_specs = [row(CONV_CH), row(2 * D_MODEL),
                 pl.BlockSpec((MLA_HEADS * HEAD_PAD, tm), lambda i: (0, i)),
                 row(MLA_HEADS * HEAD_PAD),
                 pl.BlockSpec((1, MLA_HEADS * V_ROWS, tm), lambda i: (i, 0, 0))]
    return pl.pallas_call(
        _in_proj_kernel,
        grid=(n,),
        in_specs=in_specs,
        out_specs=out_specs,
        out_shape=out_shape,
        compiler_params=pltpu.CompilerParams(dimension_semantics=("parallel",),
                                             vmem_limit_bytes=VMEM_LIMIT),
        name="in_proj",
    )(x2, g, w, qg, kvg, wqt, wk, wvt, ones, place, cos_t, sin_t, cs)


def _attn_kernel(qt_ref, k_ref, vt_ref, o_ref, s_scr, p_scr, acc_scr):
    qi = pl.program_id(2)
    tk = k_ref.shape[1]
    tq = qt_ref.shape[1]
    heads = range(ATTN_HEADS)
    chunks_per_q = tk // tq
    diag = qi // chunks_per_q
    key_idx = lax.broadcasted_iota(jnp.int32, (tk, tq), 0)
    qry_idx = lax.broadcasted_iota(jnp.int32, (tk, tq), 1) + (qi % chunks_per_q) * tq
    causal = key_idx <= qry_idx
    qts = [qt_ref[h * HEAD_PAD:(h + 1) * HEAD_PAD, :] for h in heads]

    def scores_to(slot, kc, masked, hs=heads):
        ms = []
        for h in hs:
            s = _dot(k_ref[kc, :, h * HEAD_PAD:(h + 1) * HEAD_PAD], qts[h])
            if masked:
                s = jnp.where(causal, s, MASK_VALUE)
            s_scr[slot, h] = s
            ms.append(jnp.max(s, axis=0, keepdims=True))
        return tuple(ms)

    def softmax_to(src, dst, ms, hs=heads):
        for h, m in zip(hs, ms):
            p_scr[dst, h] = jnp.exp2(s_scr[src, h] - m).astype(BF16)

    def accumulate(m_run, slot, kc, ms, hs=heads):
        new = []
        for h, m_old, m_chunk in zip(hs, m_run, ms):
            pv = _dot(vt_ref[kc, h * V_ROWS:(h + 1) * V_ROWS, :], p_scr[slot, h])
            m_new = jnp.maximum(m_old, m_chunk)
            a = jnp.exp2(m_old - m_new)
            b = jnp.exp2(m_chunk - m_new)
            acc_scr[h] = a * acc_scr[h] + b * pv
            new.append(m_new)
        return tuple(new)

    def step(cur, j, carry):
        m_run, pend_kc, pend_m, next_m = carry
        ahead_kc = jnp.minimum(j + 1, jnp.maximum(diag - 1, 0))
        new_run, ahead_m = [], []
        for h in heads:
            new_run += accumulate(m_run[h:h + 1], 1 - cur, pend_kc, pend_m[h:h + 1], (h,))
            softmax_to(1 - cur, cur, next_m[h:h + 1], (h,))
            ahead_m += scores_to(cur, ahead_kc, False, (h,))
        return tuple(new_run), j, next_m, tuple(ahead_m)

    def body(j, carry):
        return lax.cond(j % 2 == 0, functools.partial(step, 0), functools.partial(step, 1), j, carry)

    acc_scr[...] = jnp.zeros(acc_scr.shape, F32)
    m_init = tuple(jnp.full((1, tq), MASK_VALUE, F32) for _ in heads)
    diag_m, first_m = [], []
    for h in heads:
        diag_m += scores_to(0, diag, True, (h,))
        first_m += scores_to(1, 0, False, (h,))
        softmax_to(0, 1, diag_m[h:h + 1], (h,))
    carry = (m_init, diag, tuple(diag_m), tuple(first_m))
    m_run, pend_kc, pend_m, _ = lax.fori_loop(0, diag, body, carry)
    accumulate(m_run, (diag + 1) % 2, pend_kc, pend_m)
    outs = [acc_scr[h, :V_DIM, :] / acc_scr[h, V_DIM:V_DIM + 1, :] for h in heads]
    o_ref[...] = jnp.concatenate(outs, axis=0).T.astype(BF16)


def _attention(qt, k3, vt3, batch, seq):
    tq = Q_TILE
    nq = seq // tq
    nk = seq // KV_CHUNK
    t = batch * seq
    return pl.pallas_call(
        _attn_kernel,
        grid=(batch, MLA_HEADS // ATTN_HEADS, nq),
        in_specs=[pl.BlockSpec((ATTN_HEADS * HEAD_PAD, tq), lambda b, hg, qi: (hg, b * nq + qi)),
                  pl.BlockSpec((nk, KV_CHUNK, ATTN_HEADS * HEAD_PAD), lambda b, hg, qi: (b, 0, hg)),
                  pl.BlockSpec((nk, ATTN_HEADS * V_ROWS, KV_CHUNK), lambda b, hg, qi: (b, hg, 0))],
        out_specs=pl.BlockSpec((tq, ATTN_HEADS * V_DIM), lambda b, hg, qi: (b * nq + qi, hg)),
        out_shape=jax.ShapeDtypeStruct((t, MLA_HEADS * V_DIM), BF16),
        scratch_shapes=[pltpu.VMEM((2, ATTN_HEADS, KV_CHUNK, tq), F32),
                        pltpu.VMEM((2, ATTN_HEADS, KV_CHUNK, tq), BF16),
                        pltpu.VMEM((ATTN_HEADS, V_ROWS, tq), F32)],
        compiler_params=pltpu.CompilerParams(
            dimension_semantics=("parallel", "parallel", "arbitrary"),
            vmem_limit_bytes=VMEM_LIMIT),
        name="mla_attention",
    )(qt, k3, vt3)


def _mem_kernel(mem_ref, g_ref, wkt_ref, wv_ref, kt_ref, v_ref):
    mn = _rms(mem_ref[...], g_ref[...]).astype(BF16)
    kt_ref[0] = _dot_nt(wkt_ref[...], mn).astype(BF16)
    v_ref[0] = _dot(mn, wv_ref[...]).astype(BF16)


def _mem_kv(mem2, g, wkt, wv, batch, mem_len):
    xw = X_HEADS * X_HEAD_DIM
    full = lambda arr: pl.BlockSpec(arr.shape, lambda b: (0,) * arr.ndim)
    return pl.pallas_call(
        _mem_kernel,
        grid=(batch,),
        in_specs=[pl.BlockSpec((mem_len, D_MODEL), lambda b: (b, 0)), full(g), full(wkt), full(wv)],
        out_specs=[pl.BlockSpec((1, xw, mem_len), lambda b: (b, 0, 0)),
                   pl.BlockSpec((1, mem_len, xw), lambda b: (b, 0, 0))],
        out_shape=[jax.ShapeDtypeStruct((batch, xw, mem_len), BF16),
                   jax.ShapeDtypeStruct((batch, mem_len, xw), BF16)],
        compiler_params=pltpu.CompilerParams(dimension_semantics=("parallel",),
                                             vmem_limit_bytes=VMEM_LIMIT),
        name="mem_kv",
    )(mem2, g, wkt, wv)


def _mixer_kernel(tiles_per_seq, x_ref, z_ref, halo_ref, gates_ref, o_ref,
                  cw_ref, cb_ref, lng_ref, lnb_ref, wco_ref, wmo_ref, wout_ref,
                  xg_ref, wxq_ref, mkt_ref, mv_ref, wxo_ref,
                  h_ref, zext_ref, conv_ref):
    tm = x_ref.shape[0]
    first = (pl.program_id(0) % tiles_per_seq) == 0
    ext = tm + HALO
    zext_ref[0, 0:HALO, :] = jnp.where(first, 0.0, halo_ref[...].astype(F32))
    zext_ref[0, HALO:, :] = z_ref[...].astype(F32)
    for s in range(1, SUBLANES):
        zext_ref[s, 0:ext - SUBLANES, :] = zext_ref[0, s:s + ext - SUBLANES, :]

    rows = 64
    off = HALO - (CONV_WIDTH - 1)
    for c in range(CONV_CH // 128):
        cs = slice(c * 128, (c + 1) * 128)
        for r in range(0, tm, rows):
            acc = jnp.broadcast_to(cb_ref[:, cs], (rows, 128))
            for w in range(CONV_WIDTH):
                shift, base = (off + w) % SUBLANES, (off + w) // SUBLANES * SUBLANES
                acc = acc + zext_ref[shift, r + base:r + base + rows, cs] * cw_ref[w:w + 1, cs]
            conv_ref[r:r + rows, cs] = acc

    y = conv_ref[...]
    mu = jnp.mean(y, axis=-1, keepdims=True)
    yc = y - mu
    var = jnp.mean(yc * yc, axis=-1, keepdims=True)
    y = yc * lax.rsqrt(var + EPS) * lng_ref[...] + lnb_ref[...]
    y = y * _sigmoid(y)
    conv_out = _dot(y.astype(BF16), wco_ref[...])
    mla_out = _dot(o_ref[...], wmo_ref[...])
    gates = gates_ref[...].astype(F32)
    merged = gates[:, :D_MODEL] * conv_out + gates[:, D_MODEL:] * mla_out
    h1 = x_ref[...] + _dot(merged.astype(BF16), wout_ref[...])

    u = _rms(h1, xg_ref[...]).astype(BF16)
    q = (_dot(u, wxq_ref[...]) * (X_HEAD_DIM ** -0.5)).astype(BF16)
    heads = []
    for h in range(X_HEADS):
        hs = slice(h * X_HEAD_DIM, (h + 1) * X_HEAD_DIM)
        s = _dot(q[:, hs], mkt_ref[0, hs, :])
        p = jnp.exp(s - jnp.max(s, axis=-1, keepdims=True))
        p = p / jnp.sum(p, axis=-1, keepdims=True)
        heads.append(_dot(p.astype(BF16), mv_ref[0, :, hs]))
    xo = jnp.concatenate(heads, axis=1).astype(BF16)
    h_ref[...] = h1 + _dot(xo, wxo_ref[...])


def _mixer(x2, z, gates, o, cw, cb, lng, lnb, wco, wmo, wout, xg, wxq, mkt, mv, wxo, seq):
    t = x2.shape[0]
    tm = TOKEN_TILE
    tiles_per_seq = seq // tm
    halo_per_tile = tm // HALO
    mem_len = mv.shape[1]
    xw = X_HEADS * X_HEAD_DIM
    full = lambda arr: pl.BlockSpec(arr.shape, lambda i: (0,) * arr.ndim)
    row = lambda w: pl.BlockSpec((tm, w), lambda i: (i, 0))
    in_specs = [row(D_MODEL), row(CONV_CH),
                pl.BlockSpec((HALO, CONV_CH), lambda i: (jnp.maximum(i * halo_per_tile - 1, 0), 0)),
                row(2 * D_MODEL), row(MLA_HEADS * V_DIM),
                full(cw), full(cb), full(lng), full(lnb), full(wco), full(wmo), full(wout),
                full(xg), full(wxq),
                pl.BlockSpec((1, xw, mem_len), lambda i: (i // tiles_per_seq, 0, 0)),
                pl.BlockSpec((1, mem_len, xw), lambda i: (i // tiles_per_seq, 0, 0)),
                full(wxo)]
    return pl.pallas_call(
        functools.partial(_mixer_kernel, tiles_per_seq),
        grid=(t // tm,),
        in_specs=in_specs,
        out_specs=row(D_MODEL),
        out_shape=jax.ShapeDtypeStruct((t, D_MODEL), F32),
        scratch_shapes=[pltpu.VMEM((SUBLANES, tm + HALO, CONV_CH), F32), pltpu.VMEM((tm, CONV_CH), F32)],
        compiler_params=pltpu.CompilerParams(dimension_semantics=("parallel",),
                                             vmem_limit_bytes=VMEM_LIMIT),
        name="mixer",
    )(x2, z, z, gates, o, cw, cb, lng, lnb, wco, wmo, wout, xg, wxq, mkt, mv, wxo)


def _mlp_kernel(h_ref, g_ref, w1_ref, w2_ref, fg_ref, y_ref):
    h = h_ref[...]
    u = _rms(h, g_ref[...]).astype(BF16)
    chunk = 1024
    acc = h
    for c in range(0, D_FF, chunk):
        a = jnp.maximum(_dot(u, w1_ref[:, c:c + chunk]), 0.0)
        acc = acc + _dot((a * a).astype(BF16), w2_ref[c:c + chunk, :])
    y_ref[...] = _rms(acc, fg_ref[...])


def _mlp(h, g, w1, w2, fg):
    t = h.shape[0]
    tm = MLP_TILE
    full = lambda arr: pl.BlockSpec(arr.shape, lambda i: (0,) * arr.ndim, pipeline_mode=pl.Buffered(1))
    row = pl.BlockSpec((tm, D_MODEL), lambda i: (i, 0))
    return pl.pallas_call(
        _mlp_kernel,
        grid=(t // tm,),
        in_specs=[row, full(g), full(w1), full(w2), full(fg)],
        out_specs=row,
        out_shape=jax.ShapeDtypeStruct((t, D_MODEL), F32),
        compiler_params=pltpu.CompilerParams(dimension_semantics=("parallel",),
                                             vmem_limit_bytes=VMEM_LIMIT),
        name="mlp",
    )(h, g, w1, w2, fg)


def _rope_placement():
    e = np.zeros((HEAD_PAD, MLA_HEADS * HEAD_PAD), np.float32)
    for h in range(MLA_HEADS):
        for j in range(QK_ROPE):
            e[j, h * HEAD_PAD + QK_NOPE + j] = 1.0
    return jnp.asarray(e, BF16)


def kernel(x, mem, positions, norm_mix_g, w_in, conv_w, conv_b, conv_ln_g, conv_ln_b, w_conv_out, q_norm_g, w_uq, kv_norm_g, w_ukv, w_mla_out, w_out, norm_xattn_g, norm_mem_g, w_xq, w_xkv, w_xo, norm_mlp_g, w_mlp1, w_mlp2, final_norm_g):
    batch, seq, _ = x.shape
    mem_len = mem.shape[1]
    t = batch * seq
    assert w_in.shape[0] == 1, "single-layer block (the final rmsnorm is fused into the MLP kernel)"
    assert seq % KV_CHUNK == 0 and seq % TOKEN_TILE == 0 and t % ROPE_TILE == 0 and t % MLP_TILE == 0

    inv_freq = ROPE_THETA ** (-jnp.arange(ROPE_HALF, dtype=F32) / ROPE_HALF)
    cos_t, sin_t, cs = _rope_tables(positions.reshape(1, t), inv_freq.reshape(ROPE_HALF, 1))
    place = _rope_placement()
    row = lambda v: v.reshape(1, -1)

    h = x.reshape(t, D_MODEL)
    wi = _regroup_w_in(jnp.swapaxes(w_in, 1, 2))
    wuq = w_uq[0].astype(BF16).reshape(Q_LORA, MLA_HEADS, QK_NOPE + QK_ROPE)
    wqt = jnp.pad(wuq, ((0, 0), (0, 0), (0, HEAD_PAD - QK_NOPE - QK_ROPE))).reshape(Q_LORA, -1).T
    wukv = w_ukv[0].astype(BF16).reshape(KV_LORA, MLA_HEADS, QK_NOPE + V_DIM)
    wk = jnp.pad(wukv[:, :, :QK_NOPE], ((0, 0), (0, 0), (0, HEAD_PAD - QK_NOPE))).reshape(KV_LORA, -1)
    wvt = jnp.pad(wukv[:, :, QK_NOPE:], ((0, 0), (0, 0), (0, V_ROWS - V_DIM))).reshape(KV_LORA, -1).T
    ones = np.zeros((MLA_HEADS, V_ROWS, 1), np.float32)
    ones[:, V_DIM] = 1.0
    ones = jnp.asarray(ones.reshape(MLA_HEADS * V_ROWS, 1))

    z, gates, qt, kp, vt3 = _in_proj(
        h, row(norm_mix_g[0]), wi, row(q_norm_g[0]), row(kv_norm_g[0]), wqt, wk, wvt, ones, place,
        cos_t, sin_t, cs)

    k3 = kp.reshape(t // KV_CHUNK, KV_CHUNK, MLA_HEADS * HEAD_PAD)
    o = _attention(qt, k3, vt3, batch, seq)

    xw = X_HEADS * X_HEAD_DIM
    wxkv = w_xkv[0].astype(BF16)
    mkt, mv = _mem_kv(mem.reshape(batch * mem_len, D_MODEL), row(norm_mem_g[0]),
                      wxkv[:, :xw].T, wxkv[:, xw:], batch, mem_len)

    cw = jnp.pad(conv_w[0], ((0, HALO - CONV_WIDTH), (0, 0)))
    h = _mixer(h, z, gates, o, cw, row(conv_b[0]), row(conv_ln_g[0]), row(conv_ln_b[0]),
               w_conv_out[0].astype(BF16), w_mla_out[0].astype(BF16), w_out[0].astype(BF16),
               row(norm_xattn_g[0]), w_xq[0].astype(BF16), mkt, mv, w_xo[0].astype(BF16), seq)
    h = _mlp(h, row(norm_mlp_g[0]), w_mlp1[0].astype(BF16), w_mlp2[0].astype(BF16),
             row(final_norm_g))
    return h.reshape(batch, seq, D_MODEL)
```

```python
import functools

import numpy as np
import jax
import jax.numpy as jnp
from jax import lax
from jax.experimental import pallas as pl
from jax.experimental.pallas import tpu as pltpu

D_MODEL = 1024
CONV_CH = D_MODEL // 2
CONV_WIDTH = 31
MLA_HEADS = 8
QK_NOPE = D_MODEL // 16
QK_ROPE = D_MODEL // 32
V_DIM = D_MODEL // 16
Q_LORA = 3 * D_MODEL // 8
KV_LORA = D_MODEL // 4
X_HEADS = 4
X_HEAD_DIM = D_MODEL // 8
D_FF = 4 * D_MODEL
ROPE_THETA = 10000.0
EPS = 1e-6

HEAD_PAD = 128
V_ROWS = V_DIM + 16
ATTN_HEADS = 4
ROPE_HALF = QK_ROPE // 2
SUBLANES = 8
HALO = 32
TOKEN_TILE = 512
MLP_TILE = 1024
Q_TILE = 512
KV_CHUNK = 512
ROPE_TILE = 2048
LOG2E = 1.4426950408889634


def _column_ranges(widths):
    out, start = {}, 0
    for name, width in widths:
        out[name] = (start, start + width)
        start += width
    return out


IN_COLS = _column_ranges([("glu_a", CONV_CH), ("glu_g", CONV_CH), ("gates", 2 * D_MODEL),
                          ("c_q", Q_LORA), ("c_kv", KV_LORA), ("k_rope", HEAD_PAD)])
MASK_VALUE = -1e30
VMEM_LIMIT = 56 * 1024 * 1024

F32 = jnp.float32
BF16 = jnp.bfloat16


def _rms(x, g):
    return x * lax.rsqrt(jnp.mean(x * x, axis=-1, keepdims=True) + EPS) * g


def _sigmoid(x):
    return 1.0 / (1.0 + jnp.exp(-x))


def _dot(a, b):
    return jnp.dot(a, b, preferred_element_type=F32)


def _dot_nt(a, b):
    return lax.dot_general(a, b, (((1,), (1,)), ((), ())), preferred_element_type=F32)


def _rope_kernel(pos_ref, invf_ref, cos_ref, sin_ref, cs_ref):
    ang = pos_ref[...].astype(F32) * invf_ref[...]
    c = jnp.cos(ang)
    s = jnp.sin(ang)
    cos_ref[...] = c
    sin_ref[...] = s
    pad = jnp.zeros((HEAD_PAD - 2 * QK_ROPE, ang.shape[1]), F32)
    cs_ref[...] = jnp.concatenate([c, c, -s, s, pad], axis=0).T


def _rope_tables(pos_row, inv_freq_col):
    t = pos_row.shape[1]
    out = jax.ShapeDtypeStruct((ROPE_HALF, t), F32)
    spec = pl.BlockSpec((ROPE_HALF, ROPE_TILE), lambda i: (0, i))
    return pl.pallas_call(
        _rope_kernel,
        grid=(t // ROPE_TILE,),
        in_specs=[pl.BlockSpec((1, ROPE_TILE), lambda i: (0, i)),
                  pl.BlockSpec((ROPE_HALF, 1), lambda i: (0, 0))],
        out_specs=[spec, spec, pl.BlockSpec((ROPE_TILE, HEAD_PAD), lambda i: (i, 0))],
        out_shape=[out, out, jax.ShapeDtypeStruct((t, HEAD_PAD), F32)],
        name="rope_tables",
    )(pos_row, inv_freq_col)


def _regroup_kernel(w_ref, o_ref):
    c0 = 2 * CONV_CH
    c2 = c0 + Q_LORA + KV_LORA
    c3 = c2 + QK_ROPE

    def put(name, value):
        o_ref[IN_COLS[name][0]:IN_COLS[name][0] + value.shape[0], :] = value.astype(BF16)

    put("glu_a", w_ref[0, :CONV_CH, :])
    put("glu_g", w_ref[0, CONV_CH:c0, :])
    put("gates", w_ref[0, c3:, :])
    put("c_q", w_ref[0, c0:c0 + Q_LORA, :])
    put("c_kv", w_ref[0, c0 + Q_LORA:c2, :])
    kr = w_ref[0, c2:c3, :]
    zeros = jnp.zeros((HEAD_PAD - 2 * QK_ROPE, kr.shape[1]), F32)
    put("k_rope", jnp.concatenate([kr, kr[ROPE_HALF:], kr[:ROPE_HALF], zeros], axis=0))


def _regroup_w_in(wt):
    lanes = 256
    rows = IN_COLS["k_rope"][1]
    return pl.pallas_call(
        _regroup_kernel,
        grid=(wt.shape[2] // lanes,),
        in_specs=[pl.BlockSpec((1, wt.shape[1], lanes), lambda i: (0, 0, i))],
        out_specs=pl.BlockSpec((rows, lanes), lambda i: (0, i)),
        out_shape=jax.ShapeDtypeStruct((rows, wt.shape[2]), BF16),
        name="regroup_w_in",
    )(wt)


def _in_proj_kernel(x_ref, g_ref, w_ref, qg_ref, kvg_ref, wqt_ref, wk_ref, wvt_ref, ones_ref, place_ref,
                    cos_ref, sin_ref, cs_ref,
                    z_ref, gates_ref, qt_ref, k_ref, vt_ref):
    u = _rms(x_ref[...], g_ref[...]).astype(BF16)
    proj = lambda name: _dot_nt(u, w_ref[IN_COLS[name][0]:IN_COLS[name][1], :])
    z_ref[...] = (proj("glu_a") * _sigmoid(proj("glu_g"))).astype(BF16)
    gates_ref[...] = _sigmoid(proj("gates")).astype(BF16)

    cqn = _rms(proj("c_q"), qg_ref[...]).astype(BF16)
    qt = _dot_nt(wqt_ref[...], cqn)
    scale = (QK_NOPE + QK_ROPE) ** -0.5 * LOG2E
    c = cos_ref[...]
    s = sin_ref[...]
    for h in range(MLA_HEADS):
        b = h * HEAD_PAD
        r1 = b + QK_NOPE
        r2 = r1 + ROPE_HALF
        r3 = r2 + ROPE_HALF
        t1 = qt[r1:r2]
        t2 = qt[r2:r3]
        qt_ref[b:r1, :] = (qt[b:r1] * scale).astype(BF16)
        qt_ref[r1:r2, :] = ((t1 * c - t2 * s) * scale).astype(BF16)
        qt_ref[r2:r3, :] = ((t2 * c + t1 * s) * scale).astype(BF16)
        qt_ref[r3:b + HEAD_PAD, :] = (qt[r3:b + HEAD_PAD] * scale).astype(BF16)

    ckvn = _rms(proj("c_kv"), kvg_ref[...]).astype(BF16)
    t = proj("k_rope") * cs_ref[...]
    rot = t + pltpu.roll(t, HEAD_PAD - QK_ROPE, 1)
    k_ref[...] = (_dot(ckvn, wk_ref[...]) + _dot(rot.astype(BF16), place_ref[...])).astype(BF16)
    vt_ref[0] = (_dot_nt(wvt_ref[...], ckvn) + ones_ref[...]).astype(BF16)


def _in_proj(x2, g, w, qg, kvg, wqt, wk, wvt, ones, place, cos_t, sin_t, cs):
    t = x2.shape[0]
    tm = KV_CHUNK
    n = t // tm
    full = lambda arr: pl.BlockSpec(arr.shape, lambda i: (0,) * arr.ndim)
    row = lambda w: pl.BlockSpec((tm, w), lambda i: (i, 0))
    in_specs = [row(D_MODEL), full(g), full(w), full(qg), full(kvg), full(wqt), full(wk), full(wvt),
                full(ones), full(place),
                pl.BlockSpec((ROPE_HALF, tm), lambda i: (0, i)),
                pl.BlockSpec((ROPE_HALF, tm), lambda i: (0, i)),
                row(HEAD_PAD)]
    out_shape = [jax.ShapeDtypeStruct((t, CONV_CH), BF16),
                 jax.ShapeDtypeStruct((t, 2 * D_MODEL), BF16),
                 jax.ShapeDtypeStruct((MLA_HEADS * HEAD_PAD, t), BF16),
                 jax.ShapeDtypeStruct((t, MLA_HEADS * HEAD_PAD), BF16),
                 jax.ShapeDtypeStruct((n, MLA_HEADS * V_ROWS, tm), BF16)]
    out_specs = [row(CONV_CH), row(2 * D_MODEL),
                 pl.BlockSpec((MLA_HEADS * HEAD_PAD, tm), lambda i: (0, i)),
                 row(MLA_HEADS * HEAD_PAD),
                 pl.BlockSpec((1, MLA_HEADS * V_ROWS, tm), lambda i: (i, 0, 0))]
    return pl.pallas_call(
        _in_proj_kernel,
        grid=(n,),
        in_specs=in_specs,
        out_specs=out_specs,
        out_shape=out_shape,
        compiler_params=pltpu.CompilerParams(dimension_semantics=("parallel",),
                                             vmem_limit_bytes=VMEM_LIMIT),
        name="in_proj",
    )(x2, g, w, qg, kvg, wqt, wk, wvt, ones, place, cos_t, sin_t, cs)


def _attn_kernel(qt_ref, k_ref, vt_ref, o_ref, s_scr, p_scr, acc_scr):
    qi = pl.program_id(2)
    tk = k_ref.shape[1]
    tq = qt_ref.shape[1]
    heads = range(ATTN_HEADS)
    chunks_per_q = tk // tq
    diag = qi // chunks_per_q
    key_idx = lax.broadcasted_iota(jnp.int32, (tk, tq), 0)
    qry_idx = lax.broadcasted_iota(jnp.int32, (tk, tq), 1) + (qi % chunks_per_q) * tq
    causal = key_idx <= qry_idx
    qts = [qt_ref[h * HEAD_PAD:(h + 1) * HEAD_PAD, :] for h in heads]

    def scores_to(slot, kc, masked, hs=heads):
        ms = []
        for h in hs:
            s = _dot(k_ref[kc, :, h * HEAD_PAD:(h + 1) * HEAD_PAD], qts[h])
            if masked:
                s = jnp.where(causal, s, MASK_VALUE)
            s_scr[slot, h] = s
            ms.append(jnp.max(s, axis=0, keepdims=True))
        return tuple(ms)

    def softmax_to(src, dst, ms, hs=heads):
        for h, m in zip(hs, ms):
            p_scr[dst, h] = jnp.exp2(s_scr[src, h] - m).astype(BF16)

    def accumulate(m_run, slot, kc, ms, hs=heads):
        new = []
        for h, m_old, m_chunk in zip(hs, m_run, ms):
            pv = _dot(vt_ref[kc, h * V_ROWS:(h + 1) * V_ROWS, :], p_scr[slot, h])
            m_new = jnp.maximum(m_old, m_chunk)
            a = jnp.exp2(m_old - m_new)
            b = jnp.exp2(m_chunk - m_new)
            acc_scr[h] = a * acc_scr[h] + b * pv
            new.append(m_new)
        return tuple(new)

    def step(cur, look_ahead, j, carry):
        m_run, pend_kc, pend_m, next_m = carry
        new_run, ahead_m = [], []
        for h in heads:
            new_run += accumulate(m_run[h:h + 1], 1 - cur, pend_kc, pend_m[h:h + 1], (h,))
            softmax_to(1 - cur, cur, next_m[h:h + 1], (h,))
            if look_ahead:
                ahead_m += scores_to(cur, j + 1, False, (h,))
        return tuple(new_run), j, next_m, (tuple(ahead_m) if look_ahead else next_m)

    branches = [functools.partial(step, cur, look_ahead) for look_ahead in (True, False) for cur in (0, 1)]

    def body(j, carry):
        return lax.switch(2 * (j == diag - 1).astype(jnp.int32) + j % 2, branches, j, carry)

    acc_scr[...] = jnp.zeros(acc_scr.shape, F32)
    m_init = tuple(jnp.full((1, tq), MASK_VALUE, F32) for _ in heads)
    diag_m, first_m = [], []
    for h in heads:
        diag_m += scores_to(0, diag, True, (h,))
        first_m += scores_to(1, 0, False, (h,))
        softmax_to(0, 1, diag_m[h:h + 1], (h,))
    carry = (m_init, diag, tuple(diag_m), tuple(first_m))
    m_run, pend_kc, pend_m, _ = lax.fori_loop(0, diag, body, carry)
    accumulate(m_run, (diag + 1) % 2, pend_kc, pend_m)
    outs = [acc_scr[h, :V_DIM, :] / acc_scr[h, V_DIM:V_DIM + 1, :] for h in heads]
    o_ref[...] = jnp.concatenate(outs, axis=0).T.astype(BF16)


def _attention(qt, k3, vt3, batch, seq):
    tq = Q_TILE
    nq = seq // tq
    nk = seq // KV_CHUNK
    t = batch * seq
    return pl.pallas_call(
        _attn_kernel,
        grid=(batch, MLA_HEADS // ATTN_HEADS, nq),
        in_specs=[pl.BlockSpec((ATTN_HEADS * HEAD_PAD, tq), lambda b, hg, qi: (hg, b * nq + qi)),
                  pl.BlockSpec((nk, KV_CHUNK, ATTN_HEADS * HEAD_PAD), lambda b, hg, qi: (b, 0, hg)),
                  pl.BlockSpec((nk, ATTN_HEADS * V_ROWS, KV_CHUNK), lambda b, hg, qi: (b, hg, 0))],
        out_specs=pl.BlockSpec((tq, ATTN_HEADS * V_DIM), lambda b, hg, qi: (b * nq + qi, hg)),
        out_shape=jax.ShapeDtypeStruct((t, MLA_HEADS * V_DIM), BF16),
        scratch_shapes=[pltpu.VMEM((2, ATTN_HEADS, KV_CHUNK, tq), F32),
                        pltpu.VMEM((2, ATTN_HEADS, KV_CHUNK, tq), BF16),
                        pltpu.VMEM((ATTN_HEADS, V_ROWS, tq), F32)],
        compiler_params=pltpu.CompilerParams(
            dimension_semantics=("parallel", "parallel", "arbitrary"),
            vmem_limit_bytes=VMEM_LIMIT),
        name="mla_attention",
    )(qt, k3, vt3)


def _mem_kernel(mem_ref, g_ref, wkt_ref, wv_ref, kt_ref, v_ref):
    mn = _rms(mem_ref[...], g_ref[...]).astype(BF16)
    kt_ref[0] = _dot_nt(wkt_ref[...], mn).astype(BF16)
    v_ref[0] = _dot(mn, wv_ref[...]).astype(BF16)


def _mem_kv(mem2, g, wkt, wv, batch, mem_len):
    xw = X_HEADS * X_HEAD_DIM
    full = lambda arr: pl.BlockSpec(arr.shape, lambda b: (0,) * arr.ndim)
    return pl.pallas_call(
        _mem_kernel,
        grid=(batch,),
        in_specs=[pl.BlockSpec((mem_len, D_MODEL), lambda b: (b, 0)), full(g), full(wkt), full(wv)],
        out_specs=[pl.BlockSpec((1, xw, mem_len), lambda b: (b, 0, 0)),
                   pl.BlockSpec((1, mem_len, xw), lambda b: (b, 0, 0))],
        out_shape=[jax.ShapeDtypeStruct((batch, xw, mem_len), BF16),
                   jax.ShapeDtypeStruct((batch, mem_len, xw), BF16)],
        compiler_params=pltpu.CompilerParams(dimension_semantics=("parallel",),
                                             vmem_limit_bytes=VMEM_LIMIT),
        name="mem_kv",
    )(mem2, g, wkt, wv)


def _mixer_kernel(tiles_per_seq, x_ref, z_ref, halo_ref, gates_ref, o_ref,
                  cw_ref, cb_ref, lng_ref, lnb_ref, wco_ref, wmo_ref, wout_ref,
                  xg_ref, wxq_ref, mkt_ref, mv_ref, wxo_ref,
                  h_ref, zext_ref, conv_ref):
    tm = x_ref.shape[0]
    first = (pl.program_id(0) % tiles_per_seq) == 0
    ext = tm + HALO
    zext_ref[0, 0:HALO, :] = jnp.where(first, 0.0, halo_ref[...].astype(F32))
    zext_ref[0, HALO:, :] = z_ref[...].astype(F32)
    for s in range(1, SUBLANES):
        zext_ref[s, 0:ext - SUBLANES, :] = zext_ref[0, s:s + ext - SUBLANES, :]

    rows = 64
    off = HALO - (CONV_WIDTH - 1)
    for c in range(CONV_CH // 128):
        cs = slice(c * 128, (c + 1) * 128)
        for r in range(0, tm, rows):
            acc = jnp.broadcast_to(cb_ref[:, cs], (rows, 128))
            for w in range(CONV_WIDTH):
                shift, base = (off + w) % SUBLANES, (off + w) // SUBLANES * SUBLANES
                acc = acc + zext_ref[shift, r + base:r + base + rows, cs] * cw_ref[w:w + 1, cs]
            conv_ref[r:r + rows, cs] = acc

    y = conv_ref[...]
    mu = jnp.mean(y, axis=-1, keepdims=True)
    yc = y - mu
    var = jnp.mean(yc * yc, axis=-1, keepdims=True)
    y = yc * lax.rsqrt(var + EPS) * lng_ref[...] + lnb_ref[...]
    y = y * _sigmoid(y)
    conv_out = _dot(y.astype(BF16), wco_ref[...])
    mla_out = _dot(o_ref[...], wmo_ref[...])
    gates = gates_ref[...].astype(F32)
    merged = gates[:, :D_MODEL] * conv_out + gates[:, D_MODEL:] * mla_out
    h1 = x_ref[...] + _dot(merged.astype(BF16), wout_ref[...])

    u = _rms(h1, xg_ref[...]).astype(BF16)
    q = (_dot(u, wxq_ref[...]) * (X_HEAD_DIM ** -0.5)).astype(BF16)
    heads = []
    for h in range(X_HEADS):
        hs = slice(h * X_HEAD_DIM, (h + 1) * X_HEAD_DIM)
        s = _dot(q[:, hs], mkt_ref[0, hs, :])
        p = jnp.exp(s - jnp.max(s, axis=-1, keepdims=True))
        p = p / jnp.sum(p, axis=-1, keepdims=True)
        heads.append(_dot(p.astype(BF16), mv_ref[0, :, hs]))
    xo = jnp.concatenate(heads, axis=1).astype(BF16)
    h_ref[...] = h1 + _dot(xo, wxo_ref[...])


def _mixer(x2, z, gates, o, cw, cb, lng, lnb, wco, wmo, wout, xg, wxq, mkt, mv, wxo, seq):
    t = x2.shape[0]
    tm = TOKEN_TILE
    tiles_per_seq = seq // tm
    halo_per_tile = tm // HALO
    mem_len = mv.shape[1]
    xw = X_HEADS * X_HEAD_DIM
    full = lambda arr: pl.BlockSpec(arr.shape, lambda i: (0,) * arr.ndim)
    row = lambda w: pl.BlockSpec((tm, w), lambda i: (i, 0))
    in_specs = [row(D_MODEL), row(CONV_CH),
                pl.BlockSpec((HALO, CONV_CH), lambda i: (jnp.maximum(i * halo_per_tile - 1, 0), 0)),
                row(2 * D_MODEL), row(MLA_HEADS * V_DIM),
                full(cw), full(cb), full(lng), full(lnb), full(wco), full(wmo), full(wout),
                full(xg), full(wxq),
                pl.BlockSpec((1, xw, mem_len), lambda i: (i // tiles_per_seq, 0, 0)),
                pl.BlockSpec((1, mem_len, xw), lambda i: (i // tiles_per_seq, 0, 0)),
                full(wxo)]
    return pl.pallas_call(
        functools.partial(_mixer_kernel, tiles_per_seq),
        grid=(t // tm,),
        in_specs=in_specs,
        out_specs=row(D_MODEL),
        out_shape=jax.ShapeDtypeStruct((t, D_MODEL), F32),
        scratch_shapes=[pltpu.VMEM((SUBLANES, tm + HALO, CONV_CH), F32), pltpu.VMEM((tm, CONV_CH), F32)],
        compiler_params=pltpu.CompilerParams(dimension_semantics=("parallel",),
                                             vmem_limit_bytes=VMEM_LIMIT),
        name="mixer",
    )(x2, z, z, gates, o, cw, cb, lng, lnb, wco, wmo, wout, xg, wxq, mkt, mv, wxo)


def _mlp_kernel(h_ref, g_ref, w1_ref, w2_ref, fg_ref, y_ref):
    h = h_ref[...]
    u = _rms(h, g_ref[...]).astype(BF16)
    chunk = 1024
    acc = h
    for c in range(0, D_FF, chunk):
        a = jnp.maximum(_dot(u, w1_ref[:, c:c + chunk]), 0.0)
        acc = acc + _dot((a * a).astype(BF16), w2_ref[c:c + chunk, :])
    y_ref[...] = _rms(acc, fg_ref[...])


def _mlp(h, g, w1, w2, fg):
    t = h.shape[0]
    tm = MLP_TILE
    full = lambda arr: pl.BlockSpec(arr.shape, lambda i: (0,) * arr.ndim, pipeline_mode=pl.Buffered(1))
    row = pl.BlockSpec((tm, D_MODEL), lambda i: (i, 0))
    return pl.pallas_call(
        _mlp_kernel,
        grid=(t // tm,),
        in_specs=[row, full(g), full(w1), full(w2), full(fg)],
        out_specs=row,
        out_shape=jax.ShapeDtypeStruct((t, D_MODEL), F32),
        compiler_params=pltpu.CompilerParams(dimension_semantics=("parallel",),
                                             vmem_limit_bytes=VMEM_LIMIT),
        name="mlp",
    )(h, g, w1, w2, fg)


def _rope_placement():
    e = np.zeros((HEAD_PAD, MLA_HEADS * HEAD_PAD), np.float32)
    for h in range(MLA_HEADS):
        for j in range(QK_ROPE):
            e[j, h * HEAD_PAD + QK_NOPE + j] = 1.0
    return jnp.asarray(e, BF16)


def kernel(x, mem, positions, norm_mix_g, w_in, conv_w, conv_b, conv_ln_g, conv_ln_b, w_conv_out, q_norm_g, w_uq, kv_norm_g, w_ukv, w_mla_out, w_out, norm_xattn_g, norm_mem_g, w_xq, w_xkv, w_xo, norm_mlp_g, w_mlp1, w_mlp2, final_norm_g):
    batch, seq, _ = x.shape
    mem_len = mem.shape[1]
    t = batch * seq
    assert w_in.shape[0] == 1, "single-layer block (the final rmsnorm is fused into the MLP kernel)"
    assert seq % KV_CHUNK == 0 and seq % TOKEN_TILE == 0 and t % ROPE_TILE == 0 and t % MLP_TILE == 0

    inv_freq = ROPE_THETA ** (-jnp.arange(ROPE_HALF, dtype=F32) / ROPE_HALF)
    cos_t, sin_t, cs = _rope_tables(positions.reshape(1, t), inv_freq.reshape(ROPE_HALF, 1))
    place = _rope_placement()
    row = lambda v: v.reshape(1, -1)

    h = x.reshape(t, D_MODEL)
    wi = _regroup_w_in(jnp.swapaxes(w_in, 1, 2))
    wuq = w_uq[0].astype(BF16).reshape(Q_LORA, MLA_HEADS, QK_NOPE + QK_ROPE)
    wqt = jnp.pad(wuq, ((0, 0), (0, 0), (0, HEAD_PAD - QK_NOPE - QK_ROPE))).reshape(Q_LORA, -1).T
    wukv = w_ukv[0].astype(BF16).reshape(KV_LORA, MLA_HEADS, QK_NOPE + V_DIM)
    wk = jnp.pad(wukv[:, :, :QK_NOPE], ((0, 0), (0, 0), (0, HEAD_PAD - QK_NOPE))).reshape(KV_LORA, -1)
    wvt = jnp.pad(wukv[:, :, QK_NOPE:], ((0, 0), (0, 0), (0, V_ROWS - V_DIM))).reshape(KV_LORA, -1).T
    ones = np.zeros((MLA_HEADS, V_ROWS, 1), np.float32)
    ones[:, V_DIM] = 1.0
    ones = jnp.asarray(ones.reshape(MLA_HEADS * V_ROWS, 1))

    z, gates, qt, kp, vt3 = _in_proj(
        h, row(norm_mix_g[0]), wi, row(q_norm_g[0]), row(kv_norm_g[0]), wqt, wk, wvt, ones, place,
        cos_t, sin_t, cs)

    k3 = kp.reshape(t // KV_CHUNK, KV_CHUNK, MLA_HEADS * HEAD_PAD)
    o = _attention(qt, k3, vt3, batch, seq)

    xw = X_HEADS * X_HEAD_DIM
    wxkv = w_xkv[0].astype(BF16)
    mkt, mv = _mem_kv(mem.reshape(batch * mem_len, D_MODEL), row(norm_mem_g[0]),
                      wxkv[:, :xw].T, wxkv[:, xw:], batch, mem_len)

    cw = jnp.pad(conv_w[0], ((0, HALO - CONV_WIDTH), (0, 0)))
    h = _mixer(h, z, gates, o, cw, row(conv_b[0]), row(conv_ln_g[0]), row(conv_ln_b[0]),
               w_conv_out[0].astype(BF16), w_mla_out[0].astype(BF16), w_out[0].astype(BF16),
               row(norm_xattn_g[0]), w_xq[0].astype(BF16), mkt, mv, w_xo[0].astype(BF16), seq)
    h = _mlp(h, row(norm_mlp_g[0]), w_mlp1[0].astype(BF16), w_mlp2[0].astype(BF16),
             row(final_norm_g))
    return h.reshape(batch, seq, D_MODEL)
```

```python
import functools

import numpy as np
import jax
import jax.numpy as jnp
from jax import lax
from jax.experimental import pallas as pl
from jax.experimental.pallas import tpu as pltpu

D_MODEL = 1024
CONV_CH = D_MODEL // 2
CONV_WIDTH = 31
MLA_HEADS = 8
QK_NOPE = D_MODEL // 16
QK_ROPE = D_MODEL // 32
V_DIM = D_MODEL // 16
Q_LORA = 3 * D_MODEL // 8
KV_LORA = D_MODEL // 4
X_HEADS = 4
X_HEAD_DIM = D_MODEL // 8
D_FF = 4 * D_MODEL
ROPE_THETA = 10000.0
EPS = 1e-6

HEAD_PAD = 128
V_ROWS = V_DIM + 16
ATTN_HEADS = 4
ROPE_HALF = QK_ROPE // 2
SUBLANES = 8
BF16_ROWS = 2 * SUBLANES
HALO = 32
TOKEN_TILE = 512
MLP_TILE = 1024
Q_TILE = 512
KV_CHUNK = 512
ROPE_TILE = 2048
LOG2E = 1.4426950408889634


def _column_ranges(widths):
    out, start = {}, 0
    for name, width in widths:
        out[name] = (start, start + width)
        start += width
    return out


IN_COLS = _column_ranges([("glu_a", CONV_CH), ("glu_g", CONV_CH), ("gates", 2 * D_MODEL),
                          ("c_q", Q_LORA), ("c_kv", KV_LORA), ("k_rope", HEAD_PAD)])
MASK_VALUE = -1e30
VMEM_LIMIT = 56 * 1024 * 1024

F32 = jnp.float32
BF16 = jnp.bfloat16


def _rms(x, g):
    return x * lax.rsqrt(jnp.mean(x * x, axis=-1, keepdims=True) + EPS) * g


def _sigmoid(x):
    return 1.0 / (1.0 + jnp.exp(-x))


def _dot(a, b):
    return jnp.dot(a, b, preferred_element_type=F32)


def _dot_nt(a, b):
    return lax.dot_general(a, b, (((1,), (1,)), ((), ())), preferred_element_type=F32)


def _rope_kernel(pos_ref, invf_ref, cos_ref, sin_ref, cs_ref):
    ang = pos_ref[...].astype(F32) * invf_ref[...]
    c = jnp.cos(ang)
    s = jnp.sin(ang)
    cos_ref[...] = c
    sin_ref[...] = s
    pad = jnp.zeros((HEAD_PAD - 2 * QK_ROPE, ang.shape[1]), F32)
    cs_ref[...] = jnp.concatenate([c, c, -s, s, pad], axis=0).T


def _rope_tables(pos_row, inv_freq_col):
    t = pos_row.shape[1]
    out = jax.ShapeDtypeStruct((ROPE_HALF, t), F32)
    spec = pl.BlockSpec((ROPE_HALF, ROPE_TILE), lambda i: (0, i))
    return pl.pallas_call(
        _rope_kernel,
        grid=(t // ROPE_TILE,),
        in_specs=[pl.BlockSpec((1, ROPE_TILE), lambda i: (0, i)),
                  pl.BlockSpec((ROPE_HALF, 1), lambda i: (0, 0))],
        out_specs=[spec, spec, pl.BlockSpec((ROPE_TILE, HEAD_PAD), lambda i: (i, 0))],
        out_shape=[out, out, jax.ShapeDtypeStruct((t, HEAD_PAD), F32)],
        name="rope_tables",
    )(pos_row, inv_freq_col)


def _regroup_kernel(w_ref, o_ref):
    c0 = 2 * CONV_CH
    c2 = c0 + Q_LORA + KV_LORA
    c3 = c2 + QK_ROPE

    def put(name, value):
        o_ref[IN_COLS[name][0]:IN_COLS[name][0] + value.shape[0], :] = value.astype(BF16)

    put("glu_a", w_ref[0, :CONV_CH, :])
    put("glu_g", w_ref[0, CONV_CH:c0, :])
    put("gates", w_ref[0, c3:, :])
    put("c_q", w_ref[0, c0:c0 + Q_LORA, :])
    put("c_kv", w_ref[0, c0 + Q_LORA:c2, :])
    kr = w_ref[0, c2:c3, :]
    zeros = jnp.zeros((HEAD_PAD - 2 * QK_ROPE, kr.shape[1]), F32)
    put("k_rope", jnp.concatenate([kr, kr[ROPE_HALF:], kr[:ROPE_HALF], zeros], axis=0))


def _regroup_w_in(wt):
    lanes = 256
    rows = IN_COLS["k_rope"][1]
    return pl.pallas_call(
        _regroup_kernel,
        grid=(wt.shape[2] // lanes,),
        in_specs=[pl.BlockSpec((1, wt.shape[1], lanes), lambda i: (0, 0, i))],
        out_specs=pl.BlockSpec((rows, lanes), lambda i: (0, i)),
        out_shape=jax.ShapeDtypeStruct((rows, wt.shape[2]), BF16),
        name="regroup_w_in",
    )(wt)


def _in_proj_kernel(x_ref, g_ref, w_ref, qg_ref, kvg_ref, wqt_ref, wk_ref, wvt_ref, ones_ref, place_ref,
                    cos_ref, sin_ref, cs_ref,
                    z_ref, gates_ref, qt_ref, k_ref, vt_ref):
    u = _rms(x_ref[...], g_ref[...]).astype(BF16)
    proj = lambda name: _dot_nt(u, w_ref[IN_COLS[name][0]:IN_COLS[name][1], :])
    z_ref[...] = (proj("glu_a") * _sigmoid(proj("glu_g"))).astype(BF16)
    gates_ref[...] = _sigmoid(proj("gates")).astype(BF16)

    cqn = _rms(proj("c_q"), qg_ref[...]).astype(BF16)
    qt = _dot_nt(wqt_ref[...], cqn)
    scale = (QK_NOPE + QK_ROPE) ** -0.5 * LOG2E
    c = cos_ref[...]
    s = sin_ref[...]
    for h in range(MLA_HEADS):
        b = h * HEAD_PAD
        r1 = b + QK_NOPE
        r2 = r1 + ROPE_HALF
        r3 = r2 + ROPE_HALF
        t1 = qt[r1:r2]
        t2 = qt[r2:r3]
        qt_ref[b:r1, :] = (qt[b:r1] * scale).astype(BF16)
        qt_ref[r1:r2, :] = ((t1 * c - t2 * s) * scale).astype(BF16)
        qt_ref[r2:r3, :] = ((t2 * c + t1 * s) * scale).astype(BF16)
        qt_ref[r3:b + HEAD_PAD, :] = (qt[r3:b + HEAD_PAD] * scale).astype(BF16)

    ckvn = _rms(proj("c_kv"), kvg_ref[...]).astype(BF16)
    t = proj("k_rope") * cs_ref[...]
    rot = t + pltpu.roll(t, HEAD_PAD - QK_ROPE, 1)
    k_ref[...] = (_dot(ckvn, wk_ref[...]) + _dot(rot.astype(BF16), place_ref[...])).astype(BF16)
    vt_ref[0] = (_dot_nt(wvt_ref[...], ckvn) + ones_ref[...]).astype(BF16)


def _in_proj(x2, g, w, qg, kvg, wqt, wk, wvt, ones, place, cos_t, sin_t, cs):
    t = x2.shape[0]
    tm = KV_CHUNK
    n = t // tm
    full = lambda arr: pl.BlockSpec(arr.shape, lambda i: (0,) * arr.ndim)
    row = lambda w: pl.BlockSpec((tm, w), lambda i: (i, 0))
    in_specs = [row(D_MODEL), full(g), full(w), full(qg), full(kvg), full(wqt), full(wk), full(wvt),
                full(ones), full(place),
                pl.BlockSpec((ROPE_HALF, tm), lambda i: (0, i)),
                pl.BlockSpec((ROPE_HALF, tm), lambda i: (0, i)),
                row(HEAD_PAD)]
    out_shape = [jax.ShapeDtypeStruct((t, CONV_CH), BF16),
                 jax.ShapeDtypeStruct((t, 2 * D_MODEL), BF16),
                 jax.ShapeDtypeStruct((MLA_HEADS * HEAD_PAD, t), BF16),
                 jax.ShapeDtypeStruct((t, MLA_HEADS * HEAD_PAD), BF16),
                 jax.ShapeDtypeStruct((n, MLA_HEADS * V_ROWS, tm), BF16)]
    out_specs = [row(CONV_CH), row(2 * D_MODEL),
                 pl.BlockSpec((MLA_HEADS * HEAD_PAD, tm), lambda i: (0, i)),
                 row(MLA_HEADS * HEAD_PAD),
                 pl.BlockSpec((1, MLA_HEADS * V_ROWS, tm), lambda i: (i, 0, 0))]
    return pl.pallas_call(
        _in_proj_kernel,
        grid=(n,),
        in_specs=in_specs,
        out_specs=out_specs,
        out_shape=out_shape,
        compiler_params=pltpu.CompilerParams(dimension_semantics=("parallel",),
                                             vmem_limit_bytes=VMEM_LIMIT),
        name="in_proj",
    )(x2, g, w, qg, kvg, wqt, wk, wvt, ones, place, cos_t, sin_t, cs)


def _attn_kernel(qt_ref, k_ref, vt_ref, o_ref, s_scr, p_scr, acc_scr):
    qi = pl.program_id(2)
    tk = k_ref.shape[1]
    tq = qt_ref.shape[1]
    heads = range(ATTN_HEADS)
    chunks_per_q = tk // tq
    diag = qi // chunks_per_q
    key_idx = lax.broadcasted_iota(jnp.int32, (tk, tq), 0)
    qry_idx = lax.broadcasted_iota(jnp.int32, (tk, tq), 1) + (qi % chunks_per_q) * tq
    causal = key_idx <= qry_idx
    qts = [qt_ref[h * HEAD_PAD:(h + 1) * HEAD_PAD, :] for h in heads]

    def scores_to(slot, kc, masked, hs=heads):
        ms = []
        for h in hs:
            s = _dot(k_ref[kc, :, h * HEAD_PAD:(h + 1) * HEAD_PAD], qts[h])
            if masked:
                s = jnp.where(causal, s, MASK_VALUE)
            s_scr[slot, h] = s
            ms.append(jnp.max(s, axis=0, keepdims=True))
        return tuple(ms)

    def softmax_to(src, dst, ms, hs=heads):
        for h, m in zip(hs, ms):
            for r in range(0, tk, BF16_ROWS):
                p_scr[dst, h, r:r + BF16_ROWS, :] = jnp.exp2(s_scr[src, h, r:r + BF16_ROWS, :] - m).astype(BF16)

    def accumulate(m_run, slot, kc, ms, hs=heads):
        new = []
        for h, m_old, m_chunk in zip(hs, m_run, ms):
            pv = _dot(vt_ref[kc, h * V_ROWS:(h + 1) * V_ROWS, :], p_scr[slot, h])
            m_new = jnp.maximum(m_old, m_chunk)
            a = jnp.exp2(m_old - m_new)
            b = jnp.exp2(m_chunk - m_new)
            acc_scr[h] = a * acc_scr[h] + b * pv
            new.append(m_new)
        return tuple(new)

    def step(cur, look_ahead, j, carry):
        m_run, pend_kc, pend_m, next_m = carry
        new_run, ahead_m = [], []
        for h in heads:
            new_run += accumulate(m_run[h:h + 1], 1 - cur, pend_kc, pend_m[h:h + 1], (h,))
            softmax_to(1 - cur, cur, next_m[h:h + 1], (h,))
            if look_ahead:
                ahead_m += scores_to(cur, j + 1, False, (h,))
        return tuple(new_run), j, next_m, (tuple(ahead_m) if look_ahead else next_m)

    branches = [functools.partial(step, cur, look_ahead) for look_ahead in (True, False) for cur in (0, 1)]

    def body(j, carry):
        return lax.switch(2 * (j == diag - 1).astype(jnp.int32) + j % 2, branches, j, carry)

    acc_scr[...] = jnp.zeros(acc_scr.shape, F32)
    m_init = tuple(jnp.full((1, tq), MASK_VALUE, F32) for _ in heads)
    diag_m, first_m = [], []
    for h in heads:
        diag_m += scores_to(0, diag, True, (h,))
        first_m += scores_to(1, 0, False, (h,))
        softmax_to(0, 1, diag_m[h:h + 1], (h,))
    carry = (m_init, diag, tuple(diag_m), tuple(first_m))
    m_run, pend_kc, pend_m, _ = lax.fori_loop(0, diag, body, carry)
    accumulate(m_run, (diag + 1) % 2, pend_kc, pend_m)
    outs = [acc_scr[h, :V_DIM, :] / acc_scr[h, V_DIM:V_DIM + 1, :] for h in heads]
    o_ref[...] = jnp.concatenate(outs, axis=0).T.astype(BF16)


def _attention(qt, k3, vt3, batch, seq):
    tq = Q_TILE
    nq = seq // tq
    nk = seq // KV_CHUNK
    t = batch * seq
    return pl.pallas_call(
        _attn_kernel,
        grid=(batch, MLA_HEADS // ATTN_HEADS, nq),
        in_specs=[pl.BlockSpec((ATTN_HEADS * HEAD_PAD, tq), lambda b, hg, qi: (hg, b * nq + qi)),
                  pl.BlockSpec((nk, KV_CHUNK, ATTN_HEADS * HEAD_PAD), lambda b, hg, qi: (b, 0, hg)),
                  pl.BlockSpec((nk, ATTN_HEADS * V_ROWS, KV_CHUNK), lambda b, hg, qi: (b, hg, 0))],
        out_specs=pl.BlockSpec((tq, ATTN_HEADS * V_DIM), lambda b, hg, qi: (b * nq + qi, hg)),
        out_shape=jax.ShapeDtypeStruct((t, MLA_HEADS * V_DIM), BF16),
        scratch_shapes=[pltpu.VMEM((2, ATTN_HEADS, KV_CHUNK, tq), F32),
                        pltpu.VMEM((2, ATTN_HEADS, KV_CHUNK, tq), BF16),
                        pltpu.VMEM((ATTN_HEADS, V_ROWS, tq), F32)],
        compiler_params=pltpu.CompilerParams(
            dimension_semantics=("parallel", "parallel", "arbitrary"),
            vmem_limit_bytes=VMEM_LIMIT),
        name="mla_attention",
    )(qt, k3, vt3)


def _mem_kernel(mem_ref, g_ref, wkt_ref, wv_ref, kt_ref, v_ref):
    mn = _rms(mem_ref[...], g_ref[...]).astype(BF16)
    kt_ref[0] = _dot_nt(wkt_ref[...], mn).astype(BF16)
    v_ref[0] = _dot(mn, wv_ref[...]).astype(BF16)


def _mem_kv(mem2, g, wkt, wv, batch, mem_len):
    xw = X_HEADS * X_HEAD_DIM
    full = lambda arr: pl.BlockSpec(arr.shape, lambda b: (0,) * arr.ndim)
    return pl.pallas_call(
        _mem_kernel,
        grid=(batch,),
        in_specs=[pl.BlockSpec((mem_len, D_MODEL), lambda b: (b, 0)), full(g), full(wkt), full(wv)],
        out_specs=[pl.BlockSpec((1, xw, mem_len), lambda b: (b, 0, 0)),
                   pl.BlockSpec((1, mem_len, xw), lambda b: (b, 0, 0))],
        out_shape=[jax.ShapeDtypeStruct((batch, xw, mem_len), BF16),
                   jax.ShapeDtypeStruct((batch, mem_len, xw), BF16)],
        compiler_params=pltpu.CompilerParams(dimension_semantics=("parallel",),
                                             vmem_limit_bytes=VMEM_LIMIT),
        name="mem_kv",
    )(mem2, g, wkt, wv)


def _mixer_kernel(tiles_per_seq, x_ref, z_ref, halo_ref, gates_ref, o_ref,
                  cw_ref, cb_ref, lng_ref, lnb_ref, wco_ref, wmo_ref, wout_ref,
                  xg_ref, wxq_ref, mkt_ref, mv_ref, wxo_ref,
                  h_ref, zext_ref, conv_ref):
    tm = x_ref.shape[0]
    first = (pl.program_id(0) % tiles_per_seq) == 0
    ext = tm + HALO
    zext_ref[0, 0:HALO, :] = jnp.where(first, 0.0, halo_ref[...].astype(F32))
    zext_ref[0, HALO:, :] = z_ref[...].astype(F32)
    for s in range(1, SUBLANES):
        zext_ref[s, 0:ext - SUBLANES, :] = zext_ref[0, s:s + ext - SUBLANES, :]

    rows = 64
    off = HALO - (CONV_WIDTH - 1)
    for c in range(CONV_CH // 128):
        cs = slice(c * 128, (c + 1) * 128)
        for r in range(0, tm, rows):
            acc = jnp.broadcast_to(cb_ref[:, cs], (rows, 128))
            for w in range(CONV_WIDTH):
                shift, base = (off + w) % SUBLANES, (off + w) // SUBLANES * SUBLANES
                acc = acc + zext_ref[shift, r + base:r + base + rows, cs] * cw_ref[w:w + 1, cs]
            conv_ref[r:r + rows, cs] = acc

    y = conv_ref[...]
    mu = jnp.mean(y, axis=-1, keepdims=True)
    yc = y - mu
    var = jnp.mean(yc * yc, axis=-1, keepdims=True)
    y = yc * lax.rsqrt(var + EPS) * lng_ref[...] + lnb_ref[...]
    y = y * _sigmoid(y)
    conv_out = _dot(y.astype(BF16), wco_ref[...])
    mla_out = _dot(o_ref[...], wmo_ref[...])
    gates = gates_ref[...].astype(F32)
    merged = gates[:, :D_MODEL] * conv_out + gates[:, D_MODEL:] * mla_out
    h1 = x_ref[...] + _dot(merged.astype(BF16), wout_ref[...])

    u = _rms(h1, xg_ref[...]).astype(BF16)
    q = (_dot(u, wxq_ref[...]) * (X_HEAD_DIM ** -0.5)).astype(BF16)
    heads = []
    for h in range(X_HEADS):
        hs = slice(h * X_HEAD_DIM, (h + 1) * X_HEAD_DIM)
        s = _dot(q[:, hs], mkt_ref[0, hs, :])
        p = jnp.exp(s - jnp.max(s, axis=-1, keepdims=True))
        p = p / jnp.sum(p, axis=-1, keepdims=True)
        heads.append(_dot(p.astype(BF16), mv_ref[0, :, hs]))
    xo = jnp.concatenate(heads, axis=1).astype(BF16)
    h_ref[...] = h1 + _dot(xo, wxo_ref[...])


def _mixer(x2, z, gates, o, cw, cb, lng, lnb, wco, wmo, wout, xg, wxq, mkt, mv, wxo, seq):
    t = x2.shape[0]
    tm = TOKEN_TILE
    tiles_per_seq = seq // tm
    halo_per_tile = tm // HALO
    mem_len = mv.shape[1]
    xw = X_HEADS * X_HEAD_DIM
    full = lambda arr: pl.BlockSpec(arr.shape, lambda i: (0,) * arr.ndim)
    row = lambda w: pl.BlockSpec((tm, w), lambda i: (i, 0))
    in_specs = [row(D_MODEL), row(CONV_CH),
                pl.BlockSpec((HALO, CONV_CH), lambda i: (jnp.maximum(i * halo_per_tile - 1, 0), 0)),
                row(2 * D_MODEL), row(MLA_HEADS * V_DIM),
                full(cw), full(cb), full(lng), full(lnb), full(wco), full(wmo), full(wout),
                full(xg), full(wxq),
                pl.BlockSpec((1, xw, mem_len), lambda i: (i // tiles_per_seq, 0, 0)),
                pl.BlockSpec((1, mem_len, xw), lambda i: (i // tiles_per_seq, 0, 0)),
                full(wxo)]
    return pl.pallas_call(
        functools.partial(_mixer_kernel, tiles_per_seq),
        grid=(t // tm,),
        in_specs=in_specs,
        out_specs=row(D_MODEL),
        out_shape=jax.ShapeDtypeStruct((t, D_MODEL), F32),
        scratch_shapes=[pltpu.VMEM((SUBLANES, tm + HALO, CONV_CH), F32), pltpu.VMEM((tm, CONV_CH), F32)],
        compiler_params=pltpu.CompilerParams(dimension_semantics=("parallel",),
                                             vmem_limit_bytes=VMEM_LIMIT),
        name="mixer",
    )(x2, z, z, gates, o, cw, cb, lng, lnb, wco, wmo, wout, xg, wxq, mkt, mv, wxo)


def _mlp_kernel(h_ref, g_ref, w1_ref, w2_ref, fg_ref, y_ref):
    h = h_ref[...]
    u = _rms(h, g_ref[...]).astype(BF16)
    chunk = 1024
    acc = h
    for c in range(0, D_FF, chunk):
        a = jnp.maximum(_dot(u, w1_ref[:, c:c + chunk]), 0.0)
        acc = acc + _dot((a * a).astype(BF16), w2_ref[c:c + chunk, :])
    y_ref[...] = _rms(acc, fg_ref[...])


def _mlp(h, g, w1, w2, fg):
    t = h.shape[0]
    tm = MLP_TILE
    full = lambda arr: pl.BlockSpec(arr.shape, lambda i: (0,) * arr.ndim, pipeline_mode=pl.Buffered(1))
    row = pl.BlockSpec((tm, D_MODEL), lambda i: (i, 0))
    return pl.pallas_call(
        _mlp_kernel,
        grid=(t // tm,),
        in_specs=[row, full(g), full(w1), full(w2), full(fg)],
        out_specs=row,
        out_shape=jax.ShapeDtypeStruct((t, D_MODEL), F32),
        compiler_params=pltpu.CompilerParams(dimension_semantics=("parallel",),
                                             vmem_limit_bytes=VMEM_LIMIT),
        name="mlp",
    )(h, g, w1, w2, fg)


def _rope_placement():
    e = np.zeros((HEAD_PAD, MLA_HEADS * HEAD_PAD), np.float32)
    for h in range(MLA_HEADS):
        for j in range(QK_ROPE):
            e[j, h * HEAD_PAD + QK_NOPE + j] = 1.0
    return jnp.asarray(e, BF16)


def kernel(x, mem, positions, norm_mix_g, w_in, conv_w, conv_b, conv_ln_g, conv_ln_b, w_conv_out, q_norm_g, w_uq, kv_norm_g, w_ukv, w_mla_out, w_out, norm_xattn_g, norm_mem_g, w_xq, w_xkv, w_xo, norm_mlp_g, w_mlp1, w_mlp2, final_norm_g):
    batch, seq, _ = x.shape
    mem_len = mem.shape[1]
    t = batch * seq
    assert w_in.shape[0] == 1, "single-layer block (the final rmsnorm is fused into the MLP kernel)"
    assert seq % KV_CHUNK == 0 and seq % TOKEN_TILE == 0 and t % ROPE_TILE == 0 and t % MLP_TILE == 0

    inv_freq = ROPE_THETA ** (-jnp.arange(ROPE_HALF, dtype=F32) / ROPE_HALF)
    cos_t, sin_t, cs = _rope_tables(positions.reshape(1, t), inv_freq.reshape(ROPE_HALF, 1))
    place = _rope_placement()
    row = lambda v: v.reshape(1, -1)

    h = x.reshape(t, D_MODEL)
    wi = _regroup_w_in(jnp.swapaxes(w_in, 1, 2))
    wuq = w_uq[0].astype(BF16).reshape(Q_LORA, MLA_HEADS, QK_NOPE + QK_ROPE)
    wqt = jnp.pad(wuq, ((0, 0), (0, 0), (0, HEAD_PAD - QK_NOPE - QK_ROPE))).reshape(Q_LORA, -1).T
    wukv = w_ukv[0].astype(BF16).reshape(KV_LORA, MLA_HEADS, QK_NOPE + V_DIM)
    wk = jnp.pad(wukv[:, :, :QK_NOPE], ((0, 0), (0, 0), (0, HEAD_PAD - QK_NOPE))).reshape(KV_LORA, -1)
    wvt = jnp.pad(wukv[:, :, QK_NOPE:], ((0, 0), (0, 0), (0, V_ROWS - V_DIM))).reshape(KV_LORA, -1).T
    ones = np.zeros((MLA_HEADS, V_ROWS, 1), np.float32)
    ones[:, V_DIM] = 1.0
    ones = jnp.asarray(ones.reshape(MLA_HEADS * V_ROWS, 1))

    z, gates, qt, kp, vt3 = _in_proj(
        h, row(norm_mix_g[0]), wi, row(q_norm_g[0]), row(kv_norm_g[0]), wqt, wk, wvt, ones, place,
        cos_t, sin_t, cs)

    k3 = kp.reshape(t // KV_CHUNK, KV_CHUNK, MLA_HEADS * HEAD_PAD)
    o = _attention(qt, k3, vt3, batch, seq)

    xw = X_HEADS * X_HEAD_DIM
    wxkv = w_xkv[0].astype(BF16)
    mkt, mv = _mem_kv(mem.reshape(batch * mem_len, D_MODEL), row(norm_mem_g[0]),
                      wxkv[:, :xw].T, wxkv[:, xw:], batch, mem_len)

    cw = jnp.pad(conv_w[0], ((0, HALO - CONV_WIDTH), (0, 0)))
    h = _mixer(h, z, gates, o, cw, row(conv_b[0]), row(conv_ln_g[0]), row(conv_ln_b[0]),
               w_conv_out[0].astype(BF16), w_mla_out[0].astype(BF16), w_out[0].astype(BF16),
               row(norm_xattn_g[0]), w_xq[0].astype(BF16), mkt, mv, w_xo[0].astype(BF16), seq)
    h = _mlp(h, row(norm_mlp_g[0]), w_mlp1[0].astype(BF16), w_mlp2[0].astype(BF16),
             row(final_norm_g))
    return h.reshape(batch, seq, D_MODEL)
```

```python
import functools

import numpy as np
import jax
import jax.numpy as jnp
from jax import lax
from jax.experimental import pallas as pl
from jax.experimental.pallas import tpu as pltpu

D_MODEL = 1024
CONV_CH = D_MODEL // 2
CONV_WIDTH = 31
MLA_HEADS = 8
QK_NOPE = D_MODEL // 16
QK_ROPE = D_MODEL // 32
V_DIM = D_MODEL // 16
Q_LORA = 3 * D_MODEL // 8
KV_LORA = D_MODEL // 4
X_HEADS = 4
X_HEAD_DIM = D_MODEL // 8
D_FF = 4 * D_MODEL
ROPE_THETA = 10000.0
EPS = 1e-6

HEAD_PAD = 128
V_ROWS = V_DIM + 16
ATTN_HEADS = 4
ROPE_HALF = QK_ROPE // 2
SUBLANES = 8
BF16_ROWS = 2 * SUBLANES
HALO = 32
TOKEN_TILE = 512
MLP_TILE = 1024
Q_TILE = 512
KV_CHUNK = 512
ROPE_TILE = 2048
LOG2E = 1.4426950408889634


def _column_ranges(widths):
    out, start = {}, 0
    for name, width in widths:
        out[name] = (start, start + width)
        start += width
    return out


IN_COLS = _column_ranges([("glu_a", CONV_CH), ("glu_g", CONV_CH), ("gates", 2 * D_MODEL),
                          ("c_q", Q_LORA), ("c_kv", KV_LORA), ("k_rope", HEAD_PAD)])
MASK_VALUE = -1e30
VMEM_LIMIT = 56 * 1024 * 1024

F32 = jnp.float32
BF16 = jnp.bfloat16


def _rms(x, g):
    return x * lax.rsqrt(jnp.mean(x * x, axis=-1, keepdims=True) + EPS) * g


def _sigmoid(x):
    return 1.0 / (1.0 + jnp.exp(-x))


def _dot(a, b):
    return jnp.dot(a, b, preferred_element_type=F32)


def _dot_nt(a, b):
    return lax.dot_general(a, b, (((1,), (1,)), ((), ())), preferred_element_type=F32)


def _rope_kernel(pos_ref, invf_ref, cos_ref, sin_ref, cs_ref):
    ang = pos_ref[...].astype(F32) * invf_ref[...]
    c = jnp.cos(ang)
    s = jnp.sin(ang)
    cos_ref[...] = c
    sin_ref[...] = s
    pad = jnp.zeros((HEAD_PAD - 2 * QK_ROPE, ang.shape[1]), F32)
    cs_ref[...] = jnp.concatenate([c, c, -s, s, pad], axis=0).T


def _rope_tables(pos_row, inv_freq_col):
    t = pos_row.shape[1]
    out = jax.ShapeDtypeStruct((ROPE_HALF, t), F32)
    spec = pl.BlockSpec((ROPE_HALF, ROPE_TILE), lambda i: (0, i))
    return pl.pallas_call(
        _rope_kernel,
        grid=(t // ROPE_TILE,),
        in_specs=[pl.BlockSpec((1, ROPE_TILE), lambda i: (0, i)),
                  pl.BlockSpec((ROPE_HALF, 1), lambda i: (0, 0))],
        out_specs=[spec, spec, pl.BlockSpec((ROPE_TILE, HEAD_PAD), lambda i: (i, 0))],
        out_shape=[out, out, jax.ShapeDtypeStruct((t, HEAD_PAD), F32)],
        name="rope_tables",
    )(pos_row, inv_freq_col)


def _regroup_kernel(w_ref, o_ref):
    c0 = 2 * CONV_CH
    c2 = c0 + Q_LORA + KV_LORA
    c3 = c2 + QK_ROPE

    def put(name, value):
        o_ref[IN_COLS[name][0]:IN_COLS[name][0] + value.shape[0], :] = value.astype(BF16)

    put("glu_a", w_ref[0, :CONV_CH, :])
    put("glu_g", w_ref[0, CONV_CH:c0, :])
    put("gates", w_ref[0, c3:, :])
    put("c_q", w_ref[0, c0:c0 + Q_LORA, :])
    put("c_kv", w_ref[0, c0 + Q_LORA:c2, :])
    kr = w_ref[0, c2:c3, :]
    zeros = jnp.zeros((HEAD_PAD - 2 * QK_ROPE, kr.shape[1]), F32)
    put("k_rope", jnp.concatenate([kr, kr[ROPE_HALF:], kr[:ROPE_HALF], zeros], axis=0))


def _regroup_w_in(wt):
    lanes = 256
    rows = IN_COLS["k_rope"][1]
    return pl.pallas_call(
        _regroup_kernel,
        grid=(wt.shape[2] // lanes,),
        in_specs=[pl.BlockSpec((1, wt.shape[1], lanes), lambda i: (0, 0, i))],
        out_specs=pl.BlockSpec((rows, lanes), lambda i: (0, i)),
        out_shape=jax.ShapeDtypeStruct((rows, wt.shape[2]), BF16),
        name="regroup_w_in",
    )(wt)


def _in_proj_kernel(x_ref, g_ref, w_ref, qg_ref, kvg_ref, wqt_ref, wk_ref, wvt_ref, ones_ref, place_ref,
                    cos_ref, sin_ref, cs_ref,
                    z_ref, gates_ref, qt_ref, k_ref, vt_ref):
    u = _rms(x_ref[...], g_ref[...]).astype(BF16)
    proj = lambda name: _dot_nt(u, w_ref[IN_COLS[name][0]:IN_COLS[name][1], :])
    z_ref[...] = (proj("glu_a") * _sigmoid(proj("glu_g"))).astype(BF16)
    gates_ref[...] = _sigmoid(proj("gates")).astype(BF16)

    cqn = _rms(proj("c_q"), qg_ref[...]).astype(BF16)
    qt = _dot_nt(wqt_ref[...], cqn)
    scale = (QK_NOPE + QK_ROPE) ** -0.5 * LOG2E
    c = cos_ref[...]
    s = sin_ref[...]
    for h in range(MLA_HEADS):
        b = h * HEAD_PAD
        r1 = b + QK_NOPE
        r2 = r1 + ROPE_HALF
        r3 = r2 + ROPE_HALF
        t1 = qt[r1:r2]
        t2 = qt[r2:r3]
        qt_ref[b:r1, :] = (qt[b:r1] * scale).astype(BF16)
        qt_ref[r1:r2, :] = ((t1 * c - t2 * s) * scale).astype(BF16)
        qt_ref[r2:r3, :] = ((t2 * c + t1 * s) * scale).astype(BF16)
        qt_ref[r3:b + HEAD_PAD, :] = (qt[r3:b + HEAD_PAD] * scale).astype(BF16)

    ckvn = _rms(proj("c_kv"), kvg_ref[...]).astype(BF16)
    t = proj("k_rope") * cs_ref[...]
    rot = t + pltpu.roll(t, HEAD_PAD - QK_ROPE, 1)
    k_ref[...] = (_dot(ckvn, wk_ref[...]) + _dot(rot.astype(BF16), place_ref[...])).astype(BF16)
    vt_ref[0] = (_dot_nt(wvt_ref[...], ckvn) + ones_ref[...]).astype(BF16)


def _in_proj(x2, g, w, qg, kvg, wqt, wk, wvt, ones, place, cos_t, sin_t, cs):
    t = x2.shape[0]
    tm = KV_CHUNK
    n = t // tm
    full = lambda arr: pl.BlockSpec(arr.shape, lambda i: (0,) * arr.ndim)
    row = lambda w: pl.BlockSpec((tm, w), lambda i: (i, 0))
    in_specs = [row(D_MODEL), full(g), full(w), full(qg), full(kvg), full(wqt), full(wk), full(wvt),
                full(ones), full(place),
                pl.BlockSpec((ROPE_HALF, tm), lambda i: (0, i)),
                pl.BlockSpec((ROPE_HALF, tm), lambda i: (0, i)),
                row(HEAD_PAD)]
    out_shape = [jax.ShapeDtypeStruct((t, CONV_CH), BF16),
                 jax.ShapeDtypeStruct((t, 2 * D_MODEL), BF16),
                 jax.ShapeDtypeStruct((MLA_HEADS * HEAD_PAD, t), BF16),
                 jax.ShapeDtypeStruct((t, MLA_HEADS * HEAD_PAD), BF16),
                 jax.ShapeDtypeStruct((n, MLA_HEADS * V_ROWS, tm), BF16)]
    out_specs = [row(CONV_CH), row(2 * D_MODEL),
                 pl.BlockSpec((MLA_HEADS * HEAD_PAD, tm), lambda i: (0, i)),
                 row(MLA_HEADS * HEAD_PAD),
                 pl.BlockSpec((1, MLA_HEADS * V_ROWS, tm), lambda i: (i, 0, 0))]
    return pl.pallas_call(
        _in_proj_kernel,
        grid=(n,),
        in_specs=in_specs,
        out_specs=out_specs,
        out_shape=out_shape,
        compiler_params=pltpu.CompilerParams(dimension_semantics=("parallel",),
                                             vmem_limit_bytes=VMEM_LIMIT),
        name="in_proj",
    )(x2, g, w, qg, kvg, wqt, wk, wvt, ones, place, cos_t, sin_t, cs)


def _attn_kernel(qt_ref, k_ref, vt_ref, o_ref, s_scr, p_scr, acc_scr):
    qi = pl.program_id(2)
    tk = k_ref.shape[1]
    tq = qt_ref.shape[1]
    heads = range(ATTN_HEADS)
    n_diag = tq // tk
    assert tq == n_diag * tk and n_diag in (1, 2)
    diag = qi * n_diag
    n_items = n_diag + diag
    key_idx = lax.broadcasted_iota(jnp.int32, (tk, tq), 0)
    qry_idx = lax.broadcasted_iota(jnp.int32, (tk, tq), 1)
    qts = [qt_ref[h * HEAD_PAD:(h + 1) * HEAD_PAD, :] for h in heads]

    def scores_to(slot, kc, diag_part, hs=heads):
        ms = []
        for h in hs:
            s = _dot(k_ref[kc, :, h * HEAD_PAD:(h + 1) * HEAD_PAD], qts[h])
            if diag_part is not None:
                s = jnp.where(key_idx + diag_part * tk <= qry_idx, s, MASK_VALUE)
            s_scr[slot, h, :, :tq] = s
            ms.append(jnp.max(s, axis=0, keepdims=True))
        return tuple(ms)

    def softmax_to(src, dst, ms, hs=heads):
        for h, m in zip(hs, ms):
            for r in range(0, tk, BF16_ROWS):
                p_scr[dst, h, r:r + BF16_ROWS, :tq] = jnp.exp2(s_scr[src, h, r:r + BF16_ROWS, :tq] - m).astype(BF16)

    def accumulate(m_run, slot, kc, ms, hs=heads):
        new = []
        for h, m_old, m_chunk in zip(hs, m_run, ms):
            pv = _dot(vt_ref[kc, h * V_ROWS:(h + 1) * V_ROWS, :], p_scr[slot, h, :, :tq])
            m_new = jnp.maximum(m_old, m_chunk)
            a = jnp.exp2(m_old - m_new)
            b = jnp.exp2(m_chunk - m_new)
            acc_scr[h] = a * acc_scr[h] + b * pv
            new.append(m_new)
        return tuple(new)

    def step(cur, look_ahead, j, carry):
        m_run, pend_kc, pend_m, next_kc, next_m = carry
        ahead_kc = j + 2 - n_diag
        new_run, ahead_m = [], []
        for h in heads:
            new_run += accumulate(m_run[h:h + 1], 1 - cur, pend_kc, pend_m[h:h + 1], (h,))
            softmax_to(1 - cur, cur, next_m[h:h + 1], (h,))
            if look_ahead:
                ahead_m += scores_to(cur, ahead_kc, None, (h,))
        return tuple(new_run), next_kc, next_m, ahead_kc, (tuple(ahead_m) if look_ahead else next_m)

    branches = [functools.partial(step, cur, look_ahead) for look_ahead in (True, False) for cur in (0, 1)]

    def body(j, carry):
        return lax.switch(2 * (j == n_items - 2).astype(jnp.int32) + j % 2, branches, j, carry)

    acc_scr[...] = jnp.zeros(acc_scr.shape, F32)
    m_init = tuple(jnp.full((1, tq), MASK_VALUE, F32) for _ in heads)
    second_kc, second_part = (diag + 1, 1) if n_diag > 1 else (0, None)
    first_m, second_m = [], []
    for h in heads:
        first_m += scores_to(0, diag, 0, (h,))
        second_m += scores_to(1, second_kc, second_part, (h,))
        softmax_to(0, 1, first_m[h:h + 1], (h,))
    carry = (m_init, diag, tuple(first_m), second_kc, tuple(second_m))
    m_run, pend_kc, pend_m, _, _ = lax.fori_loop(0, n_items - 1, body, carry)
    accumulate(m_run, n_items % 2, pend_kc, pend_m)
    outs = [acc_scr[h, :V_DIM, :] / acc_scr[h, V_DIM:V_DIM + 1, :] for h in heads]
    o_ref[...] = jnp.concatenate(outs, axis=0).T.astype(BF16)


def _attention(qt, k3, vt3, batch, seq):
    tq = Q_TILE
    nq = seq // tq
    nk = seq // KV_CHUNK
    t = batch * seq
    return pl.pallas_call(
        _attn_kernel,
        grid=(batch, MLA_HEADS // ATTN_HEADS, nq),
        in_specs=[pl.BlockSpec((ATTN_HEADS * HEAD_PAD, tq), lambda b, hg, qi: (hg, b * nq + qi)),
                  pl.BlockSpec((nk, KV_CHUNK, ATTN_HEADS * HEAD_PAD), lambda b, hg, qi: (b, 0, hg)),
                  pl.BlockSpec((nk, ATTN_HEADS * V_ROWS, KV_CHUNK), lambda b, hg, qi: (b, hg, 0))],
        out_specs=pl.BlockSpec((tq, ATTN_HEADS * V_DIM), lambda b, hg, qi: (b * nq + qi, hg)),
        out_shape=jax.ShapeDtypeStruct((t, MLA_HEADS * V_DIM), BF16),
        scratch_shapes=[pltpu.VMEM((2, ATTN_HEADS, KV_CHUNK, tq + 128), F32),
                        pltpu.VMEM((2, ATTN_HEADS, KV_CHUNK, tq + 128), BF16),
                        pltpu.VMEM((ATTN_HEADS, V_ROWS, tq), F32)],
        compiler_params=pltpu.CompilerParams(
            dimension_semantics=("parallel", "parallel", "arbitrary"),
            vmem_limit_bytes=VMEM_LIMIT),
        name="mla_attention",
    )(qt, k3, vt3)


def _mem_kernel(mem_ref, g_ref, wkt_ref, wv_ref, kt_ref, v_ref):
    mn = _rms(mem_ref[...], g_ref[...]).astype(BF16)
    kt_ref[0] = _dot_nt(wkt_ref[...], mn).astype(BF16)
    v_ref[0] = _dot(mn, wv_ref[...]).astype(BF16)


def _mem_kv(mem2, g, wkt, wv, batch, mem_len):
    xw = X_HEADS * X_HEAD_DIM
    full = lambda arr: pl.BlockSpec(arr.shape, lambda b: (0,) * arr.ndim)
    return pl.pallas_call(
        _mem_kernel,
        grid=(batch,),
        in_specs=[pl.BlockSpec((mem_len, D_MODEL), lambda b: (b, 0)), full(g), full(wkt), full(wv)],
        out_specs=[pl.BlockSpec((1, xw, mem_len), lambda b: (b, 0, 0)),
                   pl.BlockSpec((1, mem_len, xw), lambda b: (b, 0, 0))],
        out_shape=[jax.ShapeDtypeStruct((batch, xw, mem_len), BF16),
                   jax.ShapeDtypeStruct((batch, mem_len, xw), BF16)],
        compiler_params=pltpu.CompilerParams(dimension_semantics=("parallel",),
                                             vmem_limit_bytes=VMEM_LIMIT),
        name="mem_kv",
    )(mem2, g, wkt, wv)


def _mixer_kernel(tiles_per_seq, x_ref, z_ref, halo_ref, gates_ref, o_ref,
                  cw_ref, cb_ref, lng_ref, lnb_ref, wco_ref, wmo_ref, wout_ref,
                  xg_ref, wxq_ref, mkt_ref, mv_ref, wxo_ref,
                  h_ref, zext_ref, conv_ref):
    tm = x_ref.shape[0]
    first = (pl.program_id(0) % tiles_per_seq) == 0
    ext = tm + HALO
    zext_ref[0, 0:HALO, :] = jnp.where(first, 0.0, halo_ref[...].astype(F32))
    zext_ref[0, HALO:, :] = z_ref[...].astype(F32)
    for s in range(1, SUBLANES):
        zext_ref[s, 0:ext - SUBLANES, :] = zext_ref[0, s:s + ext - SUBLANES, :]

    rows = 64
    off = HALO - (CONV_WIDTH - 1)
    for c in range(CONV_CH // 128):
        cs = slice(c * 128, (c + 1) * 128)
        for r in range(0, tm, rows):
            acc = jnp.broadcast_to(cb_ref[:, cs], (rows, 128))
            for w in range(CONV_WIDTH):
                shift, base = (off + w) % SUBLANES, (off + w) // SUBLANES * SUBLANES
                acc = acc + zext_ref[shift, r + base:r + base + rows, cs] * cw_ref[w:w + 1, cs]
            conv_ref[r:r + rows, cs] = acc

    y = conv_ref[...]
    mu = jnp.mean(y, axis=-1, keepdims=True)
    yc = y - mu
    var = jnp.mean(yc * yc, axis=-1, keepdims=True)
    y = yc * lax.rsqrt(var + EPS) * lng_ref[...] + lnb_ref[...]
    y = y * _sigmoid(y)
    conv_out = _dot(y.astype(BF16), wco_ref[...])
    mla_out = _dot(o_ref[...], wmo_ref[...])
    gates = gates_ref[...].astype(F32)
    merged = gates[:, :D_MODEL] * conv_out + gates[:, D_MODEL:] * mla_out
    h1 = x_ref[...] + _dot(merged.astype(BF16), wout_ref[...])

    u = _rms(h1, xg_ref[...]).astype(BF16)
    q = (_dot(u, wxq_ref[...]) * (X_HEAD_DIM ** -0.5)).astype(BF16)
    heads = []
    for h in range(X_HEADS):
        hs = slice(h * X_HEAD_DIM, (h + 1) * X_HEAD_DIM)
        s = _dot(q[:, hs], mkt_ref[0, hs, :])
        p = jnp.exp(s - jnp.max(s, axis=-1, keepdims=True))
        p = p / jnp.sum(p, axis=-1, keepdims=True)
        heads.append(_dot(p.astype(BF16), mv_ref[0, :, hs]))
    xo = jnp.concatenate(heads, axis=1).astype(BF16)
    h_ref[...] = h1 + _dot(xo, wxo_ref[...])


def _mixer(x2, z, gates, o, cw, cb, lng, lnb, wco, wmo, wout, xg, wxq, mkt, mv, wxo, seq):
    t = x2.shape[0]
    tm = TOKEN_TILE
    tiles_per_seq = seq // tm
    halo_per_tile = tm // HALO
    mem_len = mv.shape[1]
    xw = X_HEADS * X_HEAD_DIM
    full = lambda arr: pl.BlockSpec(arr.shape, lambda i: (0,) * arr.ndim)
    row = lambda w: pl.BlockSpec((tm, w), lambda i: (i, 0))
    in_specs = [row(D_MODEL), row(CONV_CH),
                pl.BlockSpec((HALO, CONV_CH), lambda i: (jnp.maximum(i * halo_per_tile - 1, 0), 0)),
                row(2 * D_MODEL), row(MLA_HEADS * V_DIM),
                full(cw), full(cb), full(lng), full(lnb), full(wco), full(wmo), full(wout),
                full(xg), full(wxq),
                pl.BlockSpec((1, xw, mem_len), lambda i: (i // tiles_per_seq, 0, 0)),
                pl.BlockSpec((1, mem_len, xw), lambda i: (i // tiles_per_seq, 0, 0)),
                full(wxo)]
    return pl.pallas_call(
        functools.partial(_mixer_kernel, tiles_per_seq),
        grid=(t // tm,),
        in_specs=in_specs,
        out_specs=row(D_MODEL),
        out_shape=jax.ShapeDtypeStruct((t, D_MODEL), F32),
        scratch_shapes=[pltpu.VMEM((SUBLANES, tm + HALO, CONV_CH), F32), pltpu.VMEM((tm, CONV_CH), F32)],
        compiler_params=pltpu.CompilerParams(dimension_semantics=("parallel",),
                                             vmem_limit_bytes=VMEM_LIMIT),
        name="mixer",
    )(x2, z, z, gates, o, cw, cb, lng, lnb, wco, wmo, wout, xg, wxq, mkt, mv, wxo)


def _mlp_kernel(h_ref, g_ref, w1_ref, w2_ref, fg_ref, y_ref):
    h = h_ref[...]
    u = _rms(h, g_ref[...]).astype(BF16)
    chunk = 1024
    acc = h
    for c in range(0, D_FF, chunk):
        a = jnp.maximum(_dot(u, w1_ref[:, c:c + chunk]), 0.0)
        acc = acc + _dot((a * a).astype(BF16), w2_ref[c:c + chunk, :])
    y_ref[...] = _rms(acc, fg_ref[...])


def _mlp(h, g, w1, w2, fg):
    t = h.shape[0]
    tm = MLP_TILE
    full = lambda arr: pl.BlockSpec(arr.shape, lambda i: (0,) * arr.ndim, pipeline_mode=pl.Buffered(1))
    row = pl.BlockSpec((tm, D_MODEL), lambda i: (i, 0))
    return pl.pallas_call(
        _mlp_kernel,
        grid=(t // tm,),
        in_specs=[row, full(g), full(w1), full(w2), full(fg)],
        out_specs=row,
        out_shape=jax.ShapeDtypeStruct((t, D_MODEL), F32),
        compiler_params=pltpu.CompilerParams(dimension_semantics=("parallel",),
                                             vmem_limit_bytes=VMEM_LIMIT),
        name="mlp",
    )(h, g, w1, w2, fg)


def _rope_placement():
    e = np.zeros((HEAD_PAD, MLA_HEADS * HEAD_PAD), np.float32)
    for h in range(MLA_HEADS):
        for j in range(QK_ROPE):
            e[j, h * HEAD_PAD + QK_NOPE + j] = 1.0
    return jnp.asarray(e, BF16)


def kernel(x, mem, positions, norm_mix_g, w_in, conv_w, conv_b, conv_ln_g, conv_ln_b, w_conv_out, q_norm_g, w_uq, kv_norm_g, w_ukv, w_mla_out, w_out, norm_xattn_g, norm_mem_g, w_xq, w_xkv, w_xo, norm_mlp_g, w_mlp1, w_mlp2, final_norm_g):
    batch, seq, _ = x.shape
    mem_len = mem.shape[1]
    t = batch * seq
    assert w_in.shape[0] == 1, "single-layer block (the final rmsnorm is fused into the MLP kernel)"
    assert seq % KV_CHUNK == 0 and seq % TOKEN_TILE == 0 and t % ROPE_TILE == 0 and t % MLP_TILE == 0

    inv_freq = ROPE_THETA ** (-jnp.arange(ROPE_HALF, dtype=F32) / ROPE_HALF)
    cos_t, sin_t, cs = _rope_tables(positions.reshape(1, t), inv_freq.reshape(ROPE_HALF, 1))
    place = _rope_placement()
    row = lambda v: v.reshape(1, -1)

    h = x.reshape(t, D_MODEL)
    wi = _regroup_w_in(jnp.swapaxes(w_in, 1, 2))
    wuq = w_uq[0].astype(BF16).reshape(Q_LORA, MLA_HEADS, QK_NOPE + QK_ROPE)
    wqt = jnp.pad(wuq, ((0, 0), (0, 0), (0, HEAD_PAD - QK_NOPE - QK_ROPE))).reshape(Q_LORA, -1).T
    wukv = w_ukv[0].astype(BF16).reshape(KV_LORA, MLA_HEADS, QK_NOPE + V_DIM)
    wk = jnp.pad(wukv[:, :, :QK_NOPE], ((0, 0), (0, 0), (0, HEAD_PAD - QK_NOPE))).reshape(KV_LORA, -1)
    wvt = jnp.pad(wukv[:, :, QK_NOPE:], ((0, 0), (0, 0), (0, V_ROWS - V_DIM))).reshape(KV_LORA, -1).T
    ones = np.zeros((MLA_HEADS, V_ROWS, 1), np.float32)
    ones[:, V_DIM] = 1.0
    ones = jnp.asarray(ones.reshape(MLA_HEADS * V_ROWS, 1))

    z, gates, qt, kp, vt3 = _in_proj(
        h, row(norm_mix_g[0]), wi, row(q_norm_g[0]), row(kv_norm_g[0]), wqt, wk, wvt, ones, place,
        cos_t, sin_t, cs)

    k3 = kp.reshape(t // KV_CHUNK, KV_CHUNK, MLA_HEADS * HEAD_PAD)
    o = _attention(qt, k3, vt3, batch, seq)

    xw = X_HEADS * X_HEAD_DIM
    wxkv = w_xkv[0].astype(BF16)
    mkt, mv = _mem_kv(mem.reshape(batch * mem_len, D_MODEL), row(norm_mem_g[0]),
                      wxkv[:, :xw].T, wxkv[:, xw:], batch, mem_len)

    cw = jnp.pad(conv_w[0], ((0, HALO - CONV_WIDTH), (0, 0)))
    h = _mixer(h, z, gates, o, cw, row(conv_b[0]), row(conv_ln_g[0]), row(conv_ln_b[0]),
               w_conv_out[0].astype(BF16), w_mla_out[0].astype(BF16), w_out[0].astype(BF16),
               row(norm_xattn_g[0]), w_xq[0].astype(BF16), mkt, mv, w_xo[0].astype(BF16), seq)
    h = _mlp(h, row(norm_mlp_g[0]), w_mlp1[0].astype(BF16), w_mlp2[0].astype(BF16),
             row(final_norm_g))
    return h.reshape(batch, seq, D_MODEL)
```

```python
import functools

import numpy as np
import jax
import jax.numpy as jnp
from jax import lax
from jax.experimental import pallas as pl
from jax.experimental.pallas import tpu as pltpu

D_MODEL = 1024
CONV_CH = D_MODEL // 2
CONV_WIDTH = 31
MLA_HEADS = 8
QK_NOPE = D_MODEL // 16
QK_ROPE = D_MODEL // 32
V_DIM = D_MODEL // 16
Q_LORA = 3 * D_MODEL // 8
KV_LORA = D_MODEL // 4
X_HEADS = 4
X_HEAD_DIM = D_MODEL // 8
D_FF = 4 * D_MODEL
ROPE_THETA = 10000.0
EPS = 1e-6

HEAD_PAD = 128
V_ROWS = V_DIM + 16
ATTN_HEADS = 4
ROPE_HALF = QK_ROPE // 2
SUBLANES = 8
BF16_ROWS = 2 * SUBLANES
HALO = 32
TOKEN_TILE = 512
MLP_TILE = 1024
Q_TILE = 512
KV_CHUNK = 512
ROPE_TILE = 2048
LOG2E = 1.4426950408889634


def _column_ranges(widths):
    out, start = {}, 0
    for name, width in widths:
        out[name] = (start, start + width)
        start += width
    return out


IN_COLS = _column_ranges([("glu_a", CONV_CH), ("glu_g", CONV_CH), ("gates", 2 * D_MODEL),
                          ("c_q", Q_LORA), ("c_kv", KV_LORA), ("k_rope", HEAD_PAD)])
MASK_VALUE = -1e30
VMEM_LIMIT = 56 * 1024 * 1024

F32 = jnp.float32
BF16 = jnp.bfloat16


def _rms(x, g):
    return x * lax.rsqrt(jnp.mean(x * x, axis=-1, keepdims=True) + EPS) * g


def _sigmoid(x):
    return 1.0 / (1.0 + jnp.exp(-x))


def _dot(a, b):
    return jnp.dot(a, b, preferred_element_type=F32)


def _dot_nt(a, b):
    return lax.dot_general(a, b, (((1,), (1,)), ((), ())), preferred_element_type=F32)


def _rope_kernel(pos_ref, invf_ref, cos_ref, sin_ref, cs_ref):
    ang = pos_ref[...].astype(F32) * invf_ref[...]
    c = jnp.cos(ang)
    s = jnp.sin(ang)
    cos_ref[...] = c
    sin_ref[...] = s
    pad = jnp.zeros((HEAD_PAD - 2 * QK_ROPE, ang.shape[1]), F32)
    cs_ref[...] = jnp.concatenate([c, c, -s, s, pad], axis=0).T


def _rope_tables(pos_row, inv_freq_col):
    t = pos_row.shape[1]
    out = jax.ShapeDtypeStruct((ROPE_HALF, t), F32)
    spec = pl.BlockSpec((ROPE_HALF, ROPE_TILE), lambda i: (0, i))
    return pl.pallas_call(
        _rope_kernel,
        grid=(t // ROPE_TILE,),
        in_specs=[pl.BlockSpec((1, ROPE_TILE), lambda i: (0, i)),
                  pl.BlockSpec((ROPE_HALF, 1), lambda i: (0, 0))],
        out_specs=[spec, spec, pl.BlockSpec((ROPE_TILE, HEAD_PAD), lambda i: (i, 0))],
        out_shape=[out, out, jax.ShapeDtypeStruct((t, HEAD_PAD), F32)],
        name="rope_tables",
    )(pos_row, inv_freq_col)


def _regroup_kernel(w_ref, o_ref):
    c0 = 2 * CONV_CH
    c2 = c0 + Q_LORA + KV_LORA
    c3 = c2 + QK_ROPE

    def put(name, value):
        o_ref[IN_COLS[name][0]:IN_COLS[name][0] + value.shape[0], :] = value.astype(BF16)

    put("glu_a", w_ref[0, :CONV_CH, :])
    put("glu_g", w_ref[0, CONV_CH:c0, :])
    put("gates", w_ref[0, c3:, :])
    put("c_q", w_ref[0, c0:c0 + Q_LORA, :])
    put("c_kv", w_ref[0, c0 + Q_LORA:c2, :])
    kr = w_ref[0, c2:c3, :]
    zeros = jnp.zeros((HEAD_PAD - 2 * QK_ROPE, kr.shape[1]), F32)
    put("k_rope", jnp.concatenate([kr, kr[ROPE_HALF:], kr[:ROPE_HALF], zeros], axis=0))


def _regroup_w_in(wt):
    lanes = 256
    rows = IN_COLS["k_rope"][1]
    return pl.pallas_call(
        _regroup_kernel,
        grid=(wt.shape[2] // lanes,),
        in_specs=[pl.BlockSpec((1, wt.shape[1], lanes), lambda i: (0, 0, i))],
        out_specs=pl.BlockSpec((rows, lanes), lambda i: (0, i)),
        out_shape=jax.ShapeDtypeStruct((rows, wt.shape[2]), BF16),
        name="regroup_w_in",
    )(wt)


def _in_proj_kernel(x_ref, g_ref, w_ref, qg_ref, kvg_ref, wqt_ref, wk_ref, wvt_ref, ones_ref, place_ref,
                    cos_ref, sin_ref, cs_ref,
                    z_ref, gates_ref, qt_ref, k_ref, vt_ref):
    u = _rms(x_ref[...], g_ref[...]).astype(BF16)
    proj = lambda name: _dot_nt(u, w_ref[IN_COLS[name][0]:IN_COLS[name][1], :])
    z_ref[...] = (proj("glu_a") * _sigmoid(proj("glu_g"))).astype(BF16)
    gates_ref[...] = _sigmoid(proj("gates")).astype(BF16)

    cqn = _rms(proj("c_q"), qg_ref[...]).astype(BF16)
    qt = _dot_nt(wqt_ref[...], cqn)
    scale = (QK_NOPE + QK_ROPE) ** -0.5 * LOG2E
    c = cos_ref[...]
    s = sin_ref[...]
    for h in range(MLA_HEADS):
        b = h * HEAD_PAD
        r1 = b + QK_NOPE
        r2 = r1 + ROPE_HALF
        r3 = r2 + ROPE_HALF
        t1 = qt[r1:r2]
        t2 = qt[r2:r3]
        qt_ref[b:r1, :] = (qt[b:r1] * scale).astype(BF16)
        qt_ref[r1:r2, :] = ((t1 * c - t2 * s) * scale).astype(BF16)
        qt_ref[r2:r3, :] = ((t2 * c + t1 * s) * scale).astype(BF16)
        qt_ref[r3:b + HEAD_PAD, :] = (qt[r3:b + HEAD_PAD] * scale).astype(BF16)

    ckvn = _rms(proj("c_kv"), kvg_ref[...]).astype(BF16)
    t = proj("k_rope") * cs_ref[...]
    rot = t + pltpu.roll(t, HEAD_PAD - QK_ROPE, 1)
    k_ref[...] = (_dot(ckvn, wk_ref[...]) + _dot(rot.astype(BF16), place_ref[...])).astype(BF16)
    vt_ref[0] = (_dot_nt(wvt_ref[...], ckvn) + ones_ref[...]).astype(BF16)


def _in_proj(x2, g, w, qg, kvg, wqt, wk, wvt, ones, place, cos_t, sin_t, cs):
    t = x2.shape[0]
    tm = KV_CHUNK
    n = t // tm
    full = lambda arr: pl.BlockSpec(arr.shape, lambda i: (0,) * arr.ndim)
    row = lambda w: pl.BlockSpec((tm, w), lambda i: (i, 0))
    in_specs = [row(D_MODEL), full(g), full(w), full(qg), full(kvg), full(wqt), full(wk), full(wvt),
                full(ones), full(place),
                pl.BlockSpec((ROPE_HALF, tm), lambda i: (0, i)),
                pl.BlockSpec((ROPE_HALF, tm), lambda i: (0, i)),
                row(HEAD_PAD)]
    out_shape = [jax.ShapeDtypeStruct((t, CONV_CH), BF16),
                 jax.ShapeDtypeStruct((t, 2 * D_MODEL), BF16),
                 jax.ShapeDtypeStruct((MLA_HEADS * HEAD_PAD, t), BF16),
                 jax.ShapeDtypeStruct((t, MLA_HEADS * HEAD_PAD), BF16),
                 jax.ShapeDtypeStruct((n, MLA_HEADS * V_ROWS, tm), BF16)]
    out_specs = [row(CONV_CH), row(2 * D_MODEL),
                 pl.BlockSpec((MLA_HEADS * HEAD_PAD, tm), lambda i: (0, i)),
                 row(MLA_HEADS * HEAD_PAD),
                 pl.BlockSpec((1, MLA_HEADS * V_ROWS, tm), lambda i: (i, 0, 0))]
    return pl.pallas_call(
        _in_proj_kernel,
        grid=(n,),
        in_specs=in_specs,
        out_specs=out_specs,
        out_shape=out_shape,
        compiler_params=pltpu.CompilerParams(dimension_semantics=("parallel",),
                                             vmem_limit_bytes=VMEM_LIMIT),
        name="in_proj",
    )(x2, g, w, qg, kvg, wqt, wk, wvt, ones, place, cos_t, sin_t, cs)


def _attn_kernel(qt_ref, k_ref, vt_ref, o_ref, s_scr, p_scr, acc_scr):
    qi = pl.program_id(2)
    tk = k_ref.shape[1]
    tq = qt_ref.shape[1]
    heads = range(ATTN_HEADS)
    n_diag = tq // tk
    assert tq == n_diag * tk and n_diag in (1, 2)
    diag = qi * n_diag
    n_items = n_diag + diag
    key_idx = lax.broadcasted_iota(jnp.int32, (tk, tq), 0)
    qry_idx = lax.broadcasted_iota(jnp.int32, (tk, tq), 1)
    causal = key_idx <= qry_idx
    square = n_diag == 1
    half = tk // 2
    qts = [qt_ref[h * HEAD_PAD:(h + 1) * HEAD_PAD, :] for h in heads]

    def scores_to(slot, kc, diag_part, hs=heads):
        ms = []
        for h in hs:
            k = k_ref[kc, :, h * HEAD_PAD:(h + 1) * HEAD_PAD]
            if diag_part is None:
                s = _dot(k, qts[h])
                s_scr[slot, h] = s
                ms.append(jnp.max(s, axis=0, keepdims=True))
            elif square:
                top = jnp.where(causal[:half], _dot(k[:half], qts[h]), MASK_VALUE)
                low = jnp.where(causal[half:, half:], _dot(k[half:], qts[h][:, half:]), MASK_VALUE)
                s_scr[slot, h, :half, :] = top
                s_scr[slot, h, half:, half:] = low
                low_max = jnp.concatenate([jnp.full((1, half), MASK_VALUE, F32),
                                           jnp.max(low, axis=0, keepdims=True)], axis=1)
                ms.append(jnp.maximum(jnp.max(top, axis=0, keepdims=True), low_max))
            else:
                s = jnp.where(key_idx + diag_part * tk <= qry_idx, _dot(k, qts[h]), MASK_VALUE)
                s_scr[slot, h] = s
                ms.append(jnp.max(s, axis=0, keepdims=True))
        return tuple(ms)

    def softmax_to(src, dst, ms, hs=heads, triangle=False):
        for h, m in zip(hs, ms):
            for r in range(0, tk, BF16_ROWS):
                rows = slice(r, r + BF16_ROWS)
                if triangle and r >= half:
                    p_scr[dst, h, rows, :half] = jnp.zeros((BF16_ROWS, half), BF16)
                    p_scr[dst, h, rows, half:] = jnp.exp2(s_scr[src, h, rows, half:] - m[:, half:]).astype(BF16)
                else:
                    p_scr[dst, h, rows, :] = jnp.exp2(s_scr[src, h, rows, :] - m).astype(BF16)

    def accumulate(m_run, slot, kc, ms, hs=heads):
        new = []
        for h, m_old, m_chunk in zip(hs, m_run, ms):
            pv = _dot(vt_ref[kc, h * V_ROWS:(h + 1) * V_ROWS, :], p_scr[slot, h])
            m_new = jnp.maximum(m_old, m_chunk)
            a = jnp.exp2(m_old - m_new)
            b = jnp.exp2(m_chunk - m_new)
            acc_scr[h] = a * acc_scr[h] + b * pv
            new.append(m_new)
        return tuple(new)

    def step(cur, look_ahead, j, carry):
        m_run, pend_kc, pend_m, next_kc, next_m = carry
        ahead_kc = j + 2 - n_diag
        new_run, ahead_m = [], []
        for h in heads:
            new_run += accumulate(m_run[h:h + 1], 1 - cur, pend_kc, pend_m[h:h + 1], (h,))
            softmax_to(1 - cur, cur, next_m[h:h + 1], (h,))
            if look_ahead:
                ahead_m += scores_to(cur, ahead_kc, None, (h,))
        return tuple(new_run), next_kc, next_m, ahead_kc, (tuple(ahead_m) if look_ahead else next_m)

    branches = [functools.partial(step, cur, look_ahead) for look_ahead in (True, False) for cur in (0, 1)]

    def body(j, carry):
        return lax.switch(2 * (j == n_items - 2).astype(jnp.int32) + j % 2, branches, j, carry)

    acc_scr[...] = jnp.zeros(acc_scr.shape, F32)
    m_init = tuple(jnp.full((1, tq), MASK_VALUE, F32) for _ in heads)
    second_kc, second_part = (diag + 1, 1) if n_diag > 1 else (0, None)
    first_m, second_m = [], []
    for h in heads:
        first_m += scores_to(0, diag, 0, (h,))
        second_m += scores_to(1, second_kc, second_part, (h,))
        softmax_to(0, 1, first_m[h:h + 1], (h,), triangle=square)
    carry = (m_init, diag, tuple(first_m), second_kc, tuple(second_m))
    m_run, pend_kc, pend_m, _, _ = lax.fori_loop(0, n_items - 1, body, carry)
    accumulate(m_run, n_items % 2, pend_kc, pend_m)
    outs = [acc_scr[h, :V_DIM, :] / acc_scr[h, V_DIM:V_DIM + 1, :] for h in heads]
    o_ref[...] = jnp.concatenate(outs, axis=0).astype(BF16)


def _attention(qt, k3, vt3, batch, seq):
    tq = Q_TILE
    nq = seq // tq
    nk = seq // KV_CHUNK
    t = batch * seq
    return pl.pallas_call(
        _attn_kernel,
        grid=(batch, MLA_HEADS // ATTN_HEADS, nq),
        in_specs=[pl.BlockSpec((ATTN_HEADS * HEAD_PAD, tq), lambda b, hg, qi: (hg, b * nq + qi)),
                  pl.BlockSpec((nk, KV_CHUNK, ATTN_HEADS * HEAD_PAD), lambda b, hg, qi: (b, 0, hg)),
                  pl.BlockSpec((nk, ATTN_HEADS * V_ROWS, KV_CHUNK), lambda b, hg, qi: (b, hg, 0))],
        out_specs=pl.BlockSpec((ATTN_HEADS * V_DIM, tq), lambda b, hg, qi: (hg, b * nq + qi)),
        out_shape=jax.ShapeDtypeStruct((MLA_HEADS * V_DIM, t), BF16),
        scratch_shapes=[pltpu.VMEM((2, ATTN_HEADS, KV_CHUNK, tq), F32),
                        pltpu.VMEM((2, ATTN_HEADS, KV_CHUNK, tq), BF16),
                        pltpu.VMEM((ATTN_HEADS, V_ROWS, tq), F32)],
        compiler_params=pltpu.CompilerParams(
            dimension_semantics=("parallel", "parallel", "arbitrary"),
            vmem_limit_bytes=VMEM_LIMIT),
        name="mla_attention",
    )(qt, k3, vt3)


def _mem_kernel(mem_ref, g_ref, wkt_ref, wv_ref, kt_ref, v_ref):
    mn = _rms(mem_ref[...], g_ref[...]).astype(BF16)
    kt_ref[0] = _dot_nt(wkt_ref[...], mn).astype(BF16)
    v_ref[0] = _dot(mn, wv_ref[...]).astype(BF16)


def _mem_kv(mem2, g, wkt, wv, batch, mem_len):
    xw = X_HEADS * X_HEAD_DIM
    full = lambda arr: pl.BlockSpec(arr.shape, lambda b: (0,) * arr.ndim)
    return pl.pallas_call(
        _mem_kernel,
        grid=(batch,),
        in_specs=[pl.BlockSpec((mem_len, D_MODEL), lambda b: (b, 0)), full(g), full(wkt), full(wv)],
        out_specs=[pl.BlockSpec((1, xw, mem_len), lambda b: (b, 0, 0)),
                   pl.BlockSpec((1, mem_len, xw), lambda b: (b, 0, 0))],
        out_shape=[jax.ShapeDtypeStruct((batch, xw, mem_len), BF16),
                   jax.ShapeDtypeStruct((batch, mem_len, xw), BF16)],
        compiler_params=pltpu.CompilerParams(dimension_semantics=("parallel",),
                                             vmem_limit_bytes=VMEM_LIMIT),
        name="mem_kv",
    )(mem2, g, wkt, wv)


def _mixer_kernel(tiles_per_seq, x_ref, z_ref, halo_ref, gates_ref, o_ref,
                  cw_ref, cb_ref, lng_ref, lnb_ref, wco_ref, wmo_ref, wout_ref,
                  xg_ref, wxq_ref, mkt_ref, mv_ref, wxo_ref,
                  h_ref, zext_ref, conv_ref):
    tm = x_ref.shape[0]
    first = (pl.program_id(0) % tiles_per_seq) == 0
    ext = tm + HALO
    zext_ref[0, 0:HALO, :] = jnp.where(first, 0.0, halo_ref[...].astype(F32))
    zext_ref[0, HALO:, :] = z_ref[...].astype(F32)
    for s in range(1, SUBLANES):
        zext_ref[s, 0:ext - SUBLANES, :] = zext_ref[0, s:s + ext - SUBLANES, :]

    rows = 64
    off = HALO - (CONV_WIDTH - 1)
    for c in range(CONV_CH // 128):
        cs = slice(c * 128, (c + 1) * 128)
        for r in range(0, tm, rows):
            acc = jnp.broadcast_to(cb_ref[:, cs], (rows, 128))
            for w in range(CONV_WIDTH):
                shift, base = (off + w) % SUBLANES, (off + w) // SUBLANES * SUBLANES
                acc = acc + zext_ref[shift, r + base:r + base + rows, cs] * cw_ref[w:w + 1, cs]
            conv_ref[r:r + rows, cs] = acc

    y = conv_ref[...]
    mu = jnp.mean(y, axis=-1, keepdims=True)
    yc = y - mu
    var = jnp.mean(yc * yc, axis=-1, keepdims=True)
    y = yc * lax.rsqrt(var + EPS) * lng_ref[...] + lnb_ref[...]
    y = y * _sigmoid(y)
    conv_out = _dot(y.astype(BF16), wco_ref[...])
    mla_out = lax.dot_general(o_ref[...], wmo_ref[...], (((0,), (0,)), ((), ())),
                              preferred_element_type=F32)
    gates = gates_ref[...].astype(F32)
    merged = gates[:, :D_MODEL] * conv_out + gates[:, D_MODEL:] * mla_out
    h1 = x_ref[...] + _dot(merged.astype(BF16), wout_ref[...])

    u = _rms(h1, xg_ref[...]).astype(BF16)
    q = (_dot(u, wxq_ref[...]) * (X_HEAD_DIM ** -0.5)).astype(BF16)
    heads = []
    for h in range(X_HEADS):
        hs = slice(h * X_HEAD_DIM, (h + 1) * X_HEAD_DIM)
        s = _dot(q[:, hs], mkt_ref[0, hs, :])
        p = jnp.exp(s - jnp.max(s, axis=-1, keepdims=True))
        p = p / jnp.sum(p, axis=-1, keepdims=True)
        heads.append(_dot(p.astype(BF16), mv_ref[0, :, hs]))
    xo = jnp.concatenate(heads, axis=1).astype(BF16)
    h_ref[...] = h1 + _dot(xo, wxo_ref[...])


def _mixer(x2, z, gates, o, cw, cb, lng, lnb, wco, wmo, wout, xg, wxq, mkt, mv, wxo, seq):
    t = x2.shape[0]
    tm = TOKEN_TILE
    tiles_per_seq = seq // tm
    halo_per_tile = tm // HALO
    mem_len = mv.shape[1]
    xw = X_HEADS * X_HEAD_DIM
    full = lambda arr: pl.BlockSpec(arr.shape, lambda i: (0,) * arr.ndim)
    row = lambda w: pl.BlockSpec((tm, w), lambda i: (i, 0))
    in_specs = [row(D_MODEL), row(CONV_CH),
                pl.BlockSpec((HALO, CONV_CH), lambda i: (jnp.maximum(i * halo_per_tile - 1, 0), 0)),
                row(2 * D_MODEL), pl.BlockSpec((MLA_HEADS * V_DIM, tm), lambda i: (0, i)),
                full(cw), full(cb), full(lng), full(lnb), full(wco), full(wmo), full(wout),
                full(xg), full(wxq),
                pl.BlockSpec((1, xw, mem_len), lambda i: (i // tiles_per_seq, 0, 0)),
                pl.BlockSpec((1, mem_len, xw), lambda i: (i // tiles_per_seq, 0, 0)),
                full(wxo)]
    return pl.pallas_call(
        functools.partial(_mixer_kernel, tiles_per_seq),
        grid=(t // tm,),
        in_specs=in_specs,
        out_specs=row(D_MODEL),
        out_shape=jax.ShapeDtypeStruct((t, D_MODEL), F32),
        scratch_shapes=[pltpu.VMEM((SUBLANES, tm + HALO, CONV_CH), F32), pltpu.VMEM((tm, CONV_CH), F32)],
        compiler_params=pltpu.CompilerParams(dimension_semantics=("parallel",),
                                             vmem_limit_bytes=VMEM_LIMIT),
        name="mixer",
    )(x2, z, z, gates, o, cw, cb, lng, lnb, wco, wmo, wout, xg, wxq, mkt, mv, wxo)


def _mlp_kernel(h_ref, g_ref, w1_ref, w2_ref, fg_ref, y_ref):
    h = h_ref[...]
    u = _rms(h, g_ref[...]).astype(BF16)
    chunk = 1024
    acc = h
    for c in range(0, D_FF, chunk):
        a = jnp.maximum(_dot(u, w1_ref[:, c:c + chunk]), 0.0)
        acc = acc + _dot((a * a).astype(BF16), w2_ref[c:c + chunk, :])
    y_ref[...] = _rms(acc, fg_ref[...])


def _mlp(h, g, w1, w2, fg):
    t = h.shape[0]
    tm = MLP_TILE
    full = lambda arr: pl.BlockSpec(arr.shape, lambda i: (0,) * arr.ndim, pipeline_mode=pl.Buffered(1))
    row = pl.BlockSpec((tm, D_MODEL), lambda i: (i, 0))
    return pl.pallas_call(
        _mlp_kernel,
        grid=(t // tm,),
        in_specs=[row, full(g), full(w1), full(w2), full(fg)],
        out_specs=row,
        out_shape=jax.ShapeDtypeStruct((t, D_MODEL), F32),
        compiler_params=pltpu.CompilerParams(dimension_semantics=("parallel",),
                                             vmem_limit_bytes=VMEM_LIMIT),
        name="mlp",
    )(h, g, w1, w2, fg)


def _rope_placement():
    e = np.zeros((HEAD_PAD, MLA_HEADS * HEAD_PAD), np.float32)
    for h in range(MLA_HEADS):
        for j in range(QK_ROPE):
            e[j, h * HEAD_PAD + QK_NOPE + j] = 1.0
    return jnp.asarray(e, BF16)


def kernel(x, mem, positions, norm_mix_g, w_in, conv_w, conv_b, conv_ln_g, conv_ln_b, w_conv_out, q_norm_g, w_uq, kv_norm_g, w_ukv, w_mla_out, w_out, norm_xattn_g, norm_mem_g, w_xq, w_xkv, w_xo, norm_mlp_g, w_mlp1, w_mlp2, final_norm_g):
    batch, seq, _ = x.shape
    mem_len = mem.shape[1]
    t = batch * seq
    assert w_in.shape[0] == 1, "single-layer block (the final rmsnorm is fused into the MLP kernel)"
    assert seq % KV_CHUNK == 0 and seq % TOKEN_TILE == 0 and t % ROPE_TILE == 0 and t % MLP_TILE == 0

    inv_freq = ROPE_THETA ** (-jnp.arange(ROPE_HALF, dtype=F32) / ROPE_HALF)
    cos_t, sin_t, cs = _rope_tables(positions.reshape(1, t), inv_freq.reshape(ROPE_HALF, 1))
    place = _rope_placement()
    row = lambda v: v.reshape(1, -1)

    h = x.reshape(t, D_MODEL)
    wi = _regroup_w_in(jnp.swapaxes(w_in, 1, 2))
    wuq = w_uq[0].astype(BF16).reshape(Q_LORA, MLA_HEADS, QK_NOPE + QK_ROPE)
    wqt = jnp.pad(wuq, ((0, 0), (0, 0), (0, HEAD_PAD - QK_NOPE - QK_ROPE))).reshape(Q_LORA, -1).T
    wukv = w_ukv[0].astype(BF16).reshape(KV_LORA, MLA_HEADS, QK_NOPE + V_DIM)
    wk = jnp.pad(wukv[:, :, :QK_NOPE], ((0, 0), (0, 0), (0, HEAD_PAD - QK_NOPE))).reshape(KV_LORA, -1)
    wvt = jnp.pad(wukv[:, :, QK_NOPE:], ((0, 0), (0, 0), (0, V_ROWS - V_DIM))).reshape(KV_LORA, -1).T
    ones = np.zeros((MLA_HEADS, V_ROWS, 1), np.float32)
    ones[:, V_DIM] = 1.0
    ones = jnp.asarray(ones.reshape(MLA_HEADS * V_ROWS, 1))

    z, gates, qt, kp, vt3 = _in_proj(
        h, row(norm_mix_g[0]), wi, row(q_norm_g[0]), row(kv_norm_g[0]), wqt, wk, wvt, ones, place,
        cos_t, sin_t, cs)

    k3 = kp.reshape(t // KV_CHUNK, KV_CHUNK, MLA_HEADS * HEAD_PAD)
    o = _attention(qt, k3, vt3, batch, seq)

    xw = X_HEADS * X_HEAD_DIM
    wxkv = w_xkv[0].astype(BF16)
    mkt, mv = _mem_kv(mem.reshape(batch * mem_len, D_MODEL), row(norm_mem_g[0]),
                      wxkv[:, :xw].T, wxkv[:, xw:], batch, mem_len)

    cw = jnp.pad(conv_w[0], ((0, HALO - CONV_WIDTH), (0, 0)))
    h = _mixer(h, z, gates, o, cw, row(conv_b[0]), row(conv_ln_g[0]), row(conv_ln_b[0]),
               w_conv_out[0].astype(BF16), w_mla_out[0].astype(BF16), w_out[0].astype(BF16),
               row(norm_xattn_g[0]), w_xq[0].astype(BF16), mkt, mv, w_xo[0].astype(BF16), seq)
    h = _mlp(h, row(norm_mlp_g[0]), w_mlp1[0].astype(BF16), w_mlp2[0].astype(BF16),
             row(final_norm_g))
    return h.reshape(batch, seq, D_MODEL)
```

```python
import functools

import numpy as np
import jax
import jax.numpy as jnp
from jax import lax
from jax.experimental import pallas as pl
from jax.experimental.pallas import tpu as pltpu

D_MODEL = 1024
CONV_CH = D_MODEL // 2
CONV_WIDTH = 31
MLA_HEADS = 8
QK_NOPE = D_MODEL // 16
QK_ROPE = D_MODEL // 32
V_DIM = D_MODEL // 16
Q_LORA = 3 * D_MODEL // 8
KV_LORA = D_MODEL // 4
X_HEADS = 4
X_HEAD_DIM = D_MODEL // 8
D_FF = 4 * D_MODEL
ROPE_THETA = 10000.0
EPS = 1e-6

HEAD_PAD = 128
V_ROWS = V_DIM + 16
ATTN_HEADS = 4
ROPE_HALF = QK_ROPE // 2
SUBLANES = 8
BF16_ROWS = 2 * SUBLANES
HALO = 32
TOKEN_TILE = 1024
CONV_CHUNK = 256
MLP_TILE = 1024
Q_TILE = 512
KV_CHUNK = 512
ROPE_TILE = 2048
LOG2E = 1.4426950408889634


def _column_ranges(widths):
    out, start = {}, 0
    for name, width in widths:
        out[name] = (start, start + width)
        start += width
    return out


IN_COLS = _column_ranges([("glu_a", CONV_CH), ("glu_g", CONV_CH), ("gates", 2 * D_MODEL),
                          ("c_q", Q_LORA), ("c_kv", KV_LORA), ("k_rope", HEAD_PAD)])
MASK_VALUE = -1e30
VMEM_LIMIT = 56 * 1024 * 1024

F32 = jnp.float32
BF16 = jnp.bfloat16


def _rms(x, g):
    return x * lax.rsqrt(jnp.mean(x * x, axis=-1, keepdims=True) + EPS) * g


def _sigmoid(x):
    return 1.0 / (1.0 + jnp.exp(-x))


def _dot(a, b):
    return jnp.dot(a, b, preferred_element_type=F32)


def _dot_nt(a, b):
    return lax.dot_general(a, b, (((1,), (1,)), ((), ())), preferred_element_type=F32)


def _rope_kernel(pos_ref, invf_ref, cos_ref, sin_ref, cs_ref):
    ang = pos_ref[...].astype(F32) * invf_ref[...]
    c = jnp.cos(ang)
    s = jnp.sin(ang)
    cos_ref[...] = c
    sin_ref[...] = s
    pad = jnp.zeros((HEAD_PAD - 2 * QK_ROPE, ang.shape[1]), F32)
    cs_ref[...] = jnp.concatenate([c, c, -s, s, pad], axis=0).T


def _rope_tables(pos_row, inv_freq_col):
    t = pos_row.shape[1]
    out = jax.ShapeDtypeStruct((ROPE_HALF, t), F32)
    spec = pl.BlockSpec((ROPE_HALF, ROPE_TILE), lambda i: (0, i))
    return pl.pallas_call(
        _rope_kernel,
        grid=(t // ROPE_TILE,),
        in_specs=[pl.BlockSpec((1, ROPE_TILE), lambda i: (0, i)),
                  pl.BlockSpec((ROPE_HALF, 1), lambda i: (0, 0))],
        out_specs=[spec, spec, pl.BlockSpec((ROPE_TILE, HEAD_PAD), lambda i: (i, 0))],
        out_shape=[out, out, jax.ShapeDtypeStruct((t, HEAD_PAD), F32)],
        name="rope_tables",
    )(pos_row, inv_freq_col)


def _regroup_kernel(w_ref, o_ref):
    c0 = 2 * CONV_CH
    c2 = c0 + Q_LORA + KV_LORA
    c3 = c2 + QK_ROPE

    def put(name, value):
        o_ref[IN_COLS[name][0]:IN_COLS[name][0] + value.shape[0], :] = value.astype(BF16)

    put("glu_a", w_ref[0, :CONV_CH, :])
    put("glu_g", w_ref[0, CONV_CH:c0, :])
    put("gates", w_ref[0, c3:, :])
    put("c_q", w_ref[0, c0:c0 + Q_LORA, :])
    put("c_kv", w_ref[0, c0 + Q_LORA:c2, :])
    kr = w_ref[0, c2:c3, :]
    zeros = jnp.zeros((HEAD_PAD - 2 * QK_ROPE, kr.shape[1]), F32)
    put("k_rope", jnp.concatenate([kr, kr[ROPE_HALF:], kr[:ROPE_HALF], zeros], axis=0))


def _regroup_w_in(wt):
    lanes = 256
    rows = IN_COLS["k_rope"][1]
    return pl.pallas_call(
        _regroup_kernel,
        grid=(wt.shape[2] // lanes,),
        in_specs=[pl.BlockSpec((1, wt.shape[1], lanes), lambda i: (0, 0, i))],
        out_specs=pl.BlockSpec((rows, lanes), lambda i: (0, i)),
        out_shape=jax.ShapeDtypeStruct((rows, wt.shape[2]), BF16),
        name="regroup_w_in",
    )(wt)


def _in_proj_kernel(x_ref, g_ref, w_ref, qg_ref, kvg_ref, wqt_ref, wk_ref, wvt_ref, ones_ref, place_ref,
                    cos_ref, sin_ref, cs_ref,
                    z_ref, gates_ref, qt_ref, k_ref, vt_ref):
    u = _rms(x_ref[...], g_ref[...]).astype(BF16)
    proj = lambda name: _dot_nt(u, w_ref[IN_COLS[name][0]:IN_COLS[name][1], :])
    z_ref[...] = (proj("glu_a") * _sigmoid(proj("glu_g"))).astype(BF16)
    gates_ref[...] = _sigmoid(proj("gates")).astype(BF16)

    cqn = _rms(proj("c_q"), qg_ref[...]).astype(BF16)
    qt = _dot_nt(wqt_ref[...], cqn)
    scale = (QK_NOPE + QK_ROPE) ** -0.5 * LOG2E
    c = cos_ref[...]
    s = sin_ref[...]
    for h in range(MLA_HEADS):
        b = h * HEAD_PAD
        r1 = b + QK_NOPE
        r2 = r1 + ROPE_HALF
        r3 = r2 + ROPE_HALF
        t1 = qt[r1:r2]
        t2 = qt[r2:r3]
        qt_ref[b:r1, :] = (qt[b:r1] * scale).astype(BF16)
        qt_ref[r1:r2, :] = ((t1 * c - t2 * s) * scale).astype(BF16)
        qt_ref[r2:r3, :] = ((t2 * c + t1 * s) * scale).astype(BF16)
        qt_ref[r3:b + HEAD_PAD, :] = (qt[r3:b + HEAD_PAD] * scale).astype(BF16)

    ckvn = _rms(proj("c_kv"), kvg_ref[...]).astype(BF16)
    t = proj("k_rope") * cs_ref[...]
    rot = t + pltpu.roll(t, HEAD_PAD - QK_ROPE, 1)
    k_ref[...] = (_dot(ckvn, wk_ref[...]) + _dot(rot.astype(BF16), place_ref[...])).astype(BF16)
    vt_ref[0] = (_dot_nt(wvt_ref[...], ckvn) + ones_ref[...]).astype(BF16)


def _in_proj(x2, g, w, qg, kvg, wqt, wk, wvt, ones, place, cos_t, sin_t, cs):
    t = x2.shape[0]
    tm = KV_CHUNK
    n = t // tm
    full = lambda arr: pl.BlockSpec(arr.shape, lambda i: (0,) * arr.ndim)
    row = lambda w: pl.BlockSpec((tm, w), lambda i: (i, 0))
    in_specs = [row(D_MODEL), full(g), full(w), full(qg), full(kvg), full(wqt), full(wk), full(wvt),
                full(ones), full(place),
                pl.BlockSpec((ROPE_HALF, tm), lambda i: (0, i)),
                pl.BlockSpec((ROPE_HALF, tm), lambda i: (0, i)),
                row(HEAD_PAD)]
    out_shape = [jax.ShapeDtypeStruct((t, CONV_CH), BF16),
                 jax.ShapeDtypeStruct((t, 2 * D_MODEL), BF16),
                 jax.ShapeDtypeStruct((MLA_HEADS * HEAD_PAD, t), BF16),
                 jax.ShapeDtypeStruct((t, MLA_HEADS * HEAD_PAD), BF16),
                 jax.ShapeDtypeStruct((n, MLA_HEADS * V_ROWS, tm), BF16)]
    out_specs = [row(CONV_CH), row(2 * D_MODEL),
                 pl.BlockSpec((MLA_HEADS * HEAD_PAD, tm), lambda i: (0, i)),
                 row(MLA_HEADS * HEAD_PAD),
                 pl.BlockSpec((1, MLA_HEADS * V_ROWS, tm), lambda i: (i, 0, 0))]
    return pl.pallas_call(
        _in_proj_kernel,
        grid=(n,),
        in_specs=in_specs,
        out_specs=out_specs,
        out_shape=out_shape,
        compiler_params=pltpu.CompilerParams(dimension_semantics=("parallel",),
                                             vmem_limit_bytes=VMEM_LIMIT),
        name="in_proj",
    )(x2, g, w, qg, kvg, wqt, wk, wvt, ones, place, cos_t, sin_t, cs)


def _attn_kernel(qt_ref, k_ref, vt_ref, o_ref, s_scr, p_scr, acc_scr):
    qi = pl.program_id(2)
    tk = k_ref.shape[1]
    tq = qt_ref.shape[1]
    heads = range(ATTN_HEADS)
    n_diag = tq // tk
    assert tq == n_diag * tk and n_diag in (1, 2)
    diag = qi * n_diag
    n_items = n_diag + diag
    key_idx = lax.broadcasted_iota(jnp.int32, (tk, tq), 0)
    qry_idx = lax.broadcasted_iota(jnp.int32, (tk, tq), 1)
    causal = key_idx <= qry_idx
    square = n_diag == 1
    half = tk // 2
    qts = [qt_ref[h * HEAD_PAD:(h + 1) * HEAD_PAD, :] for h in heads]

    def scores_to(slot, kc, diag_part, hs=heads):
        ms = []
        for h in hs:
            k = k_ref[kc, :, h * HEAD_PAD:(h + 1) * HEAD_PAD]
            if diag_part is None:
                s = _dot(k, qts[h])
                s_scr[slot, h] = s
                ms.append(jnp.max(s, axis=0, keepdims=True))
            elif square:
                top = jnp.where(causal[:half], _dot(k[:half], qts[h]), MASK_VALUE)
                low = jnp.where(causal[half:, half:], _dot(k[half:], qts[h][:, half:]), MASK_VALUE)
                s_scr[slot, h, :half, :] = top
                s_scr[slot, h, half:, half:] = low
                low_max = jnp.concatenate([jnp.full((1, half), MASK_VALUE, F32),
                                           jnp.max(low, axis=0, keepdims=True)], axis=1)
                ms.append(jnp.maximum(jnp.max(top, axis=0, keepdims=True), low_max))
            else:
                s = jnp.where(key_idx + diag_part * tk <= qry_idx, _dot(k, qts[h]), MASK_VALUE)
                s_scr[slot, h] = s
                ms.append(jnp.max(s, axis=0, keepdims=True))
        return tuple(ms)

    def softmax_to(src, dst, ms, hs=heads, triangle=False):
        for h, m in zip(hs, ms):
            for r in range(0, tk, BF16_ROWS):
                rows = slice(r, r + BF16_ROWS)
                if triangle and r >= half:
                    p_scr[dst, h, rows, :half] = jnp.zeros((BF16_ROWS, half), BF16)
                    p_scr[dst, h, rows, half:] = jnp.exp2(s_scr[src, h, rows, half:] - m[:, half:]).astype(BF16)
                else:
                    p_scr[dst, h, rows, :] = jnp.exp2(s_scr[src, h, rows, :] - m).astype(BF16)

    def accumulate(m_run, slot, kc, ms, hs=heads):
        new = []
        for h, m_old, m_chunk in zip(hs, m_run, ms):
            pv = _dot(vt_ref[kc, h * V_ROWS:(h + 1) * V_ROWS, :], p_scr[slot, h])
            m_new = jnp.maximum(m_old, m_chunk)
            a = jnp.exp2(m_old - m_new)
            b = jnp.exp2(m_chunk - m_new)
            acc_scr[h] = a * acc_scr[h] + b * pv
            new.append(m_new)
        return tuple(new)

    def step(cur, look_ahead, j, carry):
        m_run, pend_kc, pend_m, next_kc, next_m = carry
        ahead_kc = j + 2 - n_diag
        new_run, ahead_m = [], []
        for h in heads:
            new_run += accumulate(m_run[h:h + 1], 1 - cur, pend_kc, pend_m[h:h + 1], (h,))
            softmax_to(1 - cur, cur, next_m[h:h + 1], (h,))
            if look_ahead:
                ahead_m += scores_to(cur, ahead_kc, None, (h,))
        return tuple(new_run), next_kc, next_m, ahead_kc, (tuple(ahead_m) if look_ahead else next_m)

    branches = [functools.partial(step, cur, look_ahead) for look_ahead in (True, False) for cur in (0, 1)]

    def body(j, carry):
        return lax.switch(2 * (j == n_items - 2).astype(jnp.int32) + j % 2, branches, j, carry)

    acc_scr[...] = jnp.zeros(acc_scr.shape, F32)
    m_init = tuple(jnp.full((1, tq), MASK_VALUE, F32) for _ in heads)
    second_kc, second_part = (diag + 1, 1) if n_diag > 1 else (0, None)
    first_m, second_m = [], []
    for h in heads:
        first_m += scores_to(0, diag, 0, (h,))
        second_m += scores_to(1, second_kc, second_part, (h,))
        softmax_to(0, 1, first_m[h:h + 1], (h,), triangle=square)
    carry = (m_init, diag, tuple(first_m), second_kc, tuple(second_m))
    m_run, pend_kc, pend_m, _, _ = lax.fori_loop(0, n_items - 1, body, carry)
    accumulate(m_run, n_items % 2, pend_kc, pend_m)
    outs = [acc_scr[h, :V_DIM, :] / acc_scr[h, V_DIM:V_DIM + 1, :] for h in heads]
    o_ref[...] = jnp.concatenate(outs, axis=0).astype(BF16)


def _attention(qt, k3, vt3, batch, seq):
    tq = Q_TILE
    nq = seq // tq
    nk = seq // KV_CHUNK
    t = batch * seq
    return pl.pallas_call(
        _attn_kernel,
        grid=(batch, MLA_HEADS // ATTN_HEADS, nq),
        in_specs=[pl.BlockSpec((ATTN_HEADS * HEAD_PAD, tq), lambda b, hg, qi: (hg, b * nq + qi)),
                  pl.BlockSpec((nk, KV_CHUNK, ATTN_HEADS * HEAD_PAD), lambda b, hg, qi: (b, 0, hg)),
                  pl.BlockSpec((nk, ATTN_HEADS * V_ROWS, KV_CHUNK), lambda b, hg, qi: (b, hg, 0))],
        out_specs=pl.BlockSpec((ATTN_HEADS * V_DIM, tq), lambda b, hg, qi: (hg, b * nq + qi)),
        out_shape=jax.ShapeDtypeStruct((MLA_HEADS * V_DIM, t), BF16),
        scratch_shapes=[pltpu.VMEM((2, ATTN_HEADS, KV_CHUNK, tq), F32),
                        pltpu.VMEM((2, ATTN_HEADS, KV_CHUNK, tq), BF16),
                        pltpu.VMEM((ATTN_HEADS, V_ROWS, tq), F32)],
        compiler_params=pltpu.CompilerParams(
            dimension_semantics=("parallel", "parallel", "arbitrary"),
            vmem_limit_bytes=VMEM_LIMIT),
        name="mla_attention",
    )(qt, k3, vt3)


def _mem_kernel(mem_ref, g_ref, wkt_ref, wv_ref, kt_ref, v_ref):
    mn = _rms(mem_ref[...], g_ref[...]).astype(BF16)
    kt_ref[0] = _dot_nt(wkt_ref[...], mn).astype(BF16)
    v_ref[0] = _dot(mn, wv_ref[...]).astype(BF16)


def _mem_kv(mem2, g, wkt, wv, batch, mem_len):
    xw = X_HEADS * X_HEAD_DIM
    full = lambda arr: pl.BlockSpec(arr.shape, lambda b: (0,) * arr.ndim)
    return pl.pallas_call(
        _mem_kernel,
        grid=(batch,),
        in_specs=[pl.BlockSpec((mem_len, D_MODEL), lambda b: (b, 0)), full(g), full(wkt), full(wv)],
        out_specs=[pl.BlockSpec((1, xw, mem_len), lambda b: (b, 0, 0)),
                   pl.BlockSpec((1, mem_len, xw), lambda b: (b, 0, 0))],
        out_shape=[jax.ShapeDtypeStruct((batch, xw, mem_len), BF16),
                   jax.ShapeDtypeStruct((batch, mem_len, xw), BF16)],
        compiler_params=pltpu.CompilerParams(dimension_semantics=("parallel",),
                                             vmem_limit_bytes=VMEM_LIMIT),
        name="mem_kv",
    )(mem2, g, wkt, wv)


def _mixer_kernel(tiles_per_seq, x_ref, z_ref, halo_ref, gates_ref, o_ref,
                  cw_ref, cb_ref, lng_ref, lnb_ref, wco_ref, wmo_ref, wout_ref,
                  xg_ref, wxq_ref, mkt_ref, mv_ref, wxo_ref,
                  h_ref, zext_ref, shift_ref, conv_ref):
    tm = x_ref.shape[0]
    first = (pl.program_id(0) % tiles_per_seq) == 0
    zext_ref[0:HALO, :] = jnp.where(first, 0.0, halo_ref[...].astype(F32))
    zext_ref[HALO:, :] = z_ref[...].astype(F32)

    rows = 64
    off = HALO - (CONV_WIDTH - 1)
    win = CONV_CHUNK + HALO
    for r0 in range(0, tm, CONV_CHUNK):
        for s in range(1, SUBLANES):
            shift_ref[s - 1, 0:win - SUBLANES, :] = zext_ref[r0 + s:r0 + s + win - SUBLANES, :]
        for c in range(CONV_CH // 128):
            cs = slice(c * 128, (c + 1) * 128)
            for r in range(r0, r0 + CONV_CHUNK, rows):
                acc = jnp.broadcast_to(cb_ref[:, cs], (rows, 128))
                for w in range(CONV_WIDTH):
                    shift, base = (off + w) % SUBLANES, (off + w) // SUBLANES * SUBLANES
                    if shift == 0:
                        window = zext_ref[r + base:r + base + rows, cs]
                    else:
                        window = shift_ref[shift - 1, r - r0 + base:r - r0 + base + rows, cs]
                    acc = acc + window * cw_ref[w:w + 1, cs]
                conv_ref[r:r + rows, cs] = acc

    y = conv_ref[...]
    mu = jnp.mean(y, axis=-1, keepdims=True)
    yc = y - mu
    var = jnp.mean(yc * yc, axis=-1, keepdims=True)
    y = yc * lax.rsqrt(var + EPS) * lng_ref[...] + lnb_ref[...]
    y = y * _sigmoid(y)
    conv_out = _dot(y.astype(BF16), wco_ref[...])
    mla_out = lax.dot_general(o_ref[...], wmo_ref[...], (((0,), (0,)), ((), ())),
                              preferred_element_type=F32)
    gates = gates_ref[...].astype(F32)
    merged = gates[:, :D_MODEL] * conv_out + gates[:, D_MODEL:] * mla_out
    h1 = x_ref[...] + _dot(merged.astype(BF16), wout_ref[...])

    u = _rms(h1, xg_ref[...]).astype(BF16)
    q = (_dot(u, wxq_ref[...]) * (X_HEAD_DIM ** -0.5)).astype(BF16)
    heads = []
    for h in range(X_HEADS):
        hs = slice(h * X_HEAD_DIM, (h + 1) * X_HEAD_DIM)
        s = _dot(q[:, hs], mkt_ref[0, hs, :])
        p = jnp.exp(s - jnp.max(s, axis=-1, keepdims=True))
        p = p / jnp.sum(p, axis=-1, keepdims=True)
        heads.append(_dot(p.astype(BF16), mv_ref[0, :, hs]))
    xo = jnp.concatenate(heads, axis=1).astype(BF16)
    h_ref[...] = h1 + _dot(xo, wxo_ref[...])


def _mixer(x2, z, gates, o, cw, cb, lng, lnb, wco, wmo, wout, xg, wxq, mkt, mv, wxo, seq):
    t = x2.shape[0]
    tm = TOKEN_TILE
    tiles_per_seq = seq // tm
    halo_per_tile = tm // HALO
    mem_len = mv.shape[1]
    xw = X_HEADS * X_HEAD_DIM
    full = lambda arr: pl.BlockSpec(arr.shape, lambda i: (0,) * arr.ndim)
    row = lambda w: pl.BlockSpec((tm, w), lambda i: (i, 0))
    in_specs = [row(D_MODEL), row(CONV_CH),
                pl.BlockSpec((HALO, CONV_CH), lambda i: (jnp.maximum(i * halo_per_tile - 1, 0), 0)),
                row(2 * D_MODEL), pl.BlockSpec((MLA_HEADS * V_DIM, tm), lambda i: (0, i)),
                full(cw), full(cb), full(lng), full(lnb), full(wco), full(wmo), full(wout),
                full(xg), full(wxq),
                pl.BlockSpec((1, xw, mem_len), lambda i: (i // tiles_per_seq, 0, 0)),
                pl.BlockSpec((1, mem_len, xw), lambda i: (i // tiles_per_seq, 0, 0)),
                full(wxo)]
    return pl.pallas_call(
        functools.partial(_mixer_kernel, tiles_per_seq),
        grid=(t // tm,),
        in_specs=in_specs,
        out_specs=row(D_MODEL),
        out_shape=jax.ShapeDtypeStruct((t, D_MODEL), F32),
        scratch_shapes=[pltpu.VMEM((tm + HALO, CONV_CH), F32),
                        pltpu.VMEM((SUBLANES - 1, CONV_CHUNK + HALO, CONV_CH), F32),
                        pltpu.VMEM((tm, CONV_CH), F32)],
        compiler_params=pltpu.CompilerParams(dimension_semantics=("parallel",),
                                             vmem_limit_bytes=VMEM_LIMIT),
        name="mixer",
    )(x2, z, z, gates, o, cw, cb, lng, lnb, wco, wmo, wout, xg, wxq, mkt, mv, wxo)


def _mlp_kernel(h_ref, g_ref, w1_ref, w2_ref, fg_ref, y_ref):
    h = h_ref[...]
    u = _rms(h, g_ref[...]).astype(BF16)
    chunk = 1024
    acc = h
    for c in range(0, D_FF, chunk):
        a = jnp.maximum(_dot(u, w1_ref[:, c:c + chunk]), 0.0)
        acc = acc + _dot((a * a).astype(BF16), w2_ref[c:c + chunk, :])
    y_ref[...] = _rms(acc, fg_ref[...])


def _mlp(h, g, w1, w2, fg):
    t = h.shape[0]
    tm = MLP_TILE
    full = lambda arr: pl.BlockSpec(arr.shape, lambda i: (0,) * arr.ndim, pipeline_mode=pl.Buffered(1))
    row = pl.BlockSpec((tm, D_MODEL), lambda i: (i, 0))
    return pl.pallas_call(
        _mlp_kernel,
        grid=(t // tm,),
        in_specs=[row, full(g), full(w1), full(w2), full(fg)],
        out_specs=row,
        out_shape=jax.ShapeDtypeStruct((t, D_MODEL), F32),
        compiler_params=pltpu.CompilerParams(dimension_semantics=("parallel",),
                                             vmem_limit_bytes=VMEM_LIMIT),
        name="mlp",
    )(h, g, w1, w2, fg)


def _rope_placement():
    e = np.zeros((HEAD_PAD, MLA_HEADS * HEAD_PAD), np.float32)
    for h in range(MLA_HEADS):
        for j in range(QK_ROPE):
            e[j, h * HEAD_PAD + QK_NOPE + j] = 1.0
    return jnp.asarray(e, BF16)


def kernel(x, mem, positions, norm_mix_g, w_in, conv_w, conv_b, conv_ln_g, conv_ln_b, w_conv_out, q_norm_g, w_uq, kv_norm_g, w_ukv, w_mla_out, w_out, norm_xattn_g, norm_mem_g, w_xq, w_xkv, w_xo, norm_mlp_g, w_mlp1, w_mlp2, final_norm_g):
    batch, seq, _ = x.shape
    mem_len = mem.shape[1]
    t = batch * seq
    assert w_in.shape[0] == 1, "single-layer block (the final rmsnorm is fused into the MLP kernel)"
    assert seq % KV_CHUNK == 0 and seq % TOKEN_TILE == 0 and t % ROPE_TILE == 0 and t % MLP_TILE == 0

    inv_freq = ROPE_THETA ** (-jnp.arange(ROPE_HALF, dtype=F32) / ROPE_HALF)
    cos_t, sin_t, cs = _rope_tables(positions.reshape(1, t), inv_freq.reshape(ROPE_HALF, 1))
    place = _rope_placement()
    row = lambda v: v.reshape(1, -1)

    h = x.reshape(t, D_MODEL)
    wi = _regroup_w_in(jnp.swapaxes(w_in, 1, 2))
    wuq = w_uq[0].astype(BF16).reshape(Q_LORA, MLA_HEADS, QK_NOPE + QK_ROPE)
    wqt = jnp.pad(wuq, ((0, 0), (0, 0), (0, HEAD_PAD - QK_NOPE - QK_ROPE))).reshape(Q_LORA, -1).T
    wukv = w_ukv[0].astype(BF16).reshape(KV_LORA, MLA_HEADS, QK_NOPE + V_DIM)
    wk = jnp.pad(wukv[:, :, :QK_NOPE], ((0, 0), (0, 0), (0, HEAD_PAD - QK_NOPE))).reshape(KV_LORA, -1)
    wvt = jnp.pad(wukv[:, :, QK_NOPE:], ((0, 0), (0, 0), (0, V_ROWS - V_DIM))).reshape(KV_LORA, -1).T
    ones = np.zeros((MLA_HEADS, V_ROWS, 1), np.float32)
    ones[:, V_DIM] = 1.0
    ones = jnp.asarray(ones.reshape(MLA_HEADS * V_ROWS, 1))

    z, gates, qt, kp, vt3 = _in_proj(
        h, row(norm_mix_g[0]), wi, row(q_norm_g[0]), row(kv_norm_g[0]), wqt, wk, wvt, ones, place,
        cos_t, sin_t, cs)

    k3 = kp.reshape(t // KV_CHUNK, KV_CHUNK, MLA_HEADS * HEAD_PAD)
    o = _attention(qt, k3, vt3, batch, seq)

    xw = X_HEADS * X_HEAD_DIM
    wxkv = w_xkv[0].astype(BF16)
    mkt, mv = _mem_kv(mem.reshape(batch * mem_len, D_MODEL), row(norm_mem_g[0]),
                      wxkv[:, :xw].T, wxkv[:, xw:], batch, mem_len)

    cw = jnp.pad(conv_w[0], ((0, HALO - CONV_WIDTH), (0, 0)))
    h = _mixer(h, z, gates, o, cw, row(conv_b[0]), row(conv_ln_g[0]), row(conv_ln_b[0]),
               w_conv_out[0].astype(BF16), w_mla_out[0].astype(BF16), w_out[0].astype(BF16),
               row(norm_xattn_g[0]), w_xq[0].astype(BF16), mkt, mv, w_xo[0].astype(BF16), seq)
    h = _mlp(h, row(norm_mlp_g[0]), w_mlp1[0].astype(BF16), w_mlp2[0].astype(BF16),
             row(final_norm_g))
    return h.reshape(batch, seq, D_MODEL)
```

```python
import functools

import numpy as np
import jax
import jax.numpy as jnp
from jax import lax
from jax.experimental import pallas as pl
from jax.experimental.pallas import tpu as pltpu

D_MODEL = 1024
CONV_CH = D_MODEL // 2
CONV_WIDTH = 31
MLA_HEADS = 8
QK_NOPE = D_MODEL // 16
QK_ROPE = D_MODEL // 32
V_DIM = D_MODEL // 16
Q_LORA = 3 * D_MODEL // 8
KV_LORA = D_MODEL // 4
X_HEADS = 4
X_HEAD_DIM = D_MODEL // 8
D_FF = 4 * D_MODEL
ROPE_THETA = 10000.0
EPS = 1e-6

HEAD_PAD = 128
V_ROWS = V_DIM + 16
ATTN_HEADS = 4
ROPE_HALF = QK_ROPE // 2
SUBLANES = 8
BF16_ROWS = 2 * SUBLANES
HALO = 32
TOKEN_TILE = 1024
CONV_CHUNK = 256
MLP_TILE = 1024
Q_TILE = 512
KV_CHUNK = 512
ROPE_TILE = 2048
LOG2E = 1.4426950408889634


def _column_ranges(widths):
    out, start = {}, 0
    for name, width in widths:
        out[name] = (start, start + width)
        start += width
    return out


IN_COLS = _column_ranges([("glu_a", CONV_CH), ("glu_g", CONV_CH), ("gates", 2 * D_MODEL),
                          ("c_q", Q_LORA), ("c_kv", KV_LORA), ("k_rope", HEAD_PAD)])
MASK_VALUE = -1e30
VMEM_LIMIT = 56 * 1024 * 1024

F32 = jnp.float32
BF16 = jnp.bfloat16


def _rms(x, g):
    return x * lax.rsqrt(jnp.mean(x * x, axis=-1, keepdims=True) + EPS) * g


def _sigmoid(x):
    return 1.0 / (1.0 + jnp.exp(-x))


def _dot(a, b):
    return jnp.dot(a, b, preferred_element_type=F32)


def _dot_nt(a, b):
    return lax.dot_general(a, b, (((1,), (1,)), ((), ())), preferred_element_type=F32)


def _rope_kernel(pos_ref, invf_ref, cos_ref, sin_ref, cs_ref):
    ang = pos_ref[...].astype(F32) * invf_ref[...]
    c = jnp.cos(ang)
    s = jnp.sin(ang)
    cos_ref[...] = c
    sin_ref[...] = s
    pad = jnp.zeros((HEAD_PAD - 2 * QK_ROPE, ang.shape[1]), F32)
    cs_ref[...] = jnp.concatenate([c, c, -s, s, pad], axis=0).T


def _rope_tables(pos_row, inv_freq_col):
    t = pos_row.shape[1]
    out = jax.ShapeDtypeStruct((ROPE_HALF, t), F32)
    spec = pl.BlockSpec((ROPE_HALF, ROPE_TILE), lambda i: (0, i))
    return pl.pallas_call(
        _rope_kernel,
        grid=(t // ROPE_TILE,),
        in_specs=[pl.BlockSpec((1, ROPE_TILE), lambda i: (0, i)),
                  pl.BlockSpec((ROPE_HALF, 1), lambda i: (0, 0))],
        out_specs=[spec, spec, pl.BlockSpec((ROPE_TILE, HEAD_PAD), lambda i: (i, 0))],
        out_shape=[out, out, jax.ShapeDtypeStruct((t, HEAD_PAD), F32)],
        name="rope_tables",
    )(pos_row, inv_freq_col)


def _regroup_kernel(w_ref, o_ref):
    c0 = 2 * CONV_CH
    c2 = c0 + Q_LORA + KV_LORA
    c3 = c2 + QK_ROPE

    def put(name, value):
        o_ref[IN_COLS[name][0]:IN_COLS[name][0] + value.shape[0], :] = value.astype(BF16)

    put("glu_a", w_ref[0, :CONV_CH, :])
    put("glu_g", w_ref[0, CONV_CH:c0, :])
    put("gates", w_ref[0, c3:, :])
    put("c_q", w_ref[0, c0:c0 + Q_LORA, :])
    put("c_kv", w_ref[0, c0 + Q_LORA:c2, :])
    kr = w_ref[0, c2:c3, :]
    zeros = jnp.zeros((HEAD_PAD - 2 * QK_ROPE, kr.shape[1]), F32)
    put("k_rope", jnp.concatenate([kr, kr[ROPE_HALF:], kr[:ROPE_HALF], zeros], axis=0))


def _regroup_w_in(wt):
    lanes = 256
    rows = IN_COLS["k_rope"][1]
    return pl.pallas_call(
        _regroup_kernel,
        grid=(wt.shape[2] // lanes,),
        in_specs=[pl.BlockSpec((1, wt.shape[1], lanes), lambda i: (0, 0, i))],
        out_specs=pl.BlockSpec((rows, lanes), lambda i: (0, i)),
        out_shape=jax.ShapeDtypeStruct((rows, wt.shape[2]), BF16),
        name="regroup_w_in",
    )(wt)


def _in_proj_kernel(x_ref, g_ref, w_ref, qg_ref, kvg_ref, wqt_ref, wk_ref, wvt_ref, ones_ref, place_ref,
                    cos_ref, sin_ref, cs_ref,
                    z_ref, gates_ref, qt_ref, k_ref, vt_ref):
    u = _rms(x_ref[...], g_ref[...]).astype(BF16)
    proj = lambda name: _dot_nt(u, w_ref[IN_COLS[name][0]:IN_COLS[name][1], :])
    z_ref[...] = (proj("glu_a") * _sigmoid(proj("glu_g"))).astype(BF16)
    gates_ref[...] = _sigmoid(proj("gates")).astype(BF16)

    cqn = _rms(proj("c_q"), qg_ref[...]).astype(BF16)
    qt = _dot_nt(wqt_ref[...], cqn)
    scale = (QK_NOPE + QK_ROPE) ** -0.5 * LOG2E
    c = cos_ref[...]
    s = sin_ref[...]
    for h in range(MLA_HEADS):
        b = h * HEAD_PAD
        r1 = b + QK_NOPE
        r2 = r1 + ROPE_HALF
        r3 = r2 + ROPE_HALF
        t1 = qt[r1:r2]
        t2 = qt[r2:r3]
        qt_ref[b:r1, :] = (qt[b:r1] * scale).astype(BF16)
        qt_ref[r1:r2, :] = ((t1 * c - t2 * s) * scale).astype(BF16)
        qt_ref[r2:r3, :] = ((t2 * c + t1 * s) * scale).astype(BF16)
        qt_ref[r3:b + HEAD_PAD, :] = (qt[r3:b + HEAD_PAD] * scale).astype(BF16)

    ckvn = _rms(proj("c_kv"), kvg_ref[...]).astype(BF16)
    t = proj("k_rope") * cs_ref[...]
    rot = t + pltpu.roll(t, HEAD_PAD - QK_ROPE, 1)
    k_ref[...] = (_dot(ckvn, wk_ref[...]) + _dot(rot.astype(BF16), place_ref[...])).astype(BF16)
    vt_ref[0] = (_dot_nt(wvt_ref[...], ckvn) + ones_ref[...]).astype(BF16)


def _in_proj(x2, g, w, qg, kvg, wqt, wk, wvt, ones, place, cos_t, sin_t, cs):
    t = x2.shape[0]
    tm = KV_CHUNK
    n = t // tm
    full = lambda arr: pl.BlockSpec(arr.shape, lambda i: (0,) * arr.ndim)
    row = lambda w: pl.BlockSpec((tm, w), lambda i: (i, 0))
    in_specs = [row(D_MODEL), full(g), full(w), full(qg), full(kvg), full(wqt), full(wk), full(wvt),
                full(ones), full(place),
                pl.BlockSpec((ROPE_HALF, tm), lambda i: (0, i)),
                pl.BlockSpec((ROPE_HALF, tm), lambda i: (0, i)),
                row(HEAD_PAD)]
    out_shape = [jax.ShapeDtypeStruct((t, CONV_CH), BF16),
                 jax.ShapeDtypeStruct((t, 2 * D_MODEL), BF16),
                 jax.ShapeDtypeStruct((MLA_HEADS * HEAD_PAD, t), BF16),
                 jax.ShapeDtypeStruct((t, MLA_HEADS * HEAD_PAD), BF16),
                 jax.ShapeDtypeStruct((n, MLA_HEADS * V_ROWS, tm), BF16)]
    out_specs = [row(CONV_CH), row(2 * D_MODEL),
                 pl.BlockSpec((MLA_HEADS * HEAD_PAD, tm), lambda i: (0, i)),
                 row(MLA_HEADS * HEAD_PAD),
                 pl.BlockSpec((1, MLA_HEADS * V_ROWS, tm), lambda i: (i, 0, 0))]
    return pl.pallas_call(
        _in_proj_kernel,
        grid=(n,),
        in_specs=in_specs,
        out_specs=out_specs,
        out_shape=out_shape,
        compiler_params=pltpu.CompilerParams(dimension_semantics=("parallel",),
                                             vmem_limit_bytes=VMEM_LIMIT),
        name="in_proj",
    )(x2, g, w, qg, kvg, wqt, wk, wvt, ones, place, cos_t, sin_t, cs)


def _attn_kernel(qt_ref, k_ref, vt_ref, o_ref, s_scr, acc_scr):
    qi = pl.program_id(2)
    tk = k_ref.shape[1]
    tq = qt_ref.shape[1]
    heads = range(ATTN_HEADS)
    n_diag = tq // tk
    assert tq == n_diag * tk and n_diag in (1, 2)
    diag = qi * n_diag
    n_items = n_diag + diag
    key_idx = lax.broadcasted_iota(jnp.int32, (tk, tq), 0)
    qry_idx = lax.broadcasted_iota(jnp.int32, (tk, tq), 1)
    causal = key_idx <= qry_idx
    square = n_diag == 1
    half = tk // 2
    qts = [qt_ref[h * HEAD_PAD:(h + 1) * HEAD_PAD, :] for h in heads]

    def scores_to(slot, kc, diag_part, hs=heads):
        ms = []
        for h in hs:
            k = k_ref[kc, :, h * HEAD_PAD:(h + 1) * HEAD_PAD]
            if diag_part is None:
                s = _dot(k, qts[h])
                s_scr[slot, h] = s
                ms.append(jnp.max(s, axis=0, keepdims=True))
            elif square:
                top = jnp.where(causal[:half], _dot(k[:half], qts[h]), MASK_VALUE)
                low = jnp.where(causal[half:, half:], _dot(k[half:], qts[h][:, half:]), MASK_VALUE)
                s_scr[slot, h, :half, :] = top
                s_scr[slot, h, half:, half:] = low
                s_scr[slot, h, half:, :half] = jnp.full((tk - half, half), MASK_VALUE, F32)
                low_max = jnp.concatenate([jnp.full((1, half), MASK_VALUE, F32),
                                           jnp.max(low, axis=0, keepdims=True)], axis=1)
                ms.append(jnp.maximum(jnp.max(top, axis=0, keepdims=True), low_max))
            else:
                s = jnp.where(key_idx + diag_part * tk <= qry_idx, _dot(k, qts[h]), MASK_VALUE)
                s_scr[slot, h] = s
                ms.append(jnp.max(s, axis=0, keepdims=True))
        return tuple(ms)

    def accumulate(m_run, slot, kc, ms, hs=heads):
        new = []
        for h, m_old, m_chunk in zip(hs, m_run, ms):
            p = jnp.concatenate([jnp.exp2(s_scr[slot, h, r:r + BF16_ROWS, :] - m_chunk).astype(BF16)
                                 for r in range(0, tk, BF16_ROWS)], axis=0)
            pv = _dot(vt_ref[kc, h * V_ROWS:(h + 1) * V_ROWS, :], p)
            m_new = jnp.maximum(m_old, m_chunk)
            a = jnp.exp2(m_old - m_new)
            b = jnp.exp2(m_chunk - m_new)
            acc_scr[h] = a * acc_scr[h] + b * pv
            new.append(m_new)
        return tuple(new)

    def step(cur, look_ahead, j, carry):
        m_run, kc, ms = carry
        ahead_kc = j + 1 - n_diag
        new_run, ahead_m = [], []
        for h in heads:
            if look_ahead:
                ahead_m += scores_to(1 - cur, ahead_kc, None, (h,))
            new_run += accumulate(m_run[h:h + 1], cur, kc, ms[h:h + 1], (h,))
        return tuple(new_run), ahead_kc, (tuple(ahead_m) if look_ahead else ms)

    branches = [functools.partial(step, cur, look_ahead) for look_ahead in (True, False) for cur in (0, 1)]

    def body(j, carry):
        return lax.switch(2 * (j == n_items - 1).astype(jnp.int32) + j % 2, branches, j, carry)

    acc_scr[...] = jnp.zeros(acc_scr.shape, F32)
    m_init = tuple(jnp.full((1, tq), MASK_VALUE, F32) for _ in heads)
    assert n_diag == 1
    first_m = scores_to(0, diag, 0)
    lax.fori_loop(0, n_items, body, (m_init, diag, first_m))
    outs = [acc_scr[h, :V_DIM, :] / acc_scr[h, V_DIM:V_DIM + 1, :] for h in heads]
    o_ref[...] = jnp.concatenate(outs, axis=0).astype(BF16)


def _attention(qt, k3, vt3, batch, seq):
    tq = Q_TILE
    nq = seq // tq
    nk = seq // KV_CHUNK
    t = batch * seq
    return pl.pallas_call(
        _attn_kernel,
        grid=(batch, MLA_HEADS // ATTN_HEADS, nq),
        in_specs=[pl.BlockSpec((ATTN_HEADS * HEAD_PAD, tq), lambda b, hg, qi: (hg, b * nq + qi)),
                  pl.BlockSpec((nk, KV_CHUNK, ATTN_HEADS * HEAD_PAD), lambda b, hg, qi: (b, 0, hg)),
                  pl.BlockSpec((nk, ATTN_HEADS * V_ROWS, KV_CHUNK), lambda b, hg, qi: (b, hg, 0))],
        out_specs=pl.BlockSpec((ATTN_HEADS * V_DIM, tq), lambda b, hg, qi: (hg, b * nq + qi)),
        out_shape=jax.ShapeDtypeStruct((MLA_HEADS * V_DIM, t), BF16),
        scratch_shapes=[pltpu.VMEM((2, ATTN_HEADS, KV_CHUNK, tq), F32),
                        pltpu.VMEM((ATTN_HEADS, V_ROWS, tq), F32)],
        compiler_params=pltpu.CompilerParams(
            dimension_semantics=("parallel", "parallel", "arbitrary"),
            vmem_limit_bytes=VMEM_LIMIT),
        name="mla_attention",
    )(qt, k3, vt3)


def _mem_kernel(mem_ref, g_ref, wkt_ref, wv_ref, kt_ref, v_ref):
    mn = _rms(mem_ref[...], g_ref[...]).astype(BF16)
    kt_ref[0] = _dot_nt(wkt_ref[...], mn).astype(BF16)
    v_ref[0] = _dot(mn, wv_ref[...]).astype(BF16)


def _mem_kv(mem2, g, wkt, wv, batch, mem_len):
    xw = X_HEADS * X_HEAD_DIM
    full = lambda arr: pl.BlockSpec(arr.shape, lambda b: (0,) * arr.ndim)
    return pl.pallas_call(
        _mem_kernel,
        grid=(batch,),
        in_specs=[pl.BlockSpec((mem_len, D_MODEL), lambda b: (b, 0)), full(g), full(wkt), full(wv)],
        out_specs=[pl.BlockSpec((1, xw, mem_len), lambda b: (b, 0, 0)),
                   pl.BlockSpec((1, mem_len, xw), lambda b: (b, 0, 0))],
        out_shape=[jax.ShapeDtypeStruct((batch, xw, mem_len), BF16),
                   jax.ShapeDtypeStruct((batch, mem_len, xw), BF16)],
        compiler_params=pltpu.CompilerParams(dimension_semantics=("parallel",),
                                             vmem_limit_bytes=VMEM_LIMIT),
        name="mem_kv",
    )(mem2, g, wkt, wv)


def _mixer_kernel(tiles_per_seq, x_ref, z_ref, halo_ref, gates_ref, o_ref,
                  cw_ref, cb_ref, lng_ref, lnb_ref, wco_ref, wmo_ref, wout_ref,
                  xg_ref, wxq_ref, mkt_ref, mv_ref, wxo_ref,
                  h_ref, zext_ref, shift_ref, conv_ref):
    tm = x_ref.shape[0]
    first = (pl.program_id(0) % tiles_per_seq) == 0
    zext_ref[0:HALO, :] = jnp.where(first, 0.0, halo_ref[...].astype(F32))
    zext_ref[HALO:, :] = z_ref[...].astype(F32)

    rows = 64
    off = HALO - (CONV_WIDTH - 1)
    win = CONV_CHUNK + HALO
    for r0 in range(0, tm, CONV_CHUNK):
        for s in range(1, SUBLANES):
            shift_ref[s - 1, 0:win - SUBLANES, :] = zext_ref[r0 + s:r0 + s + win - SUBLANES, :]
        for c in range(CONV_CH // 128):
            cs = slice(c * 128, (c + 1) * 128)
            for r in range(r0, r0 + CONV_CHUNK, rows):
                acc = jnp.broadcast_to(cb_ref[:, cs], (rows, 128))
                for w in range(CONV_WIDTH):
                    shift, base = (off + w) % SUBLANES, (off + w) // SUBLANES * SUBLANES
                    if shift == 0:
                        window = zext_ref[r + base:r + base + rows, cs]
                    else:
                        window = shift_ref[shift - 1, r - r0 + base:r - r0 + base + rows, cs]
                    acc = acc + window * cw_ref[w:w + 1, cs]
                conv_ref[r:r + rows, cs] = acc

    y = conv_ref[...]
    mu = jnp.mean(y, axis=-1, keepdims=True)
    yc = y - mu
    var = jnp.mean(yc * yc, axis=-1, keepdims=True)
    y = yc * lax.rsqrt(var + EPS) * lng_ref[...] + lnb_ref[...]
    y = y * _sigmoid(y)
    conv_out = _dot(y.astype(BF16), wco_ref[...])
    mla_out = lax.dot_general(o_ref[...], wmo_ref[...], (((0,), (0,)), ((), ())),
                              preferred_element_type=F32)
    gates = gates_ref[...].astype(F32)
    merged = gates[:, :D_MODEL] * conv_out + gates[:, D_MODEL:] * mla_out
    h1 = x_ref[...] + _dot(merged.astype(BF16), wout_ref[...])

    u = _rms(h1, xg_ref[...]).astype(BF16)
    q = (_dot(u, wxq_ref[...]) * (X_HEAD_DIM ** -0.5)).astype(BF16)
    heads = []
    for h in range(X_HEADS):
        hs = slice(h * X_HEAD_DIM, (h + 1) * X_HEAD_DIM)
        s = _dot(q[:, hs], mkt_ref[0, hs, :])
        p = jnp.exp(s - jnp.max(s, axis=-1, keepdims=True))
        p = p / jnp.sum(p, axis=-1, keepdims=True)
        heads.append(_dot(p.astype(BF16), mv_ref[0, :, hs]))
    xo = jnp.concatenate(heads, axis=1).astype(BF16)
    h_ref[...] = h1 + _dot(xo, wxo_ref[...])


def _mixer(x2, z, gates, o, cw, cb, lng, lnb, wco, wmo, wout, xg, wxq, mkt, mv, wxo, seq):
    t = x2.shape[0]
    tm = TOKEN_TILE
    tiles_per_seq = seq // tm
    halo_per_tile = tm // HALO
    mem_len = mv.shape[1]
    xw = X_HEADS * X_HEAD_DIM
    full = lambda arr: pl.BlockSpec(arr.shape, lambda i: (0,) * arr.ndim)
    row = lambda w: pl.BlockSpec((tm, w), lambda i: (i, 0))
    in_specs = [row(D_MODEL), row(CONV_CH),
                pl.BlockSpec((HALO, CONV_CH), lambda i: (jnp.maximum(i * halo_per_tile - 1, 0), 0)),
                row(2 * D_MODEL), pl.BlockSpec((MLA_HEADS * V_DIM, tm), lambda i: (0, i)),
                full(cw), full(cb), full(lng), full(lnb), full(wco), full(wmo), full(wout),
                full(xg), full(wxq),
                pl.BlockSpec((1, xw, mem_len), lambda i: (i // tiles_per_seq, 0, 0)),
                pl.BlockSpec((1, mem_len, xw), lambda i: (i // tiles_per_seq, 0, 0)),
                full(wxo)]
    return pl.pallas_call(
        functools.partial(_mixer_kernel, tiles_per_seq),
        grid=(t // tm,),
        in_specs=in_specs,
        out_specs=row(D_MODEL),
        out_shape=jax.ShapeDtypeStruct((t, D_MODEL), F32),
        scratch_shapes=[pltpu.VMEM((tm + HALO, CONV_CH), F32),
                        pltpu.VMEM((SUBLANES - 1, CONV_CHUNK + HALO, CONV_CH), F32),
                        pltpu.VMEM((tm, CONV_CH), F32)],
        compiler_params=pltpu.CompilerParams(dimension_semantics=("parallel",),
                                             vmem_limit_bytes=VMEM_LIMIT),
        name="mixer",
    )(x2, z, z, gates, o, cw, cb, lng, lnb, wco, wmo, wout, xg, wxq, mkt, mv, wxo)


def _mlp_kernel(h_ref, g_ref, w1_ref, w2_ref, fg_ref, y_ref):
    h = h_ref[...]
    u = _rms(h, g_ref[...]).astype(BF16)
    chunk = 1024
    acc = h
    for c in range(0, D_FF, chunk):
        a = jnp.maximum(_dot(u, w1_ref[:, c:c + chunk]), 0.0)
        acc = acc + _dot((a * a).astype(BF16), w2_ref[c:c + chunk, :])
    y_ref[...] = _rms(acc, fg_ref[...])


def _mlp(h, g, w1, w2, fg):
    t = h.shape[0]
    tm = MLP_TILE
    full = lambda arr: pl.BlockSpec(arr.shape, lambda i: (0,) * arr.ndim, pipeline_mode=pl.Buffered(1))
    row = pl.BlockSpec((tm, D_MODEL), lambda i: (i, 0))
    return pl.pallas_call(
        _mlp_kernel,
        grid=(t // tm,),
        in_specs=[row, full(g), full(w1), full(w2), full(fg)],
        out_specs=row,
        out_shape=jax.ShapeDtypeStruct((t, D_MODEL), F32),
        compiler_params=pltpu.CompilerParams(dimension_semantics=("parallel",),
                                             vmem_limit_bytes=VMEM_LIMIT),
        name="mlp",
    )(h, g, w1, w2, fg)


def _rope_placement():
    e = np.zeros((HEAD_PAD, MLA_HEADS * HEAD_PAD), np.float32)
    for h in range(MLA_HEADS):
        for j in range(QK_ROPE):
            e[j, h * HEAD_PAD + QK_NOPE + j] = 1.0
    return jnp.asarray(e, BF16)


def kernel(x, mem, positions, norm_mix_g, w_in, conv_w, conv_b, conv_ln_g, conv_ln_b, w_conv_out, q_norm_g, w_uq, kv_norm_g, w_ukv, w_mla_out, w_out, norm_xattn_g, norm_mem_g, w_xq, w_xkv, w_xo, norm_mlp_g, w_mlp1, w_mlp2, final_norm_g):
    batch, seq, _ = x.shape
    mem_len = mem.shape[1]
    t = batch * seq
    assert w_in.shape[0] == 1, "single-layer block (the final rmsnorm is fused into the MLP kernel)"
    assert seq % KV_CHUNK == 0 and seq % TOKEN_TILE == 0 and t % ROPE_TILE == 0 and t % MLP_TILE == 0

    inv_freq = ROPE_THETA ** (-jnp.arange(ROPE_HALF, dtype=F32) / ROPE_HALF)
    cos_t, sin_t, cs = _rope_tables(positions.reshape(1, t), inv_freq.reshape(ROPE_HALF, 1))
    place = _rope_placement()
    row = lambda v: v.reshape(1, -1)

    h = x.reshape(t, D_MODEL)
    wi = _regroup_w_in(jnp.swapaxes(w_in, 1, 2))
    wuq = w_uq[0].astype(BF16).reshape(Q_LORA, MLA_HEADS, QK_NOPE + QK_ROPE)
    wqt = jnp.pad(wuq, ((0, 0), (0, 0), (0, HEAD_PAD - QK_NOPE - QK_ROPE))).reshape(Q_LORA, -1).T
    wukv = w_ukv[0].astype(BF16).reshape(KV_LORA, MLA_HEADS, QK_NOPE + V_DIM)
    wk = jnp.pad(wukv[:, :, :QK_NOPE], ((0, 0), (0, 0), (0, HEAD_PAD - QK_NOPE))).reshape(KV_LORA, -1)
    wvt = jnp.pad(wukv[:, :, QK_NOPE:], ((0, 0), (0, 0), (0, V_ROWS - V_DIM))).reshape(KV_LORA, -1).T
    ones = np.zeros((MLA_HEADS, V_ROWS, 1), np.float32)
    ones[:, V_DIM] = 1.0
    ones = jnp.asarray(ones.reshape(MLA_HEADS * V_ROWS, 1))

    z, gates, qt, kp, vt3 = _in_proj(
        h, row(norm_mix_g[0]), wi, row(q_norm_g[0]), row(kv_norm_g[0]), wqt, wk, wvt, ones, place,
        cos_t, sin_t, cs)

    k3 = kp.reshape(t // KV_CHUNK, KV_CHUNK, MLA_HEADS * HEAD_PAD)
    o = _attention(qt, k3, vt3, batch, seq)

    xw = X_HEADS * X_HEAD_DIM
    wxkv = w_xkv[0].astype(BF16)
    mkt, mv = _mem_kv(mem.reshape(batch * mem_len, D_MODEL), row(norm_mem_g[0]),
                      wxkv[:, :xw].T, wxkv[:, xw:], batch, mem_len)

    cw = jnp.pad(conv_w[0], ((0, HALO - CONV_WIDTH), (0, 0)))
    h = _mixer(h, z, gates, o, cw, row(conv_b[0]), row(conv_ln_g[0]), row(conv_ln_b[0]),
               w_conv_out[0].astype(BF16), w_mla_out[0].astype(BF16), w_out[0].astype(BF16),
               row(norm_xattn_g[0]), w_xq[0].astype(BF16), mkt, mv, w_xo[0].astype(BF16), seq)
    h = _mlp(h, row(norm_mlp_g[0]), w_mlp1[0].astype(BF16), w_mlp2[0].astype(BF16),
             row(final_norm_g))
    return h.reshape(batch, seq, D_MODEL)
```

```python
import functools

import numpy as np
import jax
import jax.numpy as jnp
from jax import lax
from jax.experimental import pallas as pl
from jax.experimental.pallas import tpu as pltpu

D_MODEL = 1024
CONV_CH = D_MODEL // 2
CONV_WIDTH = 31
MLA_HEADS = 8
QK_NOPE = D_MODEL // 16
QK_ROPE = D_MODEL // 32
V_DIM = D_MODEL // 16
Q_LORA = 3 * D_MODEL // 8
KV_LORA = D_MODEL // 4
X_HEADS = 4
X_HEAD_DIM = D_MODEL // 8
D_FF = 4 * D_MODEL
ROPE_THETA = 10000.0
EPS = 1e-6

HEAD_PAD = 128
V_ROWS = V_DIM + 16
ATTN_HEADS = 8
ROPE_HALF = QK_ROPE // 2
SUBLANES = 8
BF16_ROWS = 2 * SUBLANES
HALO = 32
TOKEN_TILE = 1024
CONV_CHUNK = 256
MLP_TILE = 1024
KV_CHUNK = 512
Q_TILE = KV_CHUNK
ROPE_TILE = 2048
LOG2E = 1.4426950408889634


def _column_ranges(widths):
    out, start = {}, 0
    for name, width in widths:
        out[name] = (start, start + width)
        start += width
    return out


IN_COLS = _column_ranges([("glu_a", CONV_CH), ("glu_g", CONV_CH), ("gates", 2 * D_MODEL),
                          ("c_q", Q_LORA), ("c_kv", KV_LORA), ("k_rope", HEAD_PAD)])
MASK_VALUE = -1e30
VMEM_LIMIT = 56 * 1024 * 1024

F32 = jnp.float32
BF16 = jnp.bfloat16


def _rms(x, g):
    return x * lax.rsqrt(jnp.mean(x * x, axis=-1, keepdims=True) + EPS) * g


def _sigmoid(x):
    return 1.0 / (1.0 + jnp.exp(-x))


def _dot(a, b):
    return jnp.dot(a, b, preferred_element_type=F32)


def _dot_nt(a, b):
    return lax.dot_general(a, b, (((1,), (1,)), ((), ())), preferred_element_type=F32)


def _rope_kernel(pos_ref, invf_ref, cos_ref, sin_ref, cs_ref):
    ang = pos_ref[...].astype(F32) * invf_ref[...]
    c = jnp.cos(ang)
    s = jnp.sin(ang)
    cos_ref[...] = c
    sin_ref[...] = s
    pad = jnp.zeros((HEAD_PAD - 2 * QK_ROPE, ang.shape[1]), F32)
    cs_ref[...] = jnp.concatenate([c, c, -s, s, pad], axis=0).T


def _rope_tables(pos_row, inv_freq_col):
    t = pos_row.shape[1]
    out = jax.ShapeDtypeStruct((ROPE_HALF, t), F32)
    spec = pl.BlockSpec((ROPE_HALF, ROPE_TILE), lambda i: (0, i))
    return pl.pallas_call(
        _rope_kernel,
        grid=(t // ROPE_TILE,),
        in_specs=[pl.BlockSpec((1, ROPE_TILE), lambda i: (0, i)),
                  pl.BlockSpec((ROPE_HALF, 1), lambda i: (0, 0))],
        out_specs=[spec, spec, pl.BlockSpec((ROPE_TILE, HEAD_PAD), lambda i: (i, 0))],
        out_shape=[out, out, jax.ShapeDtypeStruct((t, HEAD_PAD), F32)],
        name="rope_tables",
    )(pos_row, inv_freq_col)


def _regroup_kernel(w_ref, o_ref):
    c0 = 2 * CONV_CH
    c2 = c0 + Q_LORA + KV_LORA
    c3 = c2 + QK_ROPE

    def put(name, value):
        o_ref[IN_COLS[name][0]:IN_COLS[name][0] + value.shape[0], :] = value.astype(BF16)

    put("glu_a", w_ref[0, :CONV_CH, :])
    put("glu_g", w_ref[0, CONV_CH:c0, :])
    put("gates", w_ref[0, c3:, :])
    put("c_q", w_ref[0, c0:c0 + Q_LORA, :])
    put("c_kv", w_ref[0, c0 + Q_LORA:c2, :])
    kr = w_ref[0, c2:c3, :]
    zeros = jnp.zeros((HEAD_PAD - 2 * QK_ROPE, kr.shape[1]), F32)
    put("k_rope", jnp.concatenate([kr, kr[ROPE_HALF:], kr[:ROPE_HALF], zeros], axis=0))


def _regroup_w_in(wt):
    lanes = 256
    rows = IN_COLS["k_rope"][1]
    return pl.pallas_call(
        _regroup_kernel,
        grid=(wt.shape[2] // lanes,),
        in_specs=[pl.BlockSpec((1, wt.shape[1], lanes), lambda i: (0, 0, i))],
        out_specs=pl.BlockSpec((rows, lanes), lambda i: (0, i)),
        out_shape=jax.ShapeDtypeStruct((rows, wt.shape[2]), BF16),
        name="regroup_w_in",
    )(wt)


def _in_proj_kernel(x_ref, g_ref, w_ref, qg_ref, kvg_ref, wqt_ref, wk_ref, wvt_ref, ones_ref, place_ref,
                    cos_ref, sin_ref, cs_ref,
                    z_ref, gates_ref, qt_ref, k_ref, vt_ref):
    u = _rms(x_ref[...], g_ref[...]).astype(BF16)
    proj = lambda name: _dot_nt(u, w_ref[IN_COLS[name][0]:IN_COLS[name][1], :])
    z_ref[...] = (proj("glu_a") * _sigmoid(proj("glu_g"))).astype(BF16)
    gates_ref[...] = _sigmoid(proj("gates")).astype(BF16)

    cqn = _rms(proj("c_q"), qg_ref[...]).astype(BF16)
    qt = _dot_nt(wqt_ref[...], cqn)
    scale = (QK_NOPE + QK_ROPE) ** -0.5 * LOG2E
    c = cos_ref[...]
    s = sin_ref[...]
    for h in range(MLA_HEADS):
        b = h * HEAD_PAD
        r1 = b + QK_NOPE
        r2 = r1 + ROPE_HALF
        r3 = r2 + ROPE_HALF
        t1 = qt[r1:r2]
        t2 = qt[r2:r3]
        qt_ref[b:r1, :] = (qt[b:r1] * scale).astype(BF16)
        qt_ref[r1:r2, :] = ((t1 * c - t2 * s) * scale).astype(BF16)
        qt_ref[r2:r3, :] = ((t2 * c + t1 * s) * scale).astype(BF16)
        qt_ref[r3:b + HEAD_PAD, :] = (qt[r3:b + HEAD_PAD] * scale).astype(BF16)

    ckvn = _rms(proj("c_kv"), kvg_ref[...]).astype(BF16)
    t = proj("k_rope") * cs_ref[...]
    rot = t + pltpu.roll(t, HEAD_PAD - QK_ROPE, 1)
    k_ref[...] = (_dot(ckvn, wk_ref[...]) + _dot(rot.astype(BF16), place_ref[...])).astype(BF16)
    vt_ref[0] = (_dot_nt(wvt_ref[...], ckvn) + ones_ref[...]).astype(BF16)


def _in_proj(x2, g, w, qg, kvg, wqt, wk, wvt, ones, place, cos_t, sin_t, cs):
    t = x2.shape[0]
    tm = KV_CHUNK
    n = t // tm
    full = lambda arr: pl.BlockSpec(arr.shape, lambda i: (0,) * arr.ndim)
    row = lambda w: pl.BlockSpec((tm, w), lambda i: (i, 0))
    in_specs = [row(D_MODEL), full(g), full(w), full(qg), full(kvg), full(wqt), full(wk), full(wvt),
                full(ones), full(place),
                pl.BlockSpec((ROPE_HALF, tm), lambda i: (0, i)),
                pl.BlockSpec((ROPE_HALF, tm), lambda i: (0, i)),
                row(HEAD_PAD)]
    out_shape = [jax.ShapeDtypeStruct((t, CONV_CH), BF16),
                 jax.ShapeDtypeStruct((t, 2 * D_MODEL), BF16),
                 jax.ShapeDtypeStruct((MLA_HEADS * HEAD_PAD, t), BF16),
                 jax.ShapeDtypeStruct((t, MLA_HEADS * HEAD_PAD), BF16),
                 jax.ShapeDtypeStruct((n, MLA_HEADS * V_ROWS, tm), BF16)]
    out_specs = [row(CONV_CH), row(2 * D_MODEL),
                 pl.BlockSpec((MLA_HEADS * HEAD_PAD, tm), lambda i: (0, i)),
                 row(MLA_HEADS * HEAD_PAD),
                 pl.BlockSpec((1, MLA_HEADS * V_ROWS, tm), lambda i: (i, 0, 0))]
    return pl.pallas_call(
        _in_proj_kernel,
        grid=(n,),
        in_specs=in_specs,
        out_specs=out_specs,
        out_shape=out_shape,
        compiler_params=pltpu.CompilerParams(dimension_semantics=("parallel",),
                                             vmem_limit_bytes=VMEM_LIMIT),
        name="in_proj",
    )(x2, g, w, qg, kvg, wqt, wk, wvt, ones, place, cos_t, sin_t, cs)


def _attn_kernel(qt_ref, k_ref, vt_ref, o_ref, s_scr, acc_scr):
    qi = pl.program_id(2)
    tk = k_ref.shape[1]
    tq = qt_ref.shape[1]
    assert tq == tk
    heads = range(ATTN_HEADS)
    half = tk // 2
    n_items = qi + 1
    causal = (lax.broadcasted_iota(jnp.int32, (tk, tq), 0) <= lax.broadcasted_iota(jnp.int32, (tk, tq), 1))
    qts = [qt_ref[h * HEAD_PAD:(h + 1) * HEAD_PAD, :] for h in heads]

    def scores_to(slot, kc, diagonal, hs=heads):
        ms = []
        for h in hs:
            k = k_ref[kc, :, h * HEAD_PAD:(h + 1) * HEAD_PAD]
            if not diagonal:
                s = _dot(k, qts[h])
                s_scr[slot, h] = s
                ms.append(jnp.max(s, axis=0, keepdims=True))
            else:
                top = jnp.where(causal[:half], _dot(k[:half], qts[h]), MASK_VALUE)
                low = jnp.where(causal[half:, half:], _dot(k[half:], qts[h][:, half:]), MASK_VALUE)
                s_scr[slot, h, :half, :] = top
                s_scr[slot, h, half:, half:] = low
                s_scr[slot, h, half:, :half] = jnp.full((tk - half, half), MASK_VALUE, F32)
                low_max = jnp.concatenate([jnp.full((1, half), MASK_VALUE, F32),
                                           jnp.max(low, axis=0, keepdims=True)], axis=1)
                ms.append(jnp.maximum(jnp.max(top, axis=0, keepdims=True), low_max))
        return tuple(ms)

    def accumulate(m_run, slot, kc, ms, hs=heads):
        new = []
        for h, m_old, m_chunk in zip(hs, m_run, ms):
            p = jnp.concatenate([jnp.exp2(s_scr[slot, h, r:r + BF16_ROWS, :] - m_chunk).astype(BF16)
                                 for r in range(0, tk, BF16_ROWS)], axis=0)
            pv = _dot(vt_ref[kc, h * V_ROWS:(h + 1) * V_ROWS, :], p)
            m_new = jnp.maximum(m_old, m_chunk)
            a = jnp.exp2(m_old - m_new)
            b = jnp.exp2(m_chunk - m_new)
            acc_scr[h] = a * acc_scr[h] + b * pv
            new.append(m_new)
        return tuple(new)

    def step(cur, look_ahead, j, carry):
        m_run, kc, ms = carry
        new_run, ahead_m = [], []
        for h in heads:
            if look_ahead:
                ahead_m += scores_to(1 - cur, j, False, (h,))
            new_run += accumulate(m_run[h:h + 1], cur, kc, ms[h:h + 1], (h,))
        return tuple(new_run), j, (tuple(ahead_m) if look_ahead else ms)

    branches = [functools.partial(step, cur, look_ahead) for look_ahead in (True, False) for cur in (0, 1)]

    def body(j, carry):
        return lax.switch(2 * (j == n_items - 1).astype(jnp.int32) + j % 2, branches, j, carry)

    acc_scr[...] = jnp.zeros(acc_scr.shape, F32)
    m_init = tuple(jnp.full((1, tq), MASK_VALUE, F32) for _ in heads)
    lax.fori_loop(0, n_items, body, (m_init, qi, scores_to(0, qi, True)))
    outs = [acc_scr[h, :V_DIM, :] / acc_scr[h, V_DIM:V_DIM + 1, :] for h in heads]
    o_ref[...] = jnp.concatenate(outs, axis=0).astype(BF16)


def _attention(qt, k3, vt3, batch, seq):
    tq = Q_TILE
    nq = seq // tq
    nk = seq // KV_CHUNK
    t = batch * seq
    return pl.pallas_call(
        _attn_kernel,
        grid=(batch, MLA_HEADS // ATTN_HEADS, nq),
        in_specs=[pl.BlockSpec((ATTN_HEADS * HEAD_PAD, tq), lambda b, hg, qi: (hg, b * nq + qi)),
                  pl.BlockSpec((nk, KV_CHUNK, ATTN_HEADS * HEAD_PAD), lambda b, hg, qi: (b, 0, hg),
                               pipeline_mode=pl.Buffered(1)),
                  pl.BlockSpec((nk, ATTN_HEADS * V_ROWS, KV_CHUNK), lambda b, hg, qi: (b, hg, 0),
                               pipeline_mode=pl.Buffered(1))],
        out_specs=pl.BlockSpec((ATTN_HEADS * V_DIM, tq), lambda b, hg, qi: (hg, b * nq + qi)),
        out_shape=jax.ShapeDtypeStruct((MLA_HEADS * V_DIM, t), BF16),
        scratch_shapes=[pltpu.VMEM((2, ATTN_HEADS, KV_CHUNK, tq), F32),
                        pltpu.VMEM((ATTN_HEADS, V_ROWS, tq), F32)],
        compiler_params=pltpu.CompilerParams(
            dimension_semantics=("parallel", "parallel", "arbitrary"),
            vmem_limit_bytes=VMEM_LIMIT),
        name="mla_attention",
    )(qt, k3, vt3)


def _mem_kernel(mem_ref, g_ref, wkt_ref, wv_ref, kt_ref, v_ref):
    mn = _rms(mem_ref[...], g_ref[...]).astype(BF16)
    kt_ref[0] = _dot_nt(wkt_ref[...], mn).astype(BF16)
    v_ref[0] = _dot(mn, wv_ref[...]).astype(BF16)


def _mem_kv(mem2, g, wkt, wv, batch, mem_len):
    xw = X_HEADS * X_HEAD_DIM
    full = lambda arr: pl.BlockSpec(arr.shape, lambda b: (0,) * arr.ndim)
    return pl.pallas_call(
        _mem_kernel,
        grid=(batch,),
        in_specs=[pl.BlockSpec((mem_len, D_MODEL), lambda b: (b, 0)), full(g), full(wkt), full(wv)],
        out_specs=[pl.BlockSpec((1, xw, mem_len), lambda b: (b, 0, 0)),
                   pl.BlockSpec((1, mem_len, xw), lambda b: (b, 0, 0))],
        out_shape=[jax.ShapeDtypeStruct((batch, xw, mem_len), BF16),
                   jax.ShapeDtypeStruct((batch, mem_len, xw), BF16)],
        compiler_params=pltpu.CompilerParams(dimension_semantics=("parallel",),
                                             vmem_limit_bytes=VMEM_LIMIT),
        name="mem_kv",
    )(mem2, g, wkt, wv)


def _mixer_kernel(tiles_per_seq, x_ref, z_ref, halo_ref, gates_ref, o_ref,
                  cw_ref, cb_ref, lng_ref, lnb_ref, wco_ref, wmo_ref, wout_ref,
                  xg_ref, wxq_ref, mkt_ref, mv_ref, wxo_ref,
                  h_ref, zext_ref, shift_ref, conv_ref):
    tm = x_ref.shape[0]
    first = (pl.program_id(0) % tiles_per_seq) == 0
    zext_ref[0:HALO, :] = jnp.where(first, 0.0, halo_ref[...].astype(F32))
    zext_ref[HALO:, :] = z_ref[...].astype(F32)

    rows = 64
    off = HALO - (CONV_WIDTH - 1)
    win = CONV_CHUNK + HALO
    for r0 in range(0, tm, CONV_CHUNK):
        for s in range(1, SUBLANES):
            shift_ref[s - 1, 0:win - SUBLANES, :] = zext_ref[r0 + s:r0 + s + win - SUBLANES, :]
        for c in range(CONV_CH // 128):
            cs = slice(c * 128, (c + 1) * 128)
            for r in range(r0, r0 + CONV_CHUNK, rows):
                acc = jnp.broadcast_to(cb_ref[:, cs], (rows, 128))
                for w in range(CONV_WIDTH):
                    shift, base = (off + w) % SUBLANES, (off + w) // SUBLANES * SUBLANES
                    if shift == 0:
                        window = zext_ref[r + base:r + base + rows, cs]
                    else:
                        window = shift_ref[shift - 1, r - r0 + base:r - r0 + base + rows, cs]
                    acc = acc + window * cw_ref[w:w + 1, cs]
                conv_ref[r:r + rows, cs] = acc

    y = conv_ref[...]
    mu = jnp.mean(y, axis=-1, keepdims=True)
    yc = y - mu
    var = jnp.mean(yc * yc, axis=-1, keepdims=True)
    y = yc * lax.rsqrt(var + EPS) * lng_ref[...] + lnb_ref[...]
    y = y * _sigmoid(y)
    conv_out = _dot(y.astype(BF16), wco_ref[...])
    mla_out = lax.dot_general(o_ref[...], wmo_ref[...], (((0,), (0,)), ((), ())),
                              preferred_element_type=F32)
    gates = gates_ref[...].astype(F32)
    merged = gates[:, :D_MODEL] * conv_out + gates[:, D_MODEL:] * mla_out
    h1 = x_ref[...] + _dot(merged.astype(BF16), wout_ref[...])

    u = _rms(h1, xg_ref[...]).astype(BF16)
    q = (_dot(u, wxq_ref[...]) * (X_HEAD_DIM ** -0.5)).astype(BF16)
    heads = []
    for h in range(X_HEADS):
        hs = slice(h * X_HEAD_DIM, (h + 1) * X_HEAD_DIM)
        s = _dot(q[:, hs], mkt_ref[0, hs, :])
        p = jnp.exp(s - jnp.max(s, axis=-1, keepdims=True))
        p = p / jnp.sum(p, axis=-1, keepdims=True)
        heads.append(_dot(p.astype(BF16), mv_ref[0, :, hs]))
    xo = jnp.concatenate(heads, axis=1).astype(BF16)
    h_ref[...] = h1 + _dot(xo, wxo_ref[...])


def _mixer(x2, z, gates, o, cw, cb, lng, lnb, wco, wmo, wout, xg, wxq, mkt, mv, wxo, seq):
    t = x2.shape[0]
    tm = TOKEN_TILE
    tiles_per_seq = seq // tm
    halo_per_tile = tm // HALO
    mem_len = mv.shape[1]
    xw = X_HEADS * X_HEAD_DIM
    full = lambda arr: pl.BlockSpec(arr.shape, lambda i: (0,) * arr.ndim)
    row = lambda w: pl.BlockSpec((tm, w), lambda i: (i, 0))
    in_specs = [row(D_MODEL), row(CONV_CH),
                pl.BlockSpec((HALO, CONV_CH), lambda i: (jnp.maximum(i * halo_per_tile - 1, 0), 0)),
                row(2 * D_MODEL), pl.BlockSpec((MLA_HEADS * V_DIM, tm), lambda i: (0, i)),
                full(cw), full(cb), full(lng), full(lnb), full(wco), full(wmo), full(wout),
                full(xg), full(wxq),
                pl.BlockSpec((1, xw, mem_len), lambda i: (i // tiles_per_seq, 0, 0)),
                pl.BlockSpec((1, mem_len, xw), lambda i: (i // tiles_per_seq, 0, 0)),
                full(wxo)]
    return pl.pallas_call(
        functools.partial(_mixer_kernel, tiles_per_seq),
        grid=(t // tm,),
        in_specs=in_specs,
        out_specs=row(D_MODEL),
        out_shape=jax.ShapeDtypeStruct((t, D_MODEL), F32),
        scratch_shapes=[pltpu.VMEM((tm + HALO, CONV_CH), F32),
                        pltpu.VMEM((SUBLANES - 1, CONV_CHUNK + HALO, CONV_CH), F32),
                        pltpu.VMEM((tm, CONV_CH), F32)],
        compiler_params=pltpu.CompilerParams(dimension_semantics=("parallel",),
                                             vmem_limit_bytes=VMEM_LIMIT),
        name="mixer",
    )(x2, z, z, gates, o, cw, cb, lng, lnb, wco, wmo, wout, xg, wxq, mkt, mv, wxo)


def _mlp_kernel(h_ref, g_ref, w1_ref, w2_ref, fg_ref, y_ref):
    h = h_ref[...]
    u = _rms(h, g_ref[...]).astype(BF16)
    chunk = 1024
    acc = h
    for c in range(0, D_FF, chunk):
        a = jnp.maximum(_dot(u, w1_ref[:, c:c + chunk]), 0.0)
        acc = acc + _dot((a * a).astype(BF16), w2_ref[c:c + chunk, :])
    y_ref[...] = _rms(acc, fg_ref[...])


def _mlp(h, g, w1, w2, fg):
    t = h.shape[0]
    tm = MLP_TILE
    full = lambda arr: pl.BlockSpec(arr.shape, lambda i: (0,) * arr.ndim, pipeline_mode=pl.Buffered(1))
    row = pl.BlockSpec((tm, D_MODEL), lambda i: (i, 0))
    return pl.pallas_call(
        _mlp_kernel,
        grid=(t // tm,),
        in_specs=[row, full(g), full(w1), full(w2), full(fg)],
        out_specs=row,
        out_shape=jax.ShapeDtypeStruct((t, D_MODEL), F32),
        compiler_params=pltpu.CompilerParams(dimension_semantics=("parallel",),
                                             vmem_limit_bytes=VMEM_LIMIT),
        name="mlp",
    )(h, g, w1, w2, fg)


def _rope_placement():
    e = np.zeros((HEAD_PAD, MLA_HEADS * HEAD_PAD), np.float32)
    for h in range(MLA_HEADS):
        for j in range(QK_ROPE):
            e[j, h * HEAD_PAD + QK_NOPE + j] = 1.0
    return jnp.asarray(e, BF16)


def kernel(x, mem, positions, norm_mix_g, w_in, conv_w, conv_b, conv_ln_g, conv_ln_b, w_conv_out, q_norm_g, w_uq, kv_norm_g, w_ukv, w_mla_out, w_out, norm_xattn_g, norm_mem_g, w_xq, w_xkv, w_xo, norm_mlp_g, w_mlp1, w_mlp2, final_norm_g):
    batch, seq, _ = x.shape
    mem_len = mem.shape[1]
    t = batch * seq
    assert w_in.shape[0] == 1, "single-layer block (the final rmsnorm is fused into the MLP kernel)"
    assert seq % KV_CHUNK == 0 and seq % TOKEN_TILE == 0 and t % ROPE_TILE == 0 and t % MLP_TILE == 0

    inv_freq = ROPE_THETA ** (-jnp.arange(ROPE_HALF, dtype=F32) / ROPE_HALF)
    cos_t, sin_t, cs = _rope_tables(positions.reshape(1, t), inv_freq.reshape(ROPE_HALF, 1))
    place = _rope_placement()
    row = lambda v: v.reshape(1, -1)

    h = x.reshape(t, D_MODEL)
    wi = _regroup_w_in(jnp.swapaxes(w_in, 1, 2))
    wuq = w_uq[0].astype(BF16).reshape(Q_LORA, MLA_HEADS, QK_NOPE + QK_ROPE)
    wqt = jnp.pad(wuq, ((0, 0), (0, 0), (0, HEAD_PAD - QK_NOPE - QK_ROPE))).reshape(Q_LORA, -1).T
    wukv = w_ukv[0].astype(BF16).reshape(KV_LORA, MLA_HEADS, QK_NOPE + V_DIM)
    wk = jnp.pad(wukv[:, :, :QK_NOPE], ((0, 0), (0, 0), (0, HEAD_PAD - QK_NOPE))).reshape(KV_LORA, -1)
    wvt = jnp.pad(wukv[:, :, QK_NOPE:], ((0, 0), (0, 0), (0, V_ROWS - V_DIM))).reshape(KV_LORA, -1).T
    ones = np.zeros((MLA_HEADS, V_ROWS, 1), np.float32)
    ones[:, V_DIM] = 1.0
    ones = jnp.asarray(ones.reshape(MLA_HEADS * V_ROWS, 1))

    z, gates, qt, kp, vt3 = _in_proj(
        h, row(norm_mix_g[0]), wi, row(q_norm_g[0]), row(kv_norm_g[0]), wqt, wk, wvt, ones, place,
        cos_t, sin_t, cs)

    k3 = kp.reshape(t // KV_CHUNK, KV_CHUNK, MLA_HEADS * HEAD_PAD)
    o = _attention(qt, k3, vt3, batch, seq)

    xw = X_HEADS * X_HEAD_DIM
    wxkv = w_xkv[0].astype(BF16)
    mkt, mv = _mem_kv(mem.reshape(batch * mem_len, D_MODEL), row(norm_mem_g[0]),
                      wxkv[:, :xw].T, wxkv[:, xw:], batch, mem_len)

    cw = jnp.pad(conv_w[0], ((0, HALO - CONV_WIDTH), (0, 0)))
    h = _mixer(h, z, gates, o, cw, row(conv_b[0]), row(conv_ln_g[0]), row(conv_ln_b[0]),
               w_conv_out[0].astype(BF16), w_mla_out[0].astype(BF16), w_out[0].astype(BF16),
               row(norm_xattn_g[0]), w_xq[0].astype(BF16), mkt, mv, w_xo[0].astype(BF16), seq)
    h = _mlp(h, row(norm_mlp_g[0]), w_mlp1[0].astype(BF16), w_mlp2[0].astype(BF16),
             row(final_norm_g))
    return h.reshape(batch, seq, D_MODEL)
```

```python
import functools

import numpy as np
import jax
import jax.numpy as jnp
from jax import lax
from jax.experimental import pallas as pl
from jax.experimental.pallas import tpu as pltpu

D_MODEL = 1024
CONV_CH = D_MODEL // 2
CONV_WIDTH = 31
MLA_HEADS = 8
QK_NOPE = D_MODEL // 16
QK_ROPE = D_MODEL // 32
V_DIM = D_MODEL // 16
Q_LORA = 3 * D_MODEL // 8
KV_LORA = D_MODEL // 4
X_HEADS = 4
X_HEAD_DIM = D_MODEL // 8
D_FF = 4 * D_MODEL
ROPE_THETA = 10000.0
EPS = 1e-6

HEAD_PAD = 128
V_ROWS = V_DIM + 16
ATTN_HEADS = 8
ROPE_HALF = QK_ROPE // 2
SUBLANES = 8
BF16_ROWS = 2 * SUBLANES
HALO = 32
TOKEN_TILE = 1024
CONV_CHUNK = 256
MLP_TILE = 1024
KV_CHUNK = 512
IN_TILE = 2 * KV_CHUNK
Q_TILE = KV_CHUNK
ROPE_TILE = 2048
LOG2E = 1.4426950408889634


def _column_ranges(widths):
    out, start = {}, 0
    for name, width in widths:
        out[name] = (start, start + width)
        start += width
    return out


IN_COLS = _column_ranges([("glu_a", CONV_CH), ("glu_g", CONV_CH), ("gates", 2 * D_MODEL),
                          ("c_q", Q_LORA), ("c_kv", KV_LORA), ("k_rope", HEAD_PAD)])
MASK_VALUE = -1e30
VMEM_LIMIT = 56 * 1024 * 1024

F32 = jnp.float32
BF16 = jnp.bfloat16


def _rms(x, g):
    return x * lax.rsqrt(jnp.mean(x * x, axis=-1, keepdims=True) + EPS) * g


def _sigmoid(x):
    return 1.0 / (1.0 + jnp.exp(-x))


def _dot(a, b):
    return jnp.dot(a, b, preferred_element_type=F32)


def _dot_nt(a, b):
    return lax.dot_general(a, b, (((1,), (1,)), ((), ())), preferred_element_type=F32)


def _rope_kernel(pos_ref, invf_ref, cos_ref, sin_ref, cs_ref):
    ang = pos_ref[...].astype(F32) * invf_ref[...]
    c = jnp.cos(ang)
    s = jnp.sin(ang)
    cos_ref[...] = c
    sin_ref[...] = s
    pad = jnp.zeros((HEAD_PAD - 2 * QK_ROPE, ang.shape[1]), F32)
    cs_ref[...] = jnp.concatenate([c, c, -s, s, pad], axis=0).T


def _rope_tables(pos_row, inv_freq_col):
    t = pos_row.shape[1]
    out = jax.ShapeDtypeStruct((ROPE_HALF, t), F32)
    spec = pl.BlockSpec((ROPE_HALF, ROPE_TILE), lambda i: (0, i))
    return pl.pallas_call(
        _rope_kernel,
        grid=(t // ROPE_TILE,),
        in_specs=[pl.BlockSpec((1, ROPE_TILE), lambda i: (0, i)),
                  pl.BlockSpec((ROPE_HALF, 1), lambda i: (0, 0))],
        out_specs=[spec, spec, pl.BlockSpec((ROPE_TILE, HEAD_PAD), lambda i: (i, 0))],
        out_shape=[out, out, jax.ShapeDtypeStruct((t, HEAD_PAD), F32)],
        name="rope_tables",
    )(pos_row, inv_freq_col)


def _regroup_kernel(w_ref, o_ref):
    c0 = 2 * CONV_CH
    c2 = c0 + Q_LORA + KV_LORA
    c3 = c2 + QK_ROPE

    def put(name, value):
        o_ref[IN_COLS[name][0]:IN_COLS[name][0] + value.shape[0], :] = value.astype(BF16)

    put("glu_a", w_ref[0, :CONV_CH, :])
    put("glu_g", w_ref[0, CONV_CH:c0, :])
    put("gates", w_ref[0, c3:, :])
    put("c_q", w_ref[0, c0:c0 + Q_LORA, :])
    put("c_kv", w_ref[0, c0 + Q_LORA:c2, :])
    kr = w_ref[0, c2:c3, :]
    zeros = jnp.zeros((HEAD_PAD - 2 * QK_ROPE, kr.shape[1]), F32)
    put("k_rope", jnp.concatenate([kr, kr[ROPE_HALF:], kr[:ROPE_HALF], zeros], axis=0))


def _regroup_w_in(wt):
    lanes = 256
    rows = IN_COLS["k_rope"][1]
    return pl.pallas_call(
        _regroup_kernel,
        grid=(wt.shape[2] // lanes,),
        in_specs=[pl.BlockSpec((1, wt.shape[1], lanes), lambda i: (0, 0, i))],
        out_specs=pl.BlockSpec((rows, lanes), lambda i: (0, i)),
        out_shape=jax.ShapeDtypeStruct((rows, wt.shape[2]), BF16),
        name="regroup_w_in",
    )(wt)


def _in_proj_kernel(x_ref, g_ref, w_ref, qg_ref, kvg_ref, wqt_ref, wk_ref, wvt_ref, ones_ref, place_ref,
                    cos_ref, sin_ref, cs_ref,
                    z_ref, gates_ref, qt_ref, k_ref, vt_ref):
    u = _rms(x_ref[...], g_ref[...]).astype(BF16)
    proj = lambda name: _dot_nt(u, w_ref[IN_COLS[name][0]:IN_COLS[name][1], :])
    z_ref[...] = (proj("glu_a") * _sigmoid(proj("glu_g"))).astype(BF16)
    gates_ref[...] = _sigmoid(proj("gates")).astype(BF16)

    cqn = _rms(proj("c_q"), qg_ref[...]).astype(BF16)
    qt = _dot_nt(wqt_ref[...], cqn)
    scale = (QK_NOPE + QK_ROPE) ** -0.5 * LOG2E
    c = cos_ref[...]
    s = sin_ref[...]
    for h in range(MLA_HEADS):
        b = h * HEAD_PAD
        r1 = b + QK_NOPE
        r2 = r1 + ROPE_HALF
        r3 = r2 + ROPE_HALF
        t1 = qt[r1:r2]
        t2 = qt[r2:r3]
        qt_ref[b:r1, :] = (qt[b:r1] * scale).astype(BF16)
        qt_ref[r1:r2, :] = ((t1 * c - t2 * s) * scale).astype(BF16)
        qt_ref[r2:r3, :] = ((t2 * c + t1 * s) * scale).astype(BF16)
        qt_ref[r3:b + HEAD_PAD, :] = (qt[r3:b + HEAD_PAD] * scale).astype(BF16)

    ckvn = _rms(proj("c_kv"), kvg_ref[...]).astype(BF16)
    t = proj("k_rope") * cs_ref[...]
    rot = t + pltpu.roll(t, HEAD_PAD - QK_ROPE, 1)
    k_ref[...] = (_dot(ckvn, wk_ref[...]) + _dot(rot.astype(BF16), place_ref[...])).astype(BF16)
    vt = (_dot_nt(wvt_ref[...], ckvn) + ones_ref[...]).astype(BF16)
    for c in range(vt_ref.shape[0]):
        vt_ref[c] = vt[:, c * KV_CHUNK:(c + 1) * KV_CHUNK]


def _in_proj(x2, g, w, qg, kvg, wqt, wk, wvt, ones, place, cos_t, sin_t, cs):
    t = x2.shape[0]
    tm = IN_TILE
    chunks = tm // KV_CHUNK
    n = t // tm
    full = lambda arr: pl.BlockSpec(arr.shape, lambda i: (0,) * arr.ndim)
    row = lambda w: pl.BlockSpec((tm, w), lambda i: (i, 0))
    in_specs = [row(D_MODEL), full(g), full(w), full(qg), full(kvg), full(wqt), full(wk), full(wvt),
                full(ones), full(place),
                pl.BlockSpec((ROPE_HALF, tm), lambda i: (0, i)),
                pl.BlockSpec((ROPE_HALF, tm), lambda i: (0, i)),
                row(HEAD_PAD)]
    out_shape = [jax.ShapeDtypeStruct((t, CONV_CH), BF16),
                 jax.ShapeDtypeStruct((t, 2 * D_MODEL), BF16),
                 jax.ShapeDtypeStruct((MLA_HEADS * HEAD_PAD, t), BF16),
                 jax.ShapeDtypeStruct((t, MLA_HEADS * HEAD_PAD), BF16),
                 jax.ShapeDtypeStruct((n * chunks, MLA_HEADS * V_ROWS, KV_CHUNK), BF16)]
    out_specs = [row(CONV_CH), row(2 * D_MODEL),
                 pl.BlockSpec((MLA_HEADS * HEAD_PAD, tm), lambda i: (0, i)),
                 row(MLA_HEADS * HEAD_PAD),
                 pl.BlockSpec((chunks, MLA_HEADS * V_ROWS, KV_CHUNK), lambda i: (i, 0, 0))]
    return pl.pallas_call(
        _in_proj_kernel,
        grid=(n,),
        in_specs=in_specs,
        out_specs=out_specs,
        out_shape=out_shape,
        compiler_params=pltpu.CompilerParams(dimension_semantics=("parallel",),
                                             vmem_limit_bytes=VMEM_LIMIT),
        name="in_proj",
    )(x2, g, w, qg, kvg, wqt, wk, wvt, ones, place, cos_t, sin_t, cs)


def _attn_kernel(qt_ref, k_ref, vt_ref, o_ref, s_scr, acc_scr):
    qi = pl.program_id(2)
    tk = k_ref.shape[1]
    tq = qt_ref.shape[1]
    assert tq == tk
    heads = range(ATTN_HEADS)
    half = tk // 2
    n_items = qi + 1
    causal = (lax.broadcasted_iota(jnp.int32, (tk, tq), 0) <= lax.broadcasted_iota(jnp.int32, (tk, tq), 1))
    qts = [qt_ref[h * HEAD_PAD:(h + 1) * HEAD_PAD, :] for h in heads]

    def scores_to(slot, kc, diagonal, hs=heads):
        ms = []
        for h in hs:
            k = k_ref[kc, :, h * HEAD_PAD:(h + 1) * HEAD_PAD]
            if not diagonal:
                s = _dot(k, qts[h])
                s_scr[slot, h] = s
                ms.append(jnp.max(s, axis=0, keepdims=True))
            else:
                top = jnp.where(causal[:half], _dot(k[:half], qts[h]), MASK_VALUE)
                low = jnp.where(causal[half:, half:], _dot(k[half:], qts[h][:, half:]), MASK_VALUE)
                s_scr[slot, h, :half, :] = top
                s_scr[slot, h, half:, half:] = low
                s_scr[slot, h, half:, :half] = jnp.full((tk - half, half), MASK_VALUE, F32)
                low_max = jnp.concatenate([jnp.full((1, half), MASK_VALUE, F32),
                                           jnp.max(low, axis=0, keepdims=True)], axis=1)
                ms.append(jnp.maximum(jnp.max(top, axis=0, keepdims=True), low_max))
        return tuple(ms)

    def accumulate(m_run, slot, kc, ms, hs=heads):
        new = []
        for h, m_old, m_chunk in zip(hs, m_run, ms):
            p = jnp.concatenate([jnp.exp2(s_scr[slot, h, r:r + BF16_ROWS, :] - m_chunk).astype(BF16)
                                 for r in range(0, tk, BF16_ROWS)], axis=0)
            pv = _dot(vt_ref[kc, h * V_ROWS:(h + 1) * V_ROWS, :], p)
            m_new = jnp.maximum(m_old, m_chunk)
            a = jnp.exp2(m_old - m_new)
            b = jnp.exp2(m_chunk - m_new)
            acc_scr[h] = a * acc_scr[h] + b * pv
            new.append(m_new)
        return tuple(new)

    def step(cur, look_ahead, j, carry):
        m_run, kc, ms = carry
        new_run, ahead_m = [], []
        for h in heads:
            if look_ahead:
                ahead_m += scores_to(1 - cur, j, False, (h,))
            new_run += accumulate(m_run[h:h + 1], cur, kc, ms[h:h + 1], (h,))
        return tuple(new_run), j, (tuple(ahead_m) if look_ahead else ms)

    branches = [functools.partial(step, cur, look_ahead) for look_ahead in (True, False) for cur in (0, 1)]

    def body(j, carry):
        return lax.switch(2 * (j == n_items - 1).astype(jnp.int32) + j % 2, branches, j, carry)

    acc_scr[...] = jnp.zeros(acc_scr.shape, F32)
    m_init = tuple(jnp.full((1, tq), MASK_VALUE, F32) for _ in heads)
    lax.fori_loop(0, n_items, body, (m_init, qi, scores_to(0, qi, True)))
    outs = [acc_scr[h, :V_DIM, :] / acc_scr[h, V_DIM:V_DIM + 1, :] for h in heads]
    o_ref[...] = jnp.concatenate(outs, axis=0).astype(BF16)


def _attention(qt, k3, vt3, batch, seq):
    tq = Q_TILE
    nq = seq // tq
    nk = seq // KV_CHUNK
    t = batch * seq
    return pl.pallas_call(
        _attn_kernel,
        grid=(batch, MLA_HEADS // ATTN_HEADS, nq),
        in_specs=[pl.BlockSpec((ATTN_HEADS * HEAD_PAD, tq), lambda b, hg, qi: (hg, b * nq + qi)),
                  pl.BlockSpec((nk, KV_CHUNK, ATTN_HEADS * HEAD_PAD), lambda b, hg, qi: (b, 0, hg),
                               pipeline_mode=pl.Buffered(1)),
                  pl.BlockSpec((nk, ATTN_HEADS * V_ROWS, KV_CHUNK), lambda b, hg, qi: (b, hg, 0),
                               pipeline_mode=pl.Buffered(1))],
        out_specs=pl.BlockSpec((ATTN_HEADS * V_DIM, tq), lambda b, hg, qi: (hg, b * nq + qi)),
        out_shape=jax.ShapeDtypeStruct((MLA_HEADS * V_DIM, t), BF16),
        scratch_shapes=[pltpu.VMEM((2, ATTN_HEADS, KV_CHUNK, tq), F32),
                        pltpu.VMEM((ATTN_HEADS, V_ROWS, tq), F32)],
        compiler_params=pltpu.CompilerParams(
            dimension_semantics=("parallel", "parallel", "arbitrary"),
            vmem_limit_bytes=VMEM_LIMIT),
        name="mla_attention",
    )(qt, k3, vt3)


def _mem_kernel(mem_ref, g_ref, wkt_ref, wv_ref, kt_ref, v_ref):
    mn = _rms(mem_ref[...], g_ref[...]).astype(BF16)
    kt_ref[0] = _dot_nt(wkt_ref[...], mn).astype(BF16)
    v_ref[0] = _dot(mn, wv_ref[...]).astype(BF16)


def _mem_kv(mem2, g, wkt, wv, batch, mem_len):
    xw = X_HEADS * X_HEAD_DIM
    full = lambda arr: pl.BlockSpec(arr.shape, lambda b: (0,) * arr.ndim)
    return pl.pallas_call(
        _mem_kernel,
        grid=(batch,),
        in_specs=[pl.BlockSpec((mem_len, D_MODEL), lambda b: (b, 0)), full(g), full(wkt), full(wv)],
        out_specs=[pl.BlockSpec((1, xw, mem_len), lambda b: (b, 0, 0)),
                   pl.BlockSpec((1, mem_len, xw), lambda b: (b, 0, 0))],
        out_shape=[jax.ShapeDtypeStruct((batch, xw, mem_len), BF16),
                   jax.ShapeDtypeStruct((batch, mem_len, xw), BF16)],
        compiler_params=pltpu.CompilerParams(dimension_semantics=("parallel",),
                                             vmem_limit_bytes=VMEM_LIMIT),
        name="mem_kv",
    )(mem2, g, wkt, wv)


def _mixer_kernel(tiles_per_seq, x_ref, z_ref, halo_ref, gates_ref, o_ref,
                  cw_ref, cb_ref, lng_ref, lnb_ref, wco_ref, wmo_ref, wout_ref,
                  xg_ref, wxq_ref, mkt_ref, mv_ref, wxo_ref,
                  h_ref, zext_ref, shift_ref, conv_ref):
    tm = x_ref.shape[0]
    first = (pl.program_id(0) % tiles_per_seq) == 0
    zext_ref[0:HALO, :] = jnp.where(first, 0.0, halo_ref[...].astype(F32))
    zext_ref[HALO:, :] = z_ref[...].astype(F32)

    rows = 64
    off = HALO - (CONV_WIDTH - 1)
    win = CONV_CHUNK + HALO
    for r0 in range(0, tm, CONV_CHUNK):
        for s in range(1, SUBLANES):
            shift_ref[s - 1, 0:win - SUBLANES, :] = zext_ref[r0 + s:r0 + s + win - SUBLANES, :]
        for c in range(CONV_CH // 128):
            cs = slice(c * 128, (c + 1) * 128)
            for r in range(r0, r0 + CONV_CHUNK, rows):
                acc = jnp.broadcast_to(cb_ref[:, cs], (rows, 128))
                for w in range(CONV_WIDTH):
                    shift, base = (off + w) % SUBLANES, (off + w) // SUBLANES * SUBLANES
                    if shift == 0:
                        window = zext_ref[r + base:r + base + rows, cs]
                    else:
                        window = shift_ref[shift - 1, r - r0 + base:r - r0 + base + rows, cs]
                    acc = acc + window * cw_ref[w:w + 1, cs]
                conv_ref[r:r + rows, cs] = acc

    y = conv_ref[...]
    mu = jnp.mean(y, axis=-1, keepdims=True)
    yc = y - mu
    var = jnp.mean(yc * yc, axis=-1, keepdims=True)
    y = yc * lax.rsqrt(var + EPS) * lng_ref[...] + lnb_ref[...]
    y = y * _sigmoid(y)
    conv_out = _dot(y.astype(BF16), wco_ref[...])
    mla_out = lax.dot_general(o_ref[...], wmo_ref[...], (((0,), (0,)), ((), ())),
                              preferred_element_type=F32)
    gates = gates_ref[...].astype(F32)
    merged = gates[:, :D_MODEL] * conv_out + gates[:, D_MODEL:] * mla_out
    h1 = x_ref[...] + _dot(merged.astype(BF16), wout_ref[...])

    u = _rms(h1, xg_ref[...]).astype(BF16)
    q = (_dot(u, wxq_ref[...]) * (X_HEAD_DIM ** -0.5)).astype(BF16)
    heads = []
    for h in range(X_HEADS):
        hs = slice(h * X_HEAD_DIM, (h + 1) * X_HEAD_DIM)
        s = _dot(q[:, hs], mkt_ref[0, hs, :])
        p = jnp.exp(s - jnp.max(s, axis=-1, keepdims=True))
        p = p / jnp.sum(p, axis=-1, keepdims=True)
        heads.append(_dot(p.astype(BF16), mv_ref[0, :, hs]))
    xo = jnp.concatenate(heads, axis=1).astype(BF16)
    h_ref[...] = h1 + _dot(xo, wxo_ref[...])


def _mixer(x2, z, gates, o, cw, cb, lng, lnb, wco, wmo, wout, xg, wxq, mkt, mv, wxo, seq):
    t = x2.shape[0]
    tm = TOKEN_TILE
    tiles_per_seq = seq // tm
    halo_per_tile = tm // HALO
    mem_len = mv.shape[1]
    xw = X_HEADS * X_HEAD_DIM
    full = lambda arr: pl.BlockSpec(arr.shape, lambda i: (0,) * arr.ndim)
    row = lambda w: pl.BlockSpec((tm, w), lambda i: (i, 0))
    in_specs = [row(D_MODEL), row(CONV_CH),
                pl.BlockSpec((HALO, CONV_CH), lambda i: (jnp.maximum(i * halo_per_tile - 1, 0), 0)),
                row(2 * D_MODEL), pl.BlockSpec((MLA_HEADS * V_DIM, tm), lambda i: (0, i)),
                full(cw), full(cb), full(lng), full(lnb), full(wco), full(wmo), full(wout),
                full(xg), full(wxq),
                pl.BlockSpec((1, xw, mem_len), lambda i: (i // tiles_per_seq, 0, 0)),
                pl.BlockSpec((1, mem_len, xw), lambda i: (i // tiles_per_seq, 0, 0)),
                full(wxo)]
    return pl.pallas_call(
        functools.partial(_mixer_kernel, tiles_per_seq),
        grid=(t // tm,),
        in_specs=in_specs,
        out_specs=row(D_MODEL),
        out_shape=jax.ShapeDtypeStruct((t, D_MODEL), F32),
        scratch_shapes=[pltpu.VMEM((tm + HALO, CONV_CH), F32),
                        pltpu.VMEM((SUBLANES - 1, CONV_CHUNK + HALO, CONV_CH), F32),
                        pltpu.VMEM((tm, CONV_CH), F32)],
        compiler_params=pltpu.CompilerParams(dimension_semantics=("parallel",),
                                             vmem_limit_bytes=VMEM_LIMIT),
        name="mixer",
    )(x2, z, z, gates, o, cw, cb, lng, lnb, wco, wmo, wout, xg, wxq, mkt, mv, wxo)


def _mlp_kernel(h_ref, g_ref, w1_ref, w2_ref, fg_ref, y_ref):
    h = h_ref[...]
    u = _rms(h, g_ref[...]).astype(BF16)
    chunk = 1024
    acc = h
    for c in range(0, D_FF, chunk):
        a = jnp.maximum(_dot(u, w1_ref[:, c:c + chunk]), 0.0)
        acc = acc + _dot((a * a).astype(BF16), w2_ref[c:c + chunk, :])
    y_ref[...] = _rms(acc, fg_ref[...])


def _mlp(h, g, w1, w2, fg):
    t = h.shape[0]
    tm = MLP_TILE
    full = lambda arr: pl.BlockSpec(arr.shape, lambda i: (0,) * arr.ndim, pipeline_mode=pl.Buffered(1))
    row = pl.BlockSpec((tm, D_MODEL), lambda i: (i, 0))
    return pl.pallas_call(
        _mlp_kernel,
        grid=(t // tm,),
        in_specs=[row, full(g), full(w1), full(w2), full(fg)],
        out_specs=row,
        out_shape=jax.ShapeDtypeStruct((t, D_MODEL), F32),
        compiler_params=pltpu.CompilerParams(dimension_semantics=("parallel",),
                                             vmem_limit_bytes=VMEM_LIMIT),
        name="mlp",
    )(h, g, w1, w2, fg)


def _rope_placement():
    e = np.zeros((HEAD_PAD, MLA_HEADS * HEAD_PAD), np.float32)
    for h in range(MLA_HEADS):
        for j in range(QK_ROPE):
            e[j, h * HEAD_PAD + QK_NOPE + j] = 1.0
    return jnp.asarray(e, BF16)


def kernel(x, mem, positions, norm_mix_g, w_in, conv_w, conv_b, conv_ln_g, conv_ln_b, w_conv_out, q_norm_g, w_uq, kv_norm_g, w_ukv, w_mla_out, w_out, norm_xattn_g, norm_mem_g, w_xq, w_xkv, w_xo, norm_mlp_g, w_mlp1, w_mlp2, final_norm_g):
    batch, seq, _ = x.shape
    mem_len = mem.shape[1]
    t = batch * seq
    assert w_in.shape[0] == 1, "single-layer block (the final rmsnorm is fused into the MLP kernel)"
    assert seq % IN_TILE == 0 and seq % KV_CHUNK == 0 and seq % TOKEN_TILE == 0 and t % ROPE_TILE == 0 and t % MLP_TILE == 0

    inv_freq = ROPE_THETA ** (-jnp.arange(ROPE_HALF, dtype=F32) / ROPE_HALF)
    cos_t, sin_t, cs = _rope_tables(positions.reshape(1, t), inv_freq.reshape(ROPE_HALF, 1))
    place = _rope_placement()
    row = lambda v: v.reshape(1, -1)

    h = x.reshape(t, D_MODEL)
    wi = _regroup_w_in(jnp.swapaxes(w_in, 1, 2))
    wuq = w_uq[0].astype(BF16).reshape(Q_LORA, MLA_HEADS, QK_NOPE + QK_ROPE)
    wqt = jnp.pad(wuq, ((0, 0), (0, 0), (0, HEAD_PAD - QK_NOPE - QK_ROPE))).reshape(Q_LORA, -1).T
    wukv = w_ukv[0].astype(BF16).reshape(KV_LORA, MLA_HEADS, QK_NOPE + V_DIM)
    wk = jnp.pad(wukv[:, :, :QK_NOPE], ((0, 0), (0, 0), (0, HEAD_PAD - QK_NOPE))).reshape(KV_LORA, -1)
    wvt = jnp.pad(wukv[:, :, QK_NOPE:], ((0, 0), (0, 0), (0, V_ROWS - V_DIM))).reshape(KV_LORA, -1).T
    ones = np.zeros((MLA_HEADS, V_ROWS, 1), np.float32)
    ones[:, V_DIM] = 1.0
    ones = jnp.asarray(ones.reshape(MLA_HEADS * V_ROWS, 1))

    z, gates, qt, kp, vt3 = _in_proj(
        h, row(norm_mix_g[0]), wi, row(q_norm_g[0]), row(kv_norm_g[0]), wqt, wk, wvt, ones, place,
        cos_t, sin_t, cs)

    k3 = kp.reshape(t // KV_CHUNK, KV_CHUNK, MLA_HEADS * HEAD_PAD)
    o = _attention(qt, k3, vt3, batch, seq)

    xw = X_HEADS * X_HEAD_DIM
    wxkv = w_xkv[0].astype(BF16)
    mkt, mv = _mem_kv(mem.reshape(batch * mem_len, D_MODEL), row(norm_mem_g[0]),
                      wxkv[:, :xw].T, wxkv[:, xw:], batch, mem_len)

    cw = jnp.pad(conv_w[0], ((0, HALO - CONV_WIDTH), (0, 0)))
    h = _mixer(h, z, gates, o, cw, row(conv_b[0]), row(conv_ln_g[0]), row(conv_ln_b[0]),
               w_conv_out[0].astype(BF16), w_mla_out[0].astype(BF16), w_out[0].astype(BF16),
               row(norm_xattn_g[0]), w_xq[0].astype(BF16), mkt, mv, w_xo[0].astype(BF16), seq)
    h = _mlp(h, row(norm_mlp_g[0]), w_mlp1[0].astype(BF16), w_mlp2[0].astype(BF16),
             row(final_norm_g))
    return h.reshape(batch, seq, D_MODEL)
```

```python
import functools

import numpy as np
import jax
import jax.numpy as jnp
from jax import lax
from jax.experimental import pallas as pl
from jax.experimental.pallas import tpu as pltpu

D_MODEL = 1024
CONV_CH = D_MODEL // 2
CONV_WIDTH = 31
MLA_HEADS = 8
QK_NOPE = D_MODEL // 16
QK_ROPE = D_MODEL // 32
V_DIM = D_MODEL // 16
Q_LORA = 3 * D_MODEL // 8
KV_LORA = D_MODEL // 4
X_HEADS = 4
X_HEAD_DIM = D_MODEL // 8
D_FF = 4 * D_MODEL
ROPE_THETA = 10000.0
EPS = 1e-6

LANES = 128
SUBLANES = 8
BF16_ROWS = 2 * SUBLANES
HEAD_PAD = LANES
V_ROWS = V_DIM + BF16_ROWS
ATTN_HEADS = 8
ROPE_HALF = QK_ROPE // 2
HALO = 32
TOKEN_TILE = 1024
CONV_CHUNK = 256
MLP_TILE = 1024
KV_CHUNK = 512
IN_TILE = 2 * KV_CHUNK
Q_TILE = KV_CHUNK
ROPE_TILE = 2048
LOG2E = 1.4426950408889634


def _column_ranges(widths):
    out, start = {}, 0
    for name, width in widths:
        out[name] = (start, start + width)
        start += width
    return out


IN_COLS = _column_ranges([("glu_a", CONV_CH), ("glu_g", CONV_CH), ("gates", 2 * D_MODEL),
                          ("c_q", Q_LORA), ("c_kv", KV_LORA), ("k_rope", HEAD_PAD)])
MASK_VALUE = -1e30
VMEM_LIMIT = 56 * 1024 * 1024

F32 = jnp.float32
BF16 = jnp.bfloat16


def _rms(x, g):
    return x * lax.rsqrt(jnp.mean(x * x, axis=-1, keepdims=True) + EPS) * g


def _sigmoid(x):
    return 1.0 / (1.0 + jnp.exp(-x))


def _dot(a, b):
    return jnp.dot(a, b, preferred_element_type=F32)


def _dot_nt(a, b):
    return lax.dot_general(a, b, (((1,), (1,)), ((), ())), preferred_element_type=F32)


def _rope_kernel(pos_ref, invf_ref, cos_ref, sin_ref, cs_ref):
    ang = pos_ref[...].astype(F32) * invf_ref[...]
    c = jnp.cos(ang)
    s = jnp.sin(ang)
    cos_ref[...] = c
    sin_ref[...] = s
    pad = jnp.zeros((HEAD_PAD - 2 * QK_ROPE, ang.shape[1]), F32)
    cs_ref[...] = jnp.concatenate([c, c, -s, s, pad], axis=0).T


def _rope_tables(pos_row, inv_freq_col):
    t = pos_row.shape[1]
    out = jax.ShapeDtypeStruct((ROPE_HALF, t), F32)
    spec = pl.BlockSpec((ROPE_HALF, ROPE_TILE), lambda i: (0, i))
    return pl.pallas_call(
        _rope_kernel,
        grid=(t // ROPE_TILE,),
        in_specs=[pl.BlockSpec((1, ROPE_TILE), lambda i: (0, i)),
                  pl.BlockSpec((ROPE_HALF, 1), lambda i: (0, 0))],
        out_specs=[spec, spec, pl.BlockSpec((ROPE_TILE, HEAD_PAD), lambda i: (i, 0))],
        out_shape=[out, out, jax.ShapeDtypeStruct((t, HEAD_PAD), F32)],
        name="rope_tables",
    )(pos_row, inv_freq_col)


def _regroup_kernel(w_ref, o_ref):
    c0 = 2 * CONV_CH
    c2 = c0 + Q_LORA + KV_LORA
    c3 = c2 + QK_ROPE

    def put(name, value):
        o_ref[IN_COLS[name][0]:IN_COLS[name][0] + value.shape[0], :] = value.astype(BF16)

    put("glu_a", w_ref[0, :CONV_CH, :])
    put("glu_g", w_ref[0, CONV_CH:c0, :])
    put("gates", w_ref[0, c3:, :])
    put("c_q", w_ref[0, c0:c0 + Q_LORA, :])
    put("c_kv", w_ref[0, c0 + Q_LORA:c2, :])
    kr = w_ref[0, c2:c3, :]
    zeros = jnp.zeros((HEAD_PAD - 2 * QK_ROPE, kr.shape[1]), F32)
    put("k_rope", jnp.concatenate([kr, kr[ROPE_HALF:], kr[:ROPE_HALF], zeros], axis=0))


def _regroup_w_in(wt):
    lanes = 256
    rows = IN_COLS["k_rope"][1]
    return pl.pallas_call(
        _regroup_kernel,
        grid=(wt.shape[2] // lanes,),
        in_specs=[pl.BlockSpec((1, wt.shape[1], lanes), lambda i: (0, 0, i))],
        out_specs=pl.BlockSpec((rows, lanes), lambda i: (0, i)),
        out_shape=jax.ShapeDtypeStruct((rows, wt.shape[2]), BF16),
        name="regroup_w_in",
    )(wt)


def _in_proj_kernel(x_ref, g_ref, w_ref, qg_ref, kvg_ref, wqt_ref, wk_ref, wvt_ref, ones_ref, place_ref,
                    cos_ref, sin_ref, cs_ref,
                    z_ref, gates_ref, qt_ref, k_ref, vt_ref):
    u = _rms(x_ref[...], g_ref[...]).astype(BF16)
    proj = lambda name: _dot_nt(u, w_ref[IN_COLS[name][0]:IN_COLS[name][1], :])
    z_ref[...] = (proj("glu_a") * _sigmoid(proj("glu_g"))).astype(BF16)
    gates_ref[...] = _sigmoid(proj("gates")).astype(BF16)

    cqn = _rms(proj("c_q"), qg_ref[...]).astype(BF16)
    qt = _dot_nt(wqt_ref[...], cqn)
    scale = (QK_NOPE + QK_ROPE) ** -0.5 * LOG2E
    c = cos_ref[...]
    s = sin_ref[...]
    for h in range(MLA_HEADS):
        b = h * HEAD_PAD
        r1 = b + QK_NOPE
        r2 = r1 + ROPE_HALF
        r3 = r2 + ROPE_HALF
        t1 = qt[r1:r2]
        t2 = qt[r2:r3]
        qt_ref[b:r1, :] = (qt[b:r1] * scale).astype(BF16)
        qt_ref[r1:r2, :] = ((t1 * c - t2 * s) * scale).astype(BF16)
        qt_ref[r2:r3, :] = ((t2 * c + t1 * s) * scale).astype(BF16)
        qt_ref[r3:b + HEAD_PAD, :] = (qt[r3:b + HEAD_PAD] * scale).astype(BF16)

    ckvn = _rms(proj("c_kv"), kvg_ref[...]).astype(BF16)
    t = proj("k_rope") * cs_ref[...]
    rot = t + pltpu.roll(t, HEAD_PAD - QK_ROPE, 1)
    k_ref[...] = (_dot(ckvn, wk_ref[...]) + _dot(rot.astype(BF16), place_ref[...])).astype(BF16)
    vt = (_dot_nt(wvt_ref[...], ckvn) + ones_ref[...]).astype(BF16)
    for c in range(vt_ref.shape[0]):
        vt_ref[c] = vt[:, c * KV_CHUNK:(c + 1) * KV_CHUNK]


def _in_proj(x2, g, w, qg, kvg, wqt, wk, wvt, ones, place, cos_t, sin_t, cs):
    t = x2.shape[0]
    tm = IN_TILE
    chunks = tm // KV_CHUNK
    n = t // tm
    full = lambda arr: pl.BlockSpec(arr.shape, lambda i: (0,) * arr.ndim)
    row = lambda w: pl.BlockSpec((tm, w), lambda i: (i, 0))
    in_specs = [row(D_MODEL), full(g), full(w), full(qg), full(kvg), full(wqt), full(wk), full(wvt),
                full(ones), full(place),
                pl.BlockSpec((ROPE_HALF, tm), lambda i: (0, i)),
                pl.BlockSpec((ROPE_HALF, tm), lambda i: (0, i)),
                row(HEAD_PAD)]
    out_shape = [jax.ShapeDtypeStruct((t, CONV_CH), BF16),
                 jax.ShapeDtypeStruct((t, 2 * D_MODEL), BF16),
                 jax.ShapeDtypeStruct((MLA_HEADS * HEAD_PAD, t), BF16),
                 jax.ShapeDtypeStruct((t, MLA_HEADS * HEAD_PAD), BF16),
                 jax.ShapeDtypeStruct((n * chunks, MLA_HEADS * V_ROWS, KV_CHUNK), BF16)]
    out_specs = [row(CONV_CH), row(2 * D_MODEL),
                 pl.BlockSpec((MLA_HEADS * HEAD_PAD, tm), lambda i: (0, i)),
                 row(MLA_HEADS * HEAD_PAD),
                 pl.BlockSpec((chunks, MLA_HEADS * V_ROWS, KV_CHUNK), lambda i: (i, 0, 0))]
    return pl.pallas_call(
        _in_proj_kernel,
        grid=(n,),
        in_specs=in_specs,
        out_specs=out_specs,
        out_shape=out_shape,
        compiler_params=pltpu.CompilerParams(dimension_semantics=("parallel",),
                                             vmem_limit_bytes=VMEM_LIMIT),
        name="in_proj",
    )(x2, g, w, qg, kvg, wqt, wk, wvt, ones, place, cos_t, sin_t, cs)


def _attn_kernel(qt_ref, k_ref, vt_ref, o_ref, s_scr, acc_scr):
    qi = pl.program_id(2)
    tk = k_ref.shape[1]
    tq = qt_ref.shape[1]
    assert tq == tk
    heads = range(ATTN_HEADS)
    half = tk // 2
    n_items = qi + 1
    causal = (lax.broadcasted_iota(jnp.int32, (tk, tq), 0) <= lax.broadcasted_iota(jnp.int32, (tk, tq), 1))
    qts = [qt_ref[h * HEAD_PAD:(h + 1) * HEAD_PAD, :] for h in heads]

    def scores_to(slot, kc, diagonal, hs=heads):
        ms = []
        for h in hs:
            k = k_ref[kc, :, h * HEAD_PAD:(h + 1) * HEAD_PAD]
            if not diagonal:
                s = _dot(k, qts[h])
                s_scr[slot, h] = s
                ms.append(jnp.max(s, axis=0, keepdims=True))
            else:
                top = jnp.where(causal[:half], _dot(k[:half], qts[h]), MASK_VALUE)
                low = jnp.where(causal[half:, half:], _dot(k[half:], qts[h][:, half:]), MASK_VALUE)
                s_scr[slot, h, :half, :] = top
                s_scr[slot, h, half:, half:] = low
                s_scr[slot, h, half:, :half] = jnp.full((tk - half, half), MASK_VALUE, F32)
                low_max = jnp.concatenate([jnp.full((1, half), MASK_VALUE, F32),
                                           jnp.max(low, axis=0, keepdims=True)], axis=1)
                ms.append(jnp.maximum(jnp.max(top, axis=0, keepdims=True), low_max))
        return tuple(ms)

    def accumulate(m_run, slot, kc, ms, hs=heads):
        new = []
        for h, m_old, m_chunk in zip(hs, m_run, ms):
            p = jnp.concatenate([jnp.exp2(s_scr[slot, h, r:r + BF16_ROWS, :] - m_chunk).astype(BF16)
                                 for r in range(0, tk, BF16_ROWS)], axis=0)
            pv = _dot(vt_ref[kc, h * V_ROWS:(h + 1) * V_ROWS, :], p)
            m_new = jnp.maximum(m_old, m_chunk)
            a = jnp.exp2(m_old - m_new)
            b = jnp.exp2(m_chunk - m_new)
            acc_scr[h] = a * acc_scr[h] + b * pv
            new.append(m_new)
        return tuple(new)

    def step(cur, look_ahead, j, carry):
        m_run, kc, ms = carry
        new_run, ahead_m = [], []
        for h in heads:
            if look_ahead:
                ahead_m += scores_to(1 - cur, j, False, (h,))
            new_run += accumulate(m_run[h:h + 1], cur, kc, ms[h:h + 1], (h,))
        return tuple(new_run), j, (tuple(ahead_m) if look_ahead else ms)

    branches = [functools.partial(step, cur, look_ahead) for look_ahead in (True, False) for cur in (0, 1)]

    def body(j, carry):
        return lax.switch(2 * (j == n_items - 1).astype(jnp.int32) + j % 2, branches, j, carry)

    acc_scr[...] = jnp.zeros(acc_scr.shape, F32)
    m_init = tuple(jnp.full((1, tq), MASK_VALUE, F32) for _ in heads)
    lax.fori_loop(0, n_items, body, (m_init, qi, scores_to(0, qi, True)))
    outs = [acc_scr[h, :V_DIM, :] / acc_scr[h, V_DIM:V_DIM + 1, :] for h in heads]
    o_ref[...] = jnp.concatenate(outs, axis=0).astype(BF16)


def _attention(qt, k3, vt3, batch, seq):
    tq = Q_TILE
    nq = seq // tq
    nk = seq // KV_CHUNK
    t = batch * seq
    return pl.pallas_call(
        _attn_kernel,
        grid=(batch, MLA_HEADS // ATTN_HEADS, nq),
        in_specs=[pl.BlockSpec((ATTN_HEADS * HEAD_PAD, tq), lambda b, hg, qi: (hg, b * nq + qi)),
                  pl.BlockSpec((nk, KV_CHUNK, ATTN_HEADS * HEAD_PAD), lambda b, hg, qi: (b, 0, hg),
                               pipeline_mode=pl.Buffered(1)),
                  pl.BlockSpec((nk, ATTN_HEADS * V_ROWS, KV_CHUNK), lambda b, hg, qi: (b, hg, 0),
                               pipeline_mode=pl.Buffered(1))],
        out_specs=pl.BlockSpec((ATTN_HEADS * V_DIM, tq), lambda b, hg, qi: (hg, b * nq + qi)),
        out_shape=jax.ShapeDtypeStruct((MLA_HEADS * V_DIM, t), BF16),
        scratch_shapes=[pltpu.VMEM((2, ATTN_HEADS, KV_CHUNK, tq), F32),
                        pltpu.VMEM((ATTN_HEADS, V_ROWS, tq), F32)],
        compiler_params=pltpu.CompilerParams(
            dimension_semantics=("parallel", "parallel", "arbitrary"),
            vmem_limit_bytes=VMEM_LIMIT),
        name="mla_attention",
    )(qt, k3, vt3)


def _mem_kernel(mem_ref, g_ref, wkt_ref, wv_ref, kt_ref, v_ref):
    mn = _rms(mem_ref[...], g_ref[...]).astype(BF16)
    kt_ref[0] = _dot_nt(wkt_ref[...], mn).astype(BF16)
    v_ref[0] = _dot(mn, wv_ref[...]).astype(BF16)


def _mem_kv(mem2, g, wkt, wv, batch, mem_len):
    xw = X_HEADS * X_HEAD_DIM
    full = lambda arr: pl.BlockSpec(arr.shape, lambda b: (0,) * arr.ndim)
    return pl.pallas_call(
        _mem_kernel,
        grid=(batch,),
        in_specs=[pl.BlockSpec((mem_len, D_MODEL), lambda b: (b, 0)), full(g), full(wkt), full(wv)],
        out_specs=[pl.BlockSpec((1, xw, mem_len), lambda b: (b, 0, 0)),
                   pl.BlockSpec((1, mem_len, xw), lambda b: (b, 0, 0))],
        out_shape=[jax.ShapeDtypeStruct((batch, xw, mem_len), BF16),
                   jax.ShapeDtypeStruct((batch, mem_len, xw), BF16)],
        compiler_params=pltpu.CompilerParams(dimension_semantics=("parallel",),
                                             vmem_limit_bytes=VMEM_LIMIT),
        name="mem_kv",
    )(mem2, g, wkt, wv)


def _mixer_kernel(tiles_per_seq, x_ref, z_ref, halo_ref, gates_ref, o_ref,
                  cw_ref, cb_ref, lng_ref, lnb_ref, wco_ref, wmo_ref, wout_ref,
                  xg_ref, wxq_ref, mkt_ref, mv_ref, wxo_ref,
                  h_ref, zext_ref, shift_ref, conv_ref):
    tm = x_ref.shape[0]
    first = (pl.program_id(0) % tiles_per_seq) == 0
    zext_ref[0:HALO, :] = jnp.where(first, 0.0, halo_ref[...].astype(F32))
    zext_ref[HALO:, :] = z_ref[...].astype(F32)

    rows = 64
    off = HALO - (CONV_WIDTH - 1)
    win = CONV_CHUNK + HALO
    for r0 in range(0, tm, CONV_CHUNK):
        for s in range(1, SUBLANES):
            shift_ref[s - 1, 0:win - SUBLANES, :] = zext_ref[r0 + s:r0 + s + win - SUBLANES, :]
        for c in range(CONV_CH // LANES):
            cs = slice(c * LANES, (c + 1) * LANES)
            for r in range(r0, r0 + CONV_CHUNK, rows):
                acc = jnp.broadcast_to(cb_ref[:, cs], (rows, LANES))
                for w in range(CONV_WIDTH):
                    shift, base = (off + w) % SUBLANES, (off + w) // SUBLANES * SUBLANES
                    if shift == 0:
                        window = zext_ref[r + base:r + base + rows, cs]
                    else:
                        window = shift_ref[shift - 1, r - r0 + base:r - r0 + base + rows, cs]
                    acc = acc + window * cw_ref[w:w + 1, cs]
                conv_ref[r:r + rows, cs] = acc

    y = conv_ref[...]
    mu = jnp.mean(y, axis=-1, keepdims=True)
    yc = y - mu
    var = jnp.mean(yc * yc, axis=-1, keepdims=True)
    y = yc * lax.rsqrt(var + EPS) * lng_ref[...] + lnb_ref[...]
    y = y * _sigmoid(y)
    conv_out = _dot(y.astype(BF16), wco_ref[...])
    mla_out = lax.dot_general(o_ref[...], wmo_ref[...], (((0,), (0,)), ((), ())),
                              preferred_element_type=F32)
    gates = gates_ref[...].astype(F32)
    merged = gates[:, :D_MODEL] * conv_out + gates[:, D_MODEL:] * mla_out
    h1 = x_ref[...] + _dot(merged.astype(BF16), wout_ref[...])

    u = _rms(h1, xg_ref[...]).astype(BF16)
    q = (_dot(u, wxq_ref[...]) * (X_HEAD_DIM ** -0.5)).astype(BF16)
    heads = []
    for h in range(X_HEADS):
        hs = slice(h * X_HEAD_DIM, (h + 1) * X_HEAD_DIM)
        s = _dot(q[:, hs], mkt_ref[0, hs, :])
        p = jnp.exp(s - jnp.max(s, axis=-1, keepdims=True))
        p = p / jnp.sum(p, axis=-1, keepdims=True)
        heads.append(_dot(p.astype(BF16), mv_ref[0, :, hs]))
    xo = jnp.concatenate(heads, axis=1).astype(BF16)
    h_ref[...] = h1 + _dot(xo, wxo_ref[...])


def _mixer(x2, z, gates, o, cw, cb, lng, lnb, wco, wmo, wout, xg, wxq, mkt, mv, wxo, seq):
    t = x2.shape[0]
    tm = TOKEN_TILE
    tiles_per_seq = seq // tm
    halo_per_tile = tm // HALO
    mem_len = mv.shape[1]
    xw = X_HEADS * X_HEAD_DIM
    full = lambda arr: pl.BlockSpec(arr.shape, lambda i: (0,) * arr.ndim)
    row = lambda w: pl.BlockSpec((tm, w), lambda i: (i, 0))
    in_specs = [row(D_MODEL), row(CONV_CH),
                pl.BlockSpec((HALO, CONV_CH), lambda i: (jnp.maximum(i * halo_per_tile - 1, 0), 0)),
                row(2 * D_MODEL), pl.BlockSpec((MLA_HEADS * V_DIM, tm), lambda i: (0, i)),
                full(cw), full(cb), full(lng), full(lnb), full(wco), full(wmo), full(wout),
                full(xg), full(wxq),
                pl.BlockSpec((1, xw, mem_len), lambda i: (i // tiles_per_seq, 0, 0)),
                pl.BlockSpec((1, mem_len, xw), lambda i: (i // tiles_per_seq, 0, 0)),
                full(wxo)]
    return pl.pallas_call(
        functools.partial(_mixer_kernel, tiles_per_seq),
        grid=(t // tm,),
        in_specs=in_specs,
        out_specs=row(D_MODEL),
        out_shape=jax.ShapeDtypeStruct((t, D_MODEL), F32),
        scratch_shapes=[pltpu.VMEM((tm + HALO, CONV_CH), F32),
                        pltpu.VMEM((SUBLANES - 1, CONV_CHUNK + HALO, CONV_CH), F32),
                        pltpu.VMEM((tm, CONV_CH), F32)],
        compiler_params=pltpu.CompilerParams(dimension_semantics=("parallel",),
                                             vmem_limit_bytes=VMEM_LIMIT),
        name="mixer",
    )(x2, z, z, gates, o, cw, cb, lng, lnb, wco, wmo, wout, xg, wxq, mkt, mv, wxo)


def _mlp_kernel(h_ref, g_ref, w1_ref, w2_ref, fg_ref, y_ref):
    h = h_ref[...]
    u = _rms(h, g_ref[...]).astype(BF16)
    chunk = 1024
    acc = h
    for c in range(0, D_FF, chunk):
        a = jnp.maximum(_dot(u, w1_ref[:, c:c + chunk]), 0.0)
        acc = acc + _dot((a * a).astype(BF16), w2_ref[c:c + chunk, :])
    y_ref[...] = _rms(acc, fg_ref[...])


def _mlp(h, g, w1, w2, fg):
    t = h.shape[0]
    tm = MLP_TILE
    full = lambda arr: pl.BlockSpec(arr.shape, lambda i: (0,) * arr.ndim, pipeline_mode=pl.Buffered(1))
    row = pl.BlockSpec((tm, D_MODEL), lambda i: (i, 0))
    return pl.pallas_call(
        _mlp_kernel,
        grid=(t // tm,),
        in_specs=[row, full(g), full(w1), full(w2), full(fg)],
        out_specs=row,
        out_shape=jax.ShapeDtypeStruct((t, D_MODEL), F32),
        compiler_params=pltpu.CompilerParams(dimension_semantics=("parallel",),
                                             vmem_limit_bytes=VMEM_LIMIT),
        name="mlp",
    )(h, g, w1, w2, fg)


def _rope_placement():
    e = np.zeros((HEAD_PAD, MLA_HEADS * HEAD_PAD), np.float32)
    for h in range(MLA_HEADS):
        for j in range(QK_ROPE):
            e[j, h * HEAD_PAD + QK_NOPE + j] = 1.0
    return jnp.asarray(e, BF16)


def kernel(x, mem, positions, norm_mix_g, w_in, conv_w, conv_b, conv_ln_g, conv_ln_b, w_conv_out, q_norm_g, w_uq, kv_norm_g, w_ukv, w_mla_out, w_out, norm_xattn_g, norm_mem_g, w_xq, w_xkv, w_xo, norm_mlp_g, w_mlp1, w_mlp2, final_norm_g):
    batch, seq, _ = x.shape
    mem_len = mem.shape[1]
    t = batch * seq
    assert w_in.shape[0] == 1, "single-layer block (the final rmsnorm is fused into the MLP kernel)"
    assert seq % IN_TILE == 0 and seq % KV_CHUNK == 0 and seq % TOKEN_TILE == 0 and t % ROPE_TILE == 0 and t % MLP_TILE == 0

    inv_freq = ROPE_THETA ** (-jnp.arange(ROPE_HALF, dtype=F32) / ROPE_HALF)
    cos_t, sin_t, cs = _rope_tables(positions.reshape(1, t), inv_freq.reshape(ROPE_HALF, 1))
    place = _rope_placement()
    row = lambda v: v.reshape(1, -1)

    h = x.reshape(t, D_MODEL)
    wi = _regroup_w_in(jnp.swapaxes(w_in, 1, 2))
    wuq = w_uq[0].astype(BF16).reshape(Q_LORA, MLA_HEADS, QK_NOPE + QK_ROPE)
    wqt = jnp.pad(wuq, ((0, 0), (0, 0), (0, HEAD_PAD - QK_NOPE - QK_ROPE))).reshape(Q_LORA, -1).T
    wukv = w_ukv[0].astype(BF16).reshape(KV_LORA, MLA_HEADS, QK_NOPE + V_DIM)
    wk = jnp.pad(wukv[:, :, :QK_NOPE], ((0, 0), (0, 0), (0, HEAD_PAD - QK_NOPE))).reshape(KV_LORA, -1)
    wvt = jnp.pad(wukv[:, :, QK_NOPE:], ((0, 0), (0, 0), (0, V_ROWS - V_DIM))).reshape(KV_LORA, -1).T
    ones = np.zeros((MLA_HEADS, V_ROWS, 1), np.float32)
    ones[:, V_DIM] = 1.0
    ones = jnp.asarray(ones.reshape(MLA_HEADS * V_ROWS, 1))

    z, gates, qt, kp, vt3 = _in_proj(
        h, row(norm_mix_g[0]), wi, row(q_norm_g[0]), row(kv_norm_g[0]), wqt, wk, wvt, ones, place,
        cos_t, sin_t, cs)

    k3 = kp.reshape(t // KV_CHUNK, KV_CHUNK, MLA_HEADS * HEAD_PAD)
    o = _attention(qt, k3, vt3, batch, seq)

    xw = X_HEADS * X_HEAD_DIM
    wxkv = w_xkv[0].astype(BF16)
    mkt, mv = _mem_kv(mem.reshape(batch * mem_len, D_MODEL), row(norm_mem_g[0]),
                      wxkv[:, :xw].T, wxkv[:, xw:], batch, mem_len)

    cw = jnp.pad(conv_w[0], ((0, HALO - CONV_WIDTH), (0, 0)))
    h = _mixer(h, z, gates, o, cw, row(conv_b[0]), row(conv_ln_g[0]), row(conv_ln_b[0]),
               w_conv_out[0].astype(BF16), w_mla_out[0].astype(BF16), w_out[0].astype(BF16),
               row(norm_xattn_g[0]), w_xq[0].astype(BF16), mkt, mv, w_xo[0].astype(BF16), seq)
    h = _mlp(h, row(norm_mlp_g[0]), w_mlp1[0].astype(BF16), w_mlp2[0].astype(BF16),
             row(final_norm_g))
    return h.reshape(batch, seq, D_MODEL)
```

```python
import functools

import numpy as np
import jax
import jax.numpy as jnp
from jax import lax
from jax.experimental import pallas as pl
from jax.experimental.pallas import tpu as pltpu

D_MODEL = 1024
CONV_CH = D_MODEL // 2
CONV_WIDTH = 31
MLA_HEADS = 8
QK_NOPE = D_MODEL // 16
QK_ROPE = D_MODEL // 32
V_DIM = D_MODEL // 16
Q_LORA = 3 * D_MODEL // 8
KV_LORA = D_MODEL // 4
X_HEADS = 4
X_HEAD_DIM = D_MODEL // 8
D_FF = 4 * D_MODEL
ROPE_THETA = 10000.0
EPS = 1e-6

LANES = 128
SUBLANES = 8
BF16_ROWS = 2 * SUBLANES
HEAD_PAD = LANES
V_ROWS = V_DIM + BF16_ROWS
ATTN_HEADS = 8
ROPE_HALF = QK_ROPE // 2
HALO = 32
TOKEN_TILE = 1024
CONV_CHUNK = 256
MLP_TILE = 512
KV_CHUNK = 512
IN_TILE = 2 * KV_CHUNK
Q_TILE = KV_CHUNK
ROPE_TILE = 2048
LOG2E = 1.4426950408889634


def _column_ranges(widths):
    out, start = {}, 0
    for name, width in widths:
        out[name] = (start, start + width)
        start += width
    return out


IN_COLS = _column_ranges([("glu_a", CONV_CH), ("glu_g", CONV_CH), ("gates", 2 * D_MODEL),
                          ("c_q", Q_LORA), ("c_kv", KV_LORA), ("k_rope", HEAD_PAD)])
MASK_VALUE = -1e30
VMEM_LIMIT = 56 * 1024 * 1024

F32 = jnp.float32
BF16 = jnp.bfloat16


def _rms(x, g):
    return x * lax.rsqrt(jnp.mean(x * x, axis=-1, keepdims=True) + EPS) * g


def _sigmoid(x):
    return 1.0 / (1.0 + jnp.exp(-x))


def _dot(a, b):
    return jnp.dot(a, b, preferred_element_type=F32)


def _dot_nt(a, b):
    return lax.dot_general(a, b, (((1,), (1,)), ((), ())), preferred_element_type=F32)


def _rope_kernel(pos_ref, invf_ref, cos_ref, sin_ref, cs_ref):
    ang = pos_ref[...].astype(F32) * invf_ref[...]
    c = jnp.cos(ang)
    s = jnp.sin(ang)
    cos_ref[...] = c
    sin_ref[...] = s
    pad = jnp.zeros((HEAD_PAD - 2 * QK_ROPE, ang.shape[1]), F32)
    cs_ref[...] = jnp.concatenate([c, c, -s, s, pad], axis=0).T


def _rope_tables(pos_row, inv_freq_col):
    t = pos_row.shape[1]
    out = jax.ShapeDtypeStruct((ROPE_HALF, t), F32)
    spec = pl.BlockSpec((ROPE_HALF, ROPE_TILE), lambda i: (0, i))
    return pl.pallas_call(
        _rope_kernel,
        grid=(t // ROPE_TILE,),
        in_specs=[pl.BlockSpec((1, ROPE_TILE), lambda i: (0, i)),
                  pl.BlockSpec((ROPE_HALF, 1), lambda i: (0, 0))],
        out_specs=[spec, spec, pl.BlockSpec((ROPE_TILE, HEAD_PAD), lambda i: (i, 0))],
        out_shape=[out, out, jax.ShapeDtypeStruct((t, HEAD_PAD), F32)],
        name="rope_tables",
    )(pos_row, inv_freq_col)


def _regroup_kernel(w_ref, o_ref):
    c0 = 2 * CONV_CH
    c2 = c0 + Q_LORA + KV_LORA
    c3 = c2 + QK_ROPE

    def put(name, value):
        o_ref[IN_COLS[name][0]:IN_COLS[name][0] + value.shape[0], :] = value.astype(BF16)

    put("glu_a", w_ref[0, :CONV_CH, :])
    put("glu_g", w_ref[0, CONV_CH:c0, :])
    put("gates", w_ref[0, c3:, :])
    put("c_q", w_ref[0, c0:c0 + Q_LORA, :])
    put("c_kv", w_ref[0, c0 + Q_LORA:c2, :])
    kr = w_ref[0, c2:c3, :]
    zeros = jnp.zeros((HEAD_PAD - 2 * QK_ROPE, kr.shape[1]), F32)
    put("k_rope", jnp.concatenate([kr, kr[ROPE_HALF:], kr[:ROPE_HALF], zeros], axis=0))


def _regroup_w_in(wt):
    lanes = 256
    rows = IN_COLS["k_rope"][1]
    return pl.pallas_call(
        _regroup_kernel,
        grid=(wt.shape[2] // lanes,),
        in_specs=[pl.BlockSpec((1, wt.shape[1], lanes), lambda i: (0, 0, i))],
        out_specs=pl.BlockSpec((rows, lanes), lambda i: (0, i)),
        out_shape=jax.ShapeDtypeStruct((rows, wt.shape[2]), BF16),
        name="regroup_w_in",
    )(wt)


def _in_proj_kernel(x_ref, g_ref, w_ref, qg_ref, kvg_ref, wqt_ref, wk_ref, wvt_ref, ones_ref, place_ref,
                    cos_ref, sin_ref, cs_ref,
                    z_ref, gates_ref, qt_ref, k_ref, vt_ref):
    u = _rms(x_ref[...], g_ref[...]).astype(BF16)
    proj = lambda name: _dot_nt(u, w_ref[IN_COLS[name][0]:IN_COLS[name][1], :])
    z_ref[...] = (proj("glu_a") * _sigmoid(proj("glu_g"))).astype(BF16)
    gates_ref[...] = _sigmoid(proj("gates")).astype(BF16)

    cqn = _rms(proj("c_q"), qg_ref[...]).astype(BF16)
    qt = _dot_nt(wqt_ref[...], cqn)
    scale = (QK_NOPE + QK_ROPE) ** -0.5 * LOG2E
    c = cos_ref[...]
    s = sin_ref[...]
    for h in range(MLA_HEADS):
        b = h * HEAD_PAD
        r1 = b + QK_NOPE
        r2 = r1 + ROPE_HALF
        r3 = r2 + ROPE_HALF
        t1 = qt[r1:r2]
        t2 = qt[r2:r3]
        qt_ref[b:r1, :] = (qt[b:r1] * scale).astype(BF16)
        qt_ref[r1:r2, :] = ((t1 * c - t2 * s) * scale).astype(BF16)
        qt_ref[r2:r3, :] = ((t2 * c + t1 * s) * scale).astype(BF16)
        qt_ref[r3:b + HEAD_PAD, :] = (qt[r3:b + HEAD_PAD] * scale).astype(BF16)

    ckvn = _rms(proj("c_kv"), kvg_ref[...]).astype(BF16)
    t = proj("k_rope") * cs_ref[...]
    rot = t + pltpu.roll(t, HEAD_PAD - QK_ROPE, 1)
    k_ref[...] = (_dot(ckvn, wk_ref[...]) + _dot(rot.astype(BF16), place_ref[...])).astype(BF16)
    vt = (_dot_nt(wvt_ref[...], ckvn) + ones_ref[...]).astype(BF16)
    for c in range(vt_ref.shape[0]):
        vt_ref[c] = vt[:, c * KV_CHUNK:(c + 1) * KV_CHUNK]


def _in_proj(x2, g, w, qg, kvg, wqt, wk, wvt, ones, place, cos_t, sin_t, cs):
    t = x2.shape[0]
    tm = IN_TILE
    chunks = tm // KV_CHUNK
    n = t // tm
    full = lambda arr: pl.BlockSpec(arr.shape, lambda i: (0,) * arr.ndim)
    row = lambda w: pl.BlockSpec((tm, w), lambda i: (i, 0))
    in_specs = [row(D_MODEL), full(g), full(w), full(qg), full(kvg), full(wqt), full(wk), full(wvt),
                full(ones), full(place),
                pl.BlockSpec((ROPE_HALF, tm), lambda i: (0, i)),
                pl.BlockSpec((ROPE_HALF, tm), lambda i: (0, i)),
                row(HEAD_PAD)]
    out_shape = [jax.ShapeDtypeStruct((t, CONV_CH), BF16),
                 jax.ShapeDtypeStruct((t, 2 * D_MODEL), BF16),
                 jax.ShapeDtypeStruct((MLA_HEADS * HEAD_PAD, t), BF16),
                 jax.ShapeDtypeStruct((t, MLA_HEADS * HEAD_PAD), BF16),
                 jax.ShapeDtypeStruct((n * chunks, MLA_HEADS * V_ROWS, KV_CHUNK), BF16)]
    out_specs = [row(CONV_CH), row(2 * D_MODEL),
                 pl.BlockSpec((MLA_HEADS * HEAD_PAD, tm), lambda i: (0, i)),
                 row(MLA_HEADS * HEAD_PAD),
                 pl.BlockSpec((chunks, MLA_HEADS * V_ROWS, KV_CHUNK), lambda i: (i, 0, 0))]
    return pl.pallas_call(
        _in_proj_kernel,
        grid=(n,),
        in_specs=in_specs,
        out_specs=out_specs,
        out_shape=out_shape,
        compiler_params=pltpu.CompilerParams(dimension_semantics=("parallel",),
                                             vmem_limit_bytes=VMEM_LIMIT),
        name="in_proj",
    )(x2, g, w, qg, kvg, wqt, wk, wvt, ones, place, cos_t, sin_t, cs)


def _attn_kernel(qt_ref, k_ref, vt_ref, o_ref, s_scr, acc_scr):
    qi = pl.program_id(2)
    tk = k_ref.shape[1]
    tq = qt_ref.shape[1]
    assert tq == tk
    heads = range(ATTN_HEADS)
    half = tk // 2
    n_items = qi + 1
    causal = (lax.broadcasted_iota(jnp.int32, (tk, tq), 0) <= lax.broadcasted_iota(jnp.int32, (tk, tq), 1))
    qts = [qt_ref[h * HEAD_PAD:(h + 1) * HEAD_PAD, :] for h in heads]

    def scores_to(slot, kc, diagonal, hs=heads):
        ms = []
        for h in hs:
            k = k_ref[kc, :, h * HEAD_PAD:(h + 1) * HEAD_PAD]
            if not diagonal:
                s = _dot(k, qts[h])
                s_scr[slot, h] = s
                ms.append(jnp.max(s, axis=0, keepdims=True))
            else:
                top = jnp.where(causal[:half], _dot(k[:half], qts[h]), MASK_VALUE)
                low = jnp.where(causal[half:, half:], _dot(k[half:], qts[h][:, half:]), MASK_VALUE)
                s_scr[slot, h, :half, :] = top
                s_scr[slot, h, half:, half:] = low
                s_scr[slot, h, half:, :half] = jnp.full((tk - half, half), MASK_VALUE, F32)
                low_max = jnp.concatenate([jnp.full((1, half), MASK_VALUE, F32),
                                           jnp.max(low, axis=0, keepdims=True)], axis=1)
                ms.append(jnp.maximum(jnp.max(top, axis=0, keepdims=True), low_max))
        return tuple(ms)

    def accumulate(m_run, slot, kc, ms, hs=heads):
        new = []
        for h, m_old, m_chunk in zip(hs, m_run, ms):
            p = jnp.concatenate([jnp.exp2(s_scr[slot, h, r:r + BF16_ROWS, :] - m_chunk).astype(BF16)
                                 for r in range(0, tk, BF16_ROWS)], axis=0)
            pv = _dot(vt_ref[kc, h * V_ROWS:(h + 1) * V_ROWS, :], p)
            m_new = jnp.maximum(m_old, m_chunk)
            a = jnp.exp2(m_old - m_new)
            b = jnp.exp2(m_chunk - m_new)
            acc_scr[h] = a * acc_scr[h] + b * pv
            new.append(m_new)
        return tuple(new)

    def step(cur, look_ahead, j, carry):
        m_run, kc, ms = carry
        new_run, ahead_m = [], []
        for h in heads:
            if look_ahead:
                ahead_m += scores_to(1 - cur, j, False, (h,))
            new_run += accumulate(m_run[h:h + 1], cur, kc, ms[h:h + 1], (h,))
        return tuple(new_run), j, (tuple(ahead_m) if look_ahead else ms)

    branches = [functools.partial(step, cur, look_ahead) for look_ahead in (True, False) for cur in (0, 1)]

    def body(j, carry):
        return lax.switch(2 * (j == n_items - 1).astype(jnp.int32) + j % 2, branches, j, carry)

    acc_scr[...] = jnp.zeros(acc_scr.shape, F32)
    m_init = tuple(jnp.full((1, tq), MASK_VALUE, F32) for _ in heads)
    lax.fori_loop(0, n_items, body, (m_init, qi, scores_to(0, qi, True)))
    outs = [acc_scr[h, :V_DIM, :] / acc_scr[h, V_DIM:V_DIM + 1, :] for h in heads]
    o_ref[...] = jnp.concatenate(outs, axis=0).astype(BF16)


def _attention(qt, k3, vt3, batch, seq):
    tq = Q_TILE
    nq = seq // tq
    nk = seq // KV_CHUNK
    t = batch * seq
    return pl.pallas_call(
        _attn_kernel,
        grid=(batch, MLA_HEADS // ATTN_HEADS, nq),
        in_specs=[pl.BlockSpec((ATTN_HEADS * HEAD_PAD, tq), lambda b, hg, qi: (hg, b * nq + qi)),
                  pl.BlockSpec((nk, KV_CHUNK, ATTN_HEADS * HEAD_PAD), lambda b, hg, qi: (b, 0, hg),
                               pipeline_mode=pl.Buffered(1)),
                  pl.BlockSpec((nk, ATTN_HEADS * V_ROWS, KV_CHUNK), lambda b, hg, qi: (b, hg, 0),
                               pipeline_mode=pl.Buffered(1))],
        out_specs=pl.BlockSpec((ATTN_HEADS * V_DIM, tq), lambda b, hg, qi: (hg, b * nq + qi)),
        out_shape=jax.ShapeDtypeStruct((MLA_HEADS * V_DIM, t), BF16),
        scratch_shapes=[pltpu.VMEM((2, ATTN_HEADS, KV_CHUNK, tq), F32),
                        pltpu.VMEM((ATTN_HEADS, V_ROWS, tq), F32)],
        compiler_params=pltpu.CompilerParams(
            dimension_semantics=("parallel", "parallel", "arbitrary"),
            vmem_limit_bytes=VMEM_LIMIT),
        name="mla_attention",
    )(qt, k3, vt3)


def _mem_kernel(mem_ref, g_ref, wkt_ref, wv_ref, kt_ref, v_ref):
    mn = _rms(mem_ref[...], g_ref[...]).astype(BF16)
    kt_ref[0] = _dot_nt(wkt_ref[...], mn).astype(BF16)
    v_ref[0] = _dot(mn, wv_ref[...]).astype(BF16)


def _mem_kv(mem2, g, wkt, wv, batch, mem_len):
    xw = X_HEADS * X_HEAD_DIM
    full = lambda arr: pl.BlockSpec(arr.shape, lambda b: (0,) * arr.ndim)
    return pl.pallas_call(
        _mem_kernel,
        grid=(batch,),
        in_specs=[pl.BlockSpec((mem_len, D_MODEL), lambda b: (b, 0)), full(g), full(wkt), full(wv)],
        out_specs=[pl.BlockSpec((1, xw, mem_len), lambda b: (b, 0, 0)),
                   pl.BlockSpec((1, mem_len, xw), lambda b: (b, 0, 0))],
        out_shape=[jax.ShapeDtypeStruct((batch, xw, mem_len), BF16),
                   jax.ShapeDtypeStruct((batch, mem_len, xw), BF16)],
        compiler_params=pltpu.CompilerParams(dimension_semantics=("parallel",),
                                             vmem_limit_bytes=VMEM_LIMIT),
        name="mem_kv",
    )(mem2, g, wkt, wv)


def _mixer_kernel(tiles_per_seq, x_ref, z_ref, halo_ref, gates_ref, o_ref,
                  cw_ref, cb_ref, lng_ref, lnb_ref, wco_ref, wmo_ref, wout_ref,
                  xg_ref, wxq_ref, mkt_ref, mv_ref, wxo_ref,
                  h_ref, zext_ref, shift_ref, conv_ref):
    tm = x_ref.shape[0]
    first = (pl.program_id(0) % tiles_per_seq) == 0
    zext_ref[0:HALO, :] = jnp.where(first, 0.0, halo_ref[...].astype(F32))
    zext_ref[HALO:, :] = z_ref[...].astype(F32)

    rows = 64
    off = HALO - (CONV_WIDTH - 1)
    win = CONV_CHUNK + HALO
    for r0 in range(0, tm, CONV_CHUNK):
        for s in range(1, SUBLANES):
            shift_ref[s - 1, 0:win - SUBLANES, :] = zext_ref[r0 + s:r0 + s + win - SUBLANES, :]
        for c in range(CONV_CH // LANES):
            cs = slice(c * LANES, (c + 1) * LANES)
            for r in range(r0, r0 + CONV_CHUNK, rows):
                acc = jnp.broadcast_to(cb_ref[:, cs], (rows, LANES))
                for w in range(CONV_WIDTH):
                    shift, base = (off + w) % SUBLANES, (off + w) // SUBLANES * SUBLANES
                    if shift == 0:
                        window = zext_ref[r + base:r + base + rows, cs]
                    else:
                        window = shift_ref[shift - 1, r - r0 + base:r - r0 + base + rows, cs]
                    acc = acc + window * cw_ref[w:w + 1, cs]
                conv_ref[r:r + rows, cs] = acc

    y = conv_ref[...]
    mu = jnp.mean(y, axis=-1, keepdims=True)
    yc = y - mu
    var = jnp.mean(yc * yc, axis=-1, keepdims=True)
    y = yc * lax.rsqrt(var + EPS) * lng_ref[...] + lnb_ref[...]
    y = y * _sigmoid(y)
    conv_out = _dot(y.astype(BF16), wco_ref[...])
    mla_out = lax.dot_general(o_ref[...], wmo_ref[...], (((0,), (0,)), ((), ())),
                              preferred_element_type=F32)
    gates = gates_ref[...].astype(F32)
    merged = gates[:, :D_MODEL] * conv_out + gates[:, D_MODEL:] * mla_out
    h1 = x_ref[...] + _dot(merged.astype(BF16), wout_ref[...])

    u = _rms(h1, xg_ref[...]).astype(BF16)
    q = (_dot(u, wxq_ref[...]) * (X_HEAD_DIM ** -0.5)).astype(BF16)
    heads = []
    for h in range(X_HEADS):
        hs = slice(h * X_HEAD_DIM, (h + 1) * X_HEAD_DIM)
        s = _dot(q[:, hs], mkt_ref[0, hs, :])
        p = jnp.exp(s - jnp.max(s, axis=-1, keepdims=True))
        p = p / jnp.sum(p, axis=-1, keepdims=True)
        heads.append(_dot(p.astype(BF16), mv_ref[0, :, hs]))
    xo = jnp.concatenate(heads, axis=1).astype(BF16)
    h_ref[...] = h1 + _dot(xo, wxo_ref[...])


def _mixer(x2, z, gates, o, cw, cb, lng, lnb, wco, wmo, wout, xg, wxq, mkt, mv, wxo, seq):
    t = x2.shape[0]
    tm = TOKEN_TILE
    tiles_per_seq = seq // tm
    halo_per_tile = tm // HALO
    mem_len = mv.shape[1]
    xw = X_HEADS * X_HEAD_DIM
    full = lambda arr: pl.BlockSpec(arr.shape, lambda i: (0,) * arr.ndim)
    row = lambda w: pl.BlockSpec((tm, w), lambda i: (i, 0))
    in_specs = [row(D_MODEL), row(CONV_CH),
                pl.BlockSpec((HALO, CONV_CH), lambda i: (jnp.maximum(i * halo_per_tile - 1, 0), 0)),
                row(2 * D_MODEL), pl.BlockSpec((MLA_HEADS * V_DIM, tm), lambda i: (0, i)),
                full(cw), full(cb), full(lng), full(lnb), full(wco), full(wmo), full(wout),
                full(xg), full(wxq),
                pl.BlockSpec((1, xw, mem_len), lambda i: (i // tiles_per_seq, 0, 0)),
                pl.BlockSpec((1, mem_len, xw), lambda i: (i // tiles_per_seq, 0, 0)),
                full(wxo)]
    return pl.pallas_call(
        functools.partial(_mixer_kernel, tiles_per_seq),
        grid=(t // tm,),
        in_specs=in_specs,
        out_specs=row(D_MODEL),
        out_shape=jax.ShapeDtypeStruct((t, D_MODEL), F32),
        scratch_shapes=[pltpu.VMEM((tm + HALO, CONV_CH), F32),
                        pltpu.VMEM((SUBLANES - 1, CONV_CHUNK + HALO, CONV_CH), F32),
                        pltpu.VMEM((tm, CONV_CH), F32)],
        compiler_params=pltpu.CompilerParams(dimension_semantics=("parallel",),
                                             vmem_limit_bytes=VMEM_LIMIT),
        name="mixer",
    )(x2, z, z, gates, o, cw, cb, lng, lnb, wco, wmo, wout, xg, wxq, mkt, mv, wxo)


def _mlp_kernel(h_ref, g_ref, w1_ref, w2_ref, fg_ref, y_ref):
    h = h_ref[...]
    u = _rms(h, g_ref[...]).astype(BF16)
    chunk = 1024
    acc = h
    for c in range(0, D_FF, chunk):
        a = jnp.maximum(_dot(u, w1_ref[:, c:c + chunk].astype(BF16)), 0.0)
        acc = acc + _dot((a * a).astype(BF16), w2_ref[c:c + chunk, :].astype(BF16))
    y_ref[...] = _rms(acc, fg_ref[...])


def _mlp(h, g, w1, w2, fg):
    t = h.shape[0]
    tm = MLP_TILE
    full = lambda arr: pl.BlockSpec(arr.shape, lambda i: (0,) * arr.ndim, pipeline_mode=pl.Buffered(1))
    row = pl.BlockSpec((tm, D_MODEL), lambda i: (i, 0))
    return pl.pallas_call(
        _mlp_kernel,
        grid=(t // tm,),
        in_specs=[row, full(g), full(w1), full(w2), full(fg)],
        out_specs=row,
        out_shape=jax.ShapeDtypeStruct((t, D_MODEL), F32),
        compiler_params=pltpu.CompilerParams(dimension_semantics=("parallel",),
                                             vmem_limit_bytes=VMEM_LIMIT),
        name="mlp",
    )(h, g, w1, w2, fg)


def _rope_placement():
    e = np.zeros((HEAD_PAD, MLA_HEADS * HEAD_PAD), np.float32)
    for h in range(MLA_HEADS):
        for j in range(QK_ROPE):
            e[j, h * HEAD_PAD + QK_NOPE + j] = 1.0
    return jnp.asarray(e, BF16)


def kernel(x, mem, positions, norm_mix_g, w_in, conv_w, conv_b, conv_ln_g, conv_ln_b, w_conv_out, q_norm_g, w_uq, kv_norm_g, w_ukv, w_mla_out, w_out, norm_xattn_g, norm_mem_g, w_xq, w_xkv, w_xo, norm_mlp_g, w_mlp1, w_mlp2, final_norm_g):
    batch, seq, _ = x.shape
    mem_len = mem.shape[1]
    t = batch * seq
    assert w_in.shape[0] == 1, "single-layer block (the final rmsnorm is fused into the MLP kernel)"
    assert seq % IN_TILE == 0 and seq % KV_CHUNK == 0 and seq % TOKEN_TILE == 0 and t % ROPE_TILE == 0 and t % MLP_TILE == 0

    inv_freq = ROPE_THETA ** (-jnp.arange(ROPE_HALF, dtype=F32) / ROPE_HALF)
    cos_t, sin_t, cs = _rope_tables(positions.reshape(1, t), inv_freq.reshape(ROPE_HALF, 1))
    place = _rope_placement()
    row = lambda v: v.reshape(1, -1)

    h = x.reshape(t, D_MODEL)
    wi = _regroup_w_in(jnp.swapaxes(w_in, 1, 2))
    wuq = w_uq[0].astype(BF16).reshape(Q_LORA, MLA_HEADS, QK_NOPE + QK_ROPE)
    wqt = jnp.pad(wuq, ((0, 0), (0, 0), (0, HEAD_PAD - QK_NOPE - QK_ROPE))).reshape(Q_LORA, -1).T
    wukv = w_ukv[0].astype(BF16).reshape(KV_LORA, MLA_HEADS, QK_NOPE + V_DIM)
    wk = jnp.pad(wukv[:, :, :QK_NOPE], ((0, 0), (0, 0), (0, HEAD_PAD - QK_NOPE))).reshape(KV_LORA, -1)
    wvt = jnp.pad(wukv[:, :, QK_NOPE:], ((0, 0), (0, 0), (0, V_ROWS - V_DIM))).reshape(KV_LORA, -1).T
    ones = np.zeros((MLA_HEADS, V_ROWS, 1), np.float32)
    ones[:, V_DIM] = 1.0
    ones = jnp.asarray(ones.reshape(MLA_HEADS * V_ROWS, 1))

    z, gates, qt, kp, vt3 = _in_proj(
        h, row(norm_mix_g[0]), wi, row(q_norm_g[0]), row(kv_norm_g[0]), wqt, wk, wvt, ones, place,
        cos_t, sin_t, cs)

    k3 = kp.reshape(t // KV_CHUNK, KV_CHUNK, MLA_HEADS * HEAD_PAD)
    o = _attention(qt, k3, vt3, batch, seq)

    xw = X_HEADS * X_HEAD_DIM
    wxkv = w_xkv[0].astype(BF16)
    mkt, mv = _mem_kv(mem.reshape(batch * mem_len, D_MODEL), row(norm_mem_g[0]),
                      wxkv[:, :xw].T, wxkv[:, xw:], batch, mem_len)

    cw = jnp.pad(conv_w[0], ((0, HALO - CONV_WIDTH), (0, 0)))
    h = _mixer(h, z, gates, o, cw, row(conv_b[0]), row(conv_ln_g[0]), row(conv_ln_b[0]),
               w_conv_out[0].astype(BF16), w_mla_out[0].astype(BF16), w_out[0].astype(BF16),
               row(norm_xattn_g[0]), w_xq[0].astype(BF16), mkt, mv, w_xo[0].astype(BF16), seq)
    h = _mlp(h, row(norm_mlp_g[0]), w_mlp1[0], w_mlp2[0],
             row(final_norm_g))
    return h.reshape(batch, seq, D_MODEL)
```

```python
import functools

import numpy as np
import jax
import jax.numpy as jnp
from jax import lax
from jax.experimental import pallas as pl
from jax.experimental.pallas import tpu as pltpu

D_MODEL = 1024
CONV_CH = D_MODEL // 2
CONV_WIDTH = 31
MLA_HEADS = 8
QK_NOPE = D_MODEL // 16
QK_ROPE = D_MODEL // 32
V_DIM = D_MODEL // 16
Q_LORA = 3 * D_MODEL // 8
KV_LORA = D_MODEL // 4
X_HEADS = 4
X_HEAD_DIM = D_MODEL // 8
D_FF = 4 * D_MODEL
ROPE_THETA = 10000.0
EPS = 1e-6

LANES = 128
SUBLANES = 8
BF16_ROWS = 2 * SUBLANES
HEAD_PAD = LANES
V_ROWS = V_DIM + BF16_ROWS
ATTN_HEADS = 8
ROPE_HALF = QK_ROPE // 2
HALO = 32
TOKEN_TILE = 1024
CONV_CHUNK = 256
MLP_TILE = 512
KV_CHUNK = 512
IN_TILE = 2 * KV_CHUNK
Q_TILE = KV_CHUNK
ROPE_TILE = 2048
LOG2E = 1.4426950408889634


def _column_ranges(widths):
    out, start = {}, 0
    for name, width in widths:
        out[name] = (start, start + width)
        start += width
    return out


IN_COLS = _column_ranges([("glu_a", CONV_CH), ("glu_g", CONV_CH), ("c_q", Q_LORA), ("c_kv", KV_LORA),
                          ("k_rope", QK_ROPE), ("gates", 2 * D_MODEL)])
MASK_VALUE = -1e30
VMEM_LIMIT = 56 * 1024 * 1024

F32 = jnp.float32
BF16 = jnp.bfloat16


def _rms(x, g):
    return x * lax.rsqrt(jnp.mean(x * x, axis=-1, keepdims=True) + EPS) * g


def _sigmoid(x):
    return 1.0 / (1.0 + jnp.exp(-x))


def _dot(a, b):
    return jnp.dot(a, b, preferred_element_type=F32)


def _dot_nt(a, b):
    return lax.dot_general(a, b, (((1,), (1,)), ((), ())), preferred_element_type=F32)


def _rope_kernel(pos_ref, invf_ref, cos_ref, sin_ref, cs_ref):
    ang = pos_ref[...].astype(F32) * invf_ref[...]
    c = jnp.cos(ang)
    s = jnp.sin(ang)
    cos_ref[...] = c
    sin_ref[...] = s
    pad = jnp.zeros((HEAD_PAD - 2 * QK_ROPE, ang.shape[1]), F32)
    cs_ref[...] = jnp.concatenate([c, c, -s, s, pad], axis=0).T


def _rope_tables(pos_row, inv_freq_col):
    t = pos_row.shape[1]
    out = jax.ShapeDtypeStruct((ROPE_HALF, t), F32)
    spec = pl.BlockSpec((ROPE_HALF, ROPE_TILE), lambda i: (0, i))
    return pl.pallas_call(
        _rope_kernel,
        grid=(t // ROPE_TILE,),
        in_specs=[pl.BlockSpec((1, ROPE_TILE), lambda i: (0, i)),
                  pl.BlockSpec((ROPE_HALF, 1), lambda i: (0, 0))],
        out_specs=[spec, spec, pl.BlockSpec((ROPE_TILE, HEAD_PAD), lambda i: (i, 0))],
        out_shape=[out, out, jax.ShapeDtypeStruct((t, HEAD_PAD), F32)],
        name="rope_tables",
    )(pos_row, inv_freq_col)


def _in_proj_kernel(x_ref, g_ref, w_ref, qg_ref, kvg_ref, wqt_ref, wk_ref, wvt_ref, ones_ref, place_ref,
                    cos_ref, sin_ref, cs_ref,
                    z_ref, gates_ref, qt_ref, k_ref, vt_ref):
    u = _rms(x_ref[...], g_ref[...]).astype(BF16)
    proj = lambda name: _dot_nt(u, w_ref[0, IN_COLS[name][0]:IN_COLS[name][1], :].astype(BF16))
    z_ref[...] = (proj("glu_a") * _sigmoid(proj("glu_g"))).astype(BF16)
    gates_ref[...] = _sigmoid(proj("gates")).astype(BF16)

    cqn = _rms(proj("c_q"), qg_ref[...]).astype(BF16)
    qt = _dot_nt(wqt_ref[...], cqn)
    scale = (QK_NOPE + QK_ROPE) ** -0.5 * LOG2E
    c = cos_ref[...]
    s = sin_ref[...]
    for h in range(MLA_HEADS):
        b = h * HEAD_PAD
        r1 = b + QK_NOPE
        r2 = r1 + ROPE_HALF
        r3 = r2 + ROPE_HALF
        t1 = qt[r1:r2]
        t2 = qt[r2:r3]
        qt_ref[b:r1, :] = (qt[b:r1] * scale).astype(BF16)
        qt_ref[r1:r2, :] = ((t1 * c - t2 * s) * scale).astype(BF16)
        qt_ref[r2:r3, :] = ((t2 * c + t1 * s) * scale).astype(BF16)
        qt_ref[r3:b + HEAD_PAD, :] = (qt[r3:b + HEAD_PAD] * scale).astype(BF16)

    ckvn = _rms(proj("c_kv"), kvg_ref[...]).astype(BF16)
    kr = w_ref[0, IN_COLS["k_rope"][0]:IN_COLS["k_rope"][1], :]
    kr = jnp.concatenate([kr, kr[ROPE_HALF:], kr[:ROPE_HALF],
                          jnp.zeros((HEAD_PAD - 2 * QK_ROPE, kr.shape[1]), F32)], axis=0).astype(BF16)
    t = _dot_nt(u, kr) * cs_ref[...]
    rot = t + pltpu.roll(t, HEAD_PAD - QK_ROPE, 1)
    k_ref[...] = (_dot(ckvn, wk_ref[...]) + _dot(rot.astype(BF16), place_ref[...])).astype(BF16)
    vt = (_dot_nt(wvt_ref[...], ckvn) + ones_ref[...]).astype(BF16)
    for c in range(vt_ref.shape[0]):
        vt_ref[c] = vt[:, c * KV_CHUNK:(c + 1) * KV_CHUNK]


def _in_proj(x2, g, w, qg, kvg, wqt, wk, wvt, ones, place, cos_t, sin_t, cs):
    t = x2.shape[0]
    tm = IN_TILE
    chunks = tm // KV_CHUNK
    n = t // tm
    full = lambda arr: pl.BlockSpec(arr.shape, lambda i: (0,) * arr.ndim)
    row = lambda w: pl.BlockSpec((tm, w), lambda i: (i, 0))
    in_specs = [row(D_MODEL), full(g), full(w), full(qg), full(kvg), full(wqt), full(wk), full(wvt),
                full(ones), full(place),
                pl.BlockSpec((ROPE_HALF, tm), lambda i: (0, i)),
                pl.BlockSpec((ROPE_HALF, tm), lambda i: (0, i)),
                row(HEAD_PAD)]
    out_shape = [jax.ShapeDtypeStruct((t, CONV_CH), BF16),
                 jax.ShapeDtypeStruct((t, 2 * D_MODEL), BF16),
                 jax.ShapeDtypeStruct((MLA_HEADS * HEAD_PAD, t), BF16),
                 jax.ShapeDtypeStruct((t, MLA_HEADS * HEAD_PAD), BF16),
                 jax.ShapeDtypeStruct((n * chunks, MLA_HEADS * V_ROWS, KV_CHUNK), BF16)]
    out_specs = [row(CONV_CH), row(2 * D_MODEL),
                 pl.BlockSpec((MLA_HEADS * HEAD_PAD, tm), lambda i: (0, i)),
                 row(MLA_HEADS * HEAD_PAD),
                 pl.BlockSpec((chunks, MLA_HEADS * V_ROWS, KV_CHUNK), lambda i: (i, 0, 0))]
    return pl.pallas_call(
        _in_proj_kernel,
        grid=(n,),
        in_specs=in_specs,
        out_specs=out_specs,
        out_shape=out_shape,
        compiler_params=pltpu.CompilerParams(dimension_semantics=("parallel",),
                                             vmem_limit_bytes=VMEM_LIMIT),
        name="in_proj",
    )(x2, g, w, qg, kvg, wqt, wk, wvt, ones, place, cos_t, sin_t, cs)


def _attn_kernel(qt_ref, k_ref, vt_ref, o_ref, s_scr, acc_scr):
    qi = pl.program_id(2)
    tk = k_ref.shape[1]
    tq = qt_ref.shape[1]
    assert tq == tk
    heads = range(ATTN_HEADS)
    half = tk // 2
    n_items = qi + 1
    causal = (lax.broadcasted_iota(jnp.int32, (tk, tq), 0) <= lax.broadcasted_iota(jnp.int32, (tk, tq), 1))
    qts = [qt_ref[h * HEAD_PAD:(h + 1) * HEAD_PAD, :] for h in heads]

    def scores_to(slot, kc, diagonal, hs=heads):
        ms = []
        for h in hs:
            k = k_ref[kc, :, h * HEAD_PAD:(h + 1) * HEAD_PAD]
            if not diagonal:
                s = _dot(k, qts[h])
                s_scr[slot, h] = s
                ms.append(jnp.max(s, axis=0, keepdims=True))
            else:
                top = jnp.where(causal[:half], _dot(k[:half], qts[h]), MASK_VALUE)
                low = jnp.where(causal[half:, half:], _dot(k[half:], qts[h][:, half:]), MASK_VALUE)
                s_scr[slot, h, :half, :] = top
                s_scr[slot, h, half:, half:] = low
                s_scr[slot, h, half:, :half] = jnp.full((tk - half, half), MASK_VALUE, F32)
                low_max = jnp.concatenate([jnp.full((1, half), MASK_VALUE, F32),
                                           jnp.max(low, axis=0, keepdims=True)], axis=1)
                ms.append(jnp.maximum(jnp.max(top, axis=0, keepdims=True), low_max))
        return tuple(ms)

    def accumulate(m_run, slot, kc, ms, hs=heads):
        new = []
        for h, m_old, m_chunk in zip(hs, m_run, ms):
            p = jnp.concatenate([jnp.exp2(s_scr[slot, h, r:r + BF16_ROWS, :] - m_chunk).astype(BF16)
                                 for r in range(0, tk, BF16_ROWS)], axis=0)
            pv = _dot(vt_ref[kc, h * V_ROWS:(h + 1) * V_ROWS, :], p)
            m_new = jnp.maximum(m_old, m_chunk)
            a = jnp.exp2(m_old - m_new)
            b = jnp.exp2(m_chunk - m_new)
            acc_scr[h] = a * acc_scr[h] + b * pv
            new.append(m_new)
        return tuple(new)

    def step(cur, look_ahead, j, carry):
        m_run, kc, ms = carry
        new_run, ahead_m = [], []
        for h in heads:
            if look_ahead:
                ahead_m += scores_to(1 - cur, j, False, (h,))
            new_run += accumulate(m_run[h:h + 1], cur, kc, ms[h:h + 1], (h,))
        return tuple(new_run), j, (tuple(ahead_m) if look_ahead else ms)

    branches = [functools.partial(step, cur, look_ahead) for look_ahead in (True, False) for cur in (0, 1)]

    def body(j, carry):
        return lax.switch(2 * (j == n_items - 1).astype(jnp.int32) + j % 2, branches, j, carry)

    acc_scr[...] = jnp.zeros(acc_scr.shape, F32)
    m_init = tuple(jnp.full((1, tq), MASK_VALUE, F32) for _ in heads)
    lax.fori_loop(0, n_items, body, (m_init, qi, scores_to(0, qi, True)))
    outs = [acc_scr[h, :V_DIM, :] / acc_scr[h, V_DIM:V_DIM + 1, :] for h in heads]
    o_ref[...] = jnp.concatenate(outs, axis=0).astype(BF16)


def _attention(qt, k3, vt3, batch, seq):
    tq = Q_TILE
    nq = seq // tq
    nk = seq // KV_CHUNK
    t = batch * seq
    return pl.pallas_call(
        _attn_kernel,
        grid=(batch, MLA_HEADS // ATTN_HEADS, nq),
        in_specs=[pl.BlockSpec((ATTN_HEADS * HEAD_PAD, tq), lambda b, hg, qi: (hg, b * nq + qi)),
                  pl.BlockSpec((nk, KV_CHUNK, ATTN_HEADS * HEAD_PAD), lambda b, hg, qi: (b, 0, hg),
                               pipeline_mode=pl.Buffered(1)),
                  pl.BlockSpec((nk, ATTN_HEADS * V_ROWS, KV_CHUNK), lambda b, hg, qi: (b, hg, 0),
                               pipeline_mode=pl.Buffered(1))],
        out_specs=pl.BlockSpec((ATTN_HEADS * V_DIM, tq), lambda b, hg, qi: (hg, b * nq + qi)),
        out_shape=jax.ShapeDtypeStruct((MLA_HEADS * V_DIM, t), BF16),
        scratch_shapes=[pltpu.VMEM((2, ATTN_HEADS, KV_CHUNK, tq), F32),
                        pltpu.VMEM((ATTN_HEADS, V_ROWS, tq), F32)],
        compiler_params=pltpu.CompilerParams(
            dimension_semantics=("parallel", "parallel", "arbitrary"),
            vmem_limit_bytes=VMEM_LIMIT),
        name="mla_attention",
    )(qt, k3, vt3)


def _mem_kernel(mem_ref, g_ref, wkt_ref, wv_ref, kt_ref, v_ref):
    mn = _rms(mem_ref[...], g_ref[...]).astype(BF16)
    kt_ref[0] = _dot_nt(wkt_ref[...], mn).astype(BF16)
    v_ref[0] = _dot(mn, wv_ref[...]).astype(BF16)


def _mem_kv(mem2, g, wkt, wv, batch, mem_len):
    xw = X_HEADS * X_HEAD_DIM
    full = lambda arr: pl.BlockSpec(arr.shape, lambda b: (0,) * arr.ndim)
    return pl.pallas_call(
        _mem_kernel,
        grid=(batch,),
        in_specs=[pl.BlockSpec((mem_len, D_MODEL), lambda b: (b, 0)), full(g), full(wkt), full(wv)],
        out_specs=[pl.BlockSpec((1, xw, mem_len), lambda b: (b, 0, 0)),
                   pl.BlockSpec((1, mem_len, xw), lambda b: (b, 0, 0))],
        out_shape=[jax.ShapeDtypeStruct((batch, xw, mem_len), BF16),
                   jax.ShapeDtypeStruct((batch, mem_len, xw), BF16)],
        compiler_params=pltpu.CompilerParams(dimension_semantics=("parallel",),
                                             vmem_limit_bytes=VMEM_LIMIT),
        name="mem_kv",
    )(mem2, g, wkt, wv)


def _mixer_kernel(tiles_per_seq, x_ref, z_ref, halo_ref, gates_ref, o_ref,
                  cw_ref, cb_ref, lng_ref, lnb_ref, wco_ref, wmo_ref, wout_ref,
                  xg_ref, wxq_ref, mkt_ref, mv_ref, wxo_ref,
                  h_ref, zext_ref, shift_ref, conv_ref):
    tm = x_ref.shape[0]
    first = (pl.program_id(0) % tiles_per_seq) == 0
    zext_ref[0:HALO, :] = jnp.where(first, 0.0, halo_ref[...].astype(F32))
    zext_ref[HALO:, :] = z_ref[...].astype(F32)

    rows = 64
    off = HALO - (CONV_WIDTH - 1)
    win = CONV_CHUNK + HALO
    for r0 in range(0, tm, CONV_CHUNK):
        for s in range(1, SUBLANES):
            shift_ref[s - 1, 0:win - SUBLANES, :] = zext_ref[r0 + s:r0 + s + win - SUBLANES, :]
        for c in range(CONV_CH // LANES):
            cs = slice(c * LANES, (c + 1) * LANES)
            for r in range(r0, r0 + CONV_CHUNK, rows):
                acc = jnp.broadcast_to(cb_ref[:, cs], (rows, LANES))
                for w in range(CONV_WIDTH):
                    shift, base = (off + w) % SUBLANES, (off + w) // SUBLANES * SUBLANES
                    if shift == 0:
                        window = zext_ref[r + base:r + base + rows, cs]
                    else:
                        window = shift_ref[shift - 1, r - r0 + base:r - r0 + base + rows, cs]
                    acc = acc + window * cw_ref[w:w + 1, cs]
                conv_ref[r:r + rows, cs] = acc

    y = conv_ref[...]
    mu = jnp.mean(y, axis=-1, keepdims=True)
    yc = y - mu
    var = jnp.mean(yc * yc, axis=-1, keepdims=True)
    y = yc * lax.rsqrt(var + EPS) * lng_ref[...] + lnb_ref[...]
    y = y * _sigmoid(y)
    conv_out = _dot(y.astype(BF16), wco_ref[...])
    mla_out = lax.dot_general(o_ref[...], wmo_ref[...], (((0,), (0,)), ((), ())),
                              preferred_element_type=F32)
    gates = gates_ref[...].astype(F32)
    merged = gates[:, :D_MODEL] * conv_out + gates[:, D_MODEL:] * mla_out
    h1 = x_ref[...] + _dot(merged.astype(BF16), wout_ref[...])

    u = _rms(h1, xg_ref[...]).astype(BF16)
    q = (_dot(u, wxq_ref[...]) * (X_HEAD_DIM ** -0.5)).astype(BF16)
    heads = []
    for h in range(X_HEADS):
        hs = slice(h * X_HEAD_DIM, (h + 1) * X_HEAD_DIM)
        s = _dot(q[:, hs], mkt_ref[0, hs, :])
        p = jnp.exp(s - jnp.max(s, axis=-1, keepdims=True))
        p = p / jnp.sum(p, axis=-1, keepdims=True)
        heads.append(_dot(p.astype(BF16), mv_ref[0, :, hs]))
    xo = jnp.concatenate(heads, axis=1).astype(BF16)
    h_ref[...] = h1 + _dot(xo, wxo_ref[...])


def _mixer(x2, z, gates, o, cw, cb, lng, lnb, wco, wmo, wout, xg, wxq, mkt, mv, wxo, seq):
    t = x2.shape[0]
    tm = TOKEN_TILE
    tiles_per_seq = seq // tm
    halo_per_tile = tm // HALO
    mem_len = mv.shape[1]
    xw = X_HEADS * X_HEAD_DIM
    full = lambda arr: pl.BlockSpec(arr.shape, lambda i: (0,) * arr.ndim)
    row = lambda w: pl.BlockSpec((tm, w), lambda i: (i, 0))
    in_specs = [row(D_MODEL), row(CONV_CH),
                pl.BlockSpec((HALO, CONV_CH), lambda i: (jnp.maximum(i * halo_per_tile - 1, 0), 0)),
                row(2 * D_MODEL), pl.BlockSpec((MLA_HEADS * V_DIM, tm), lambda i: (0, i)),
                full(cw), full(cb), full(lng), full(lnb), full(wco), full(wmo), full(wout),
                full(xg), full(wxq),
                pl.BlockSpec((1, xw, mem_len), lambda i: (i // tiles_per_seq, 0, 0)),
                pl.BlockSpec((1, mem_len, xw), lambda i: (i // tiles_per_seq, 0, 0)),
                full(wxo)]
    return pl.pallas_call(
        functools.partial(_mixer_kernel, tiles_per_seq),
        grid=(t // tm,),
        in_specs=in_specs,
        out_specs=row(D_MODEL),
        out_shape=jax.ShapeDtypeStruct((t, D_MODEL), F32),
        scratch_shapes=[pltpu.VMEM((tm + HALO, CONV_CH), F32),
                        pltpu.VMEM((SUBLANES - 1, CONV_CHUNK + HALO, CONV_CH), F32),
                        pltpu.VMEM((tm, CONV_CH), F32)],
        compiler_params=pltpu.CompilerParams(dimension_semantics=("parallel",),
                                             vmem_limit_bytes=VMEM_LIMIT),
        name="mixer",
    )(x2, z, z, gates, o, cw, cb, lng, lnb, wco, wmo, wout, xg, wxq, mkt, mv, wxo)


def _mlp_kernel(h_ref, g_ref, w1_ref, w2_ref, fg_ref, y_ref):
    h = h_ref[...]
    u = _rms(h, g_ref[...]).astype(BF16)
    chunk = 1024
    acc = h
    for c in range(0, D_FF, chunk):
        a = jnp.maximum(_dot(u, w1_ref[:, c:c + chunk].astype(BF16)), 0.0)
        acc = acc + _dot((a * a).astype(BF16), w2_ref[c:c + chunk, :].astype(BF16))
    y_ref[...] = _rms(acc, fg_ref[...])


def _mlp(h, g, w1, w2, fg):
    t = h.shape[0]
    tm = MLP_TILE
    full = lambda arr: pl.BlockSpec(arr.shape, lambda i: (0,) * arr.ndim, pipeline_mode=pl.Buffered(1))
    row = pl.BlockSpec((tm, D_MODEL), lambda i: (i, 0))
    return pl.pallas_call(
        _mlp_kernel,
        grid=(t // tm,),
        in_specs=[row, full(g), full(w1), full(w2), full(fg)],
        out_specs=row,
        out_shape=jax.ShapeDtypeStruct((t, D_MODEL), F32),
        compiler_params=pltpu.CompilerParams(dimension_semantics=("parallel",),
                                             vmem_limit_bytes=VMEM_LIMIT),
        name="mlp",
    )(h, g, w1, w2, fg)


def _rope_placement():
    e = np.zeros((HEAD_PAD, MLA_HEADS * HEAD_PAD), np.float32)
    for h in range(MLA_HEADS):
        for j in range(QK_ROPE):
            e[j, h * HEAD_PAD + QK_NOPE + j] = 1.0
    return jnp.asarray(e, BF16)


def kernel(x, mem, positions, norm_mix_g, w_in, conv_w, conv_b, conv_ln_g, conv_ln_b, w_conv_out, q_norm_g, w_uq, kv_norm_g, w_ukv, w_mla_out, w_out, norm_xattn_g, norm_mem_g, w_xq, w_xkv, w_xo, norm_mlp_g, w_mlp1, w_mlp2, final_norm_g):
    batch, seq, _ = x.shape
    mem_len = mem.shape[1]
    t = batch * seq
    assert w_in.shape[0] == 1, "single-layer block (the final rmsnorm is fused into the MLP kernel)"
    assert seq % IN_TILE == 0 and seq % KV_CHUNK == 0 and seq % TOKEN_TILE == 0 and t % ROPE_TILE == 0 and t % MLP_TILE == 0

    inv_freq = ROPE_THETA ** (-jnp.arange(ROPE_HALF, dtype=F32) / ROPE_HALF)
    cos_t, sin_t, cs = _rope_tables(positions.reshape(1, t), inv_freq.reshape(ROPE_HALF, 1))
    place = _rope_placement()
    row = lambda v: v.reshape(1, -1)

    h = x.reshape(t, D_MODEL)
    wi = jnp.swapaxes(w_in, 1, 2)
    wuq = w_uq[0].astype(BF16).reshape(Q_LORA, MLA_HEADS, QK_NOPE + QK_ROPE)
    wqt = jnp.pad(wuq, ((0, 0), (0, 0), (0, HEAD_PAD - QK_NOPE - QK_ROPE))).reshape(Q_LORA, -1).T
    wukv = w_ukv[0].astype(BF16).reshape(KV_LORA, MLA_HEADS, QK_NOPE + V_DIM)
    wk = jnp.pad(wukv[:, :, :QK_NOPE], ((0, 0), (0, 0), (0, HEAD_PAD - QK_NOPE))).reshape(KV_LORA, -1)
    wvt = jnp.pad(wukv[:, :, QK_NOPE:], ((0, 0), (0, 0), (0, V_ROWS - V_DIM))).reshape(KV_LORA, -1).T
    ones = np.zeros((MLA_HEADS, V_ROWS, 1), np.float32)
    ones[:, V_DIM] = 1.0
    ones = jnp.asarray(ones.reshape(MLA_HEADS * V_ROWS, 1))

    z, gates, qt, kp, vt3 = _in_proj(
        h, row(norm_mix_g[0]), wi, row(q_norm_g[0]), row(kv_norm_g[0]), wqt, wk, wvt, ones, place,
        cos_t, sin_t, cs)

    k3 = kp.reshape(t // KV_CHUNK, KV_CHUNK, MLA_HEADS * HEAD_PAD)
    o = _attention(qt, k3, vt3, batch, seq)

    xw = X_HEADS * X_HEAD_DIM
    wxkv = w_xkv[0].astype(BF16)
    mkt, mv = _mem_kv(mem.reshape(batch * mem_len, D_MODEL), row(norm_mem_g[0]),
                      wxkv[:, :xw].T, wxkv[:, xw:], batch, mem_len)

    cw = jnp.pad(conv_w[0], ((0, HALO - CONV_WIDTH), (0, 0)))
    h = _mixer(h, z, gates, o, cw, row(conv_b[0]), row(conv_ln_g[0]), row(conv_ln_b[0]),
               w_conv_out[0].astype(BF16), w_mla_out[0].astype(BF16), w_out[0].astype(BF16),
               row(norm_xattn_g[0]), w_xq[0].astype(BF16), mkt, mv, w_xo[0].astype(BF16), seq)
    h = _mlp(h, row(norm_mlp_g[0]), w_mlp1[0], w_mlp2[0],
             row(final_norm_g))
    return h.reshape(batch, seq, D_MODEL)
```

```python
import functools

import numpy as np
import jax
import jax.numpy as jnp
from jax import lax
from jax.experimental import pallas as pl
from jax.experimental.pallas import tpu as pltpu

D_MODEL = 1024
CONV_CH = D_MODEL // 2
CONV_WIDTH = 31
MLA_HEADS = 8
QK_NOPE = D_MODEL // 16
QK_ROPE = D_MODEL // 32
V_DIM = D_MODEL // 16
Q_LORA = 3 * D_MODEL // 8
KV_LORA = D_MODEL // 4
X_HEADS = 4
X_HEAD_DIM = D_MODEL // 8
D_FF = 4 * D_MODEL
ROPE_THETA = 10000.0
EPS = 1e-6

LANES = 128
SUBLANES = 8
BF16_ROWS = 2 * SUBLANES
HEAD_PAD = LANES
V_ROWS = V_DIM + BF16_ROWS
ATTN_HEADS = 8
ROPE_HALF = QK_ROPE // 2
HALO = 32
TOKEN_TILE = 1024
CONV_CHUNK = 256
MLP_TILE = 512
KV_CHUNK = 512
IN_TILE = 2 * KV_CHUNK
Q_TILE = KV_CHUNK
ROPE_TILE = 2048
LOG2E = 1.4426950408889634


def _column_ranges(widths):
    out, start = {}, 0
    for name, width in widths:
        out[name] = (start, start + width)
        start += width
    return out


IN_COLS = _column_ranges([("glu_a", CONV_CH), ("glu_g", CONV_CH), ("c_q", Q_LORA), ("c_kv", KV_LORA),
                          ("k_rope", QK_ROPE), ("gates", 2 * D_MODEL)])
MASK_VALUE = -1e30
VMEM_LIMIT = 56 * 1024 * 1024

F32 = jnp.float32
BF16 = jnp.bfloat16


def _rms(x, g):
    return x * lax.rsqrt(jnp.mean(x * x, axis=-1, keepdims=True) + EPS) * g


def _sigmoid(x):
    return 1.0 / (1.0 + jnp.exp(-x))


def _dot(a, b):
    return jnp.dot(a, b, preferred_element_type=F32)


def _dot_nt(a, b):
    return lax.dot_general(a, b, (((1,), (1,)), ((), ())), preferred_element_type=F32)


def _rope_kernel(pos_ref, invf_ref, cos_ref, sin_ref, cs_ref):
    ang = pos_ref[...].astype(F32) * invf_ref[...]
    c = jnp.cos(ang)
    s = jnp.sin(ang)
    cos_ref[...] = c
    sin_ref[...] = s
    pad = jnp.zeros((HEAD_PAD - 2 * QK_ROPE, ang.shape[1]), F32)
    cs_ref[...] = jnp.concatenate([c, c, -s, s, pad], axis=0).T


def _rope_tables(pos_row, inv_freq_col):
    t = pos_row.shape[1]
    out = jax.ShapeDtypeStruct((ROPE_HALF, t), F32)
    spec = pl.BlockSpec((ROPE_HALF, ROPE_TILE), lambda i: (0, i))
    return pl.pallas_call(
        _rope_kernel,
        grid=(t // ROPE_TILE,),
        in_specs=[pl.BlockSpec((1, ROPE_TILE), lambda i: (0, i)),
                  pl.BlockSpec((ROPE_HALF, 1), lambda i: (0, 0))],
        out_specs=[spec, spec, pl.BlockSpec((ROPE_TILE, HEAD_PAD), lambda i: (i, 0))],
        out_shape=[out, out, jax.ShapeDtypeStruct((t, HEAD_PAD), F32)],
        name="rope_tables",
    )(pos_row, inv_freq_col)


def _in_proj_kernel(x_ref, g_ref, w_ref, qg_ref, kvg_ref, wqt_ref, wk_ref, wvt_ref, ones_ref, place_ref,
                    cos_ref, sin_ref, cs_ref,
                    z_ref, gates_ref, qt_ref, k_ref, vt_ref):
    u = _rms(x_ref[...], g_ref[...]).astype(BF16)
    proj = lambda name: _dot_nt(u, w_ref[0, IN_COLS[name][0]:IN_COLS[name][1], :].astype(BF16))
    z_ref[...] = (proj("glu_a") * _sigmoid(proj("glu_g"))).astype(BF16)
    gates_ref[...] = _sigmoid(proj("gates")).astype(BF16)

    cqn = _rms(proj("c_q"), qg_ref[...]).astype(BF16)
    qt = _dot_nt(wqt_ref[...], cqn)
    scale = (QK_NOPE + QK_ROPE) ** -0.5 * LOG2E
    c = cos_ref[...]
    s = sin_ref[...]
    for h in range(MLA_HEADS):
        b = h * HEAD_PAD
        r1 = b + QK_NOPE
        r2 = r1 + ROPE_HALF
        r3 = r2 + ROPE_HALF
        t1 = qt[r1:r2]
        t2 = qt[r2:r3]
        qt_ref[b:r1, :] = (qt[b:r1] * scale).astype(BF16)
        qt_ref[r1:r2, :] = ((t1 * c - t2 * s) * scale).astype(BF16)
        qt_ref[r2:r3, :] = ((t2 * c + t1 * s) * scale).astype(BF16)
        qt_ref[r3:b + HEAD_PAD, :] = (qt[r3:b + HEAD_PAD] * scale).astype(BF16)

    ckvn = _rms(proj("c_kv"), kvg_ref[...]).astype(BF16)
    kr = w_ref[0, IN_COLS["k_rope"][0]:IN_COLS["k_rope"][1], :]
    kr = jnp.concatenate([kr, kr[ROPE_HALF:], kr[:ROPE_HALF],
                          jnp.zeros((HEAD_PAD - 2 * QK_ROPE, kr.shape[1]), F32)], axis=0).astype(BF16)
    t = _dot_nt(u, kr) * cs_ref[...]
    rot = t + pltpu.roll(t, HEAD_PAD - QK_ROPE, 1)
    k_ref[...] = (_dot(ckvn, wk_ref[...]) + _dot(rot.astype(BF16), place_ref[...])).astype(BF16)
    vt = (_dot_nt(wvt_ref[...], ckvn) + ones_ref[...]).astype(BF16)
    for c in range(vt_ref.shape[0]):
        vt_ref[c] = vt[:, c * KV_CHUNK:(c + 1) * KV_CHUNK]


def _in_proj(x2, g, w, qg, kvg, wqt, wk, wvt, ones, place, cos_t, sin_t, cs):
    t = x2.shape[0]
    tm = IN_TILE
    chunks = tm // KV_CHUNK
    n = t // tm
    full = lambda arr: pl.BlockSpec(arr.shape, lambda i: (0,) * arr.ndim)
    row = lambda w: pl.BlockSpec((tm, w), lambda i: (i, 0))
    in_specs = [row(D_MODEL), full(g), full(w), full(qg), full(kvg), full(wqt), full(wk), full(wvt),
                full(ones), full(place),
                pl.BlockSpec((ROPE_HALF, tm), lambda i: (0, i)),
                pl.BlockSpec((ROPE_HALF, tm), lambda i: (0, i)),
                row(HEAD_PAD)]
    out_shape = [jax.ShapeDtypeStruct((t, CONV_CH), BF16),
                 jax.ShapeDtypeStruct((t, 2 * D_MODEL), BF16),
                 jax.ShapeDtypeStruct((MLA_HEADS * HEAD_PAD, t), BF16),
                 jax.ShapeDtypeStruct((t, MLA_HEADS * HEAD_PAD), BF16),
                 jax.ShapeDtypeStruct((n * chunks, MLA_HEADS * V_ROWS, KV_CHUNK), BF16)]
    out_specs = [row(CONV_CH), row(2 * D_MODEL),
                 pl.BlockSpec((MLA_HEADS * HEAD_PAD, tm), lambda i: (0, i)),
                 row(MLA_HEADS * HEAD_PAD),
                 pl.BlockSpec((chunks, MLA_HEADS * V_ROWS, KV_CHUNK), lambda i: (i, 0, 0))]
    return pl.pallas_call(
        _in_proj_kernel,
        grid=(n,),
        in_specs=in_specs,
        out_specs=out_specs,
        out_shape=out_shape,
        compiler_params=pltpu.CompilerParams(dimension_semantics=("parallel",),
                                             vmem_limit_bytes=VMEM_LIMIT),
        name="in_proj",
    )(x2, g, w, qg, kvg, wqt, wk, wvt, ones, place, cos_t, sin_t, cs)


def _attn_kernel(qt_ref, k_ref, vt_ref, o_ref, s_scr, acc_scr):
    qi = pl.program_id(2)
    tk = k_ref.shape[1]
    tq = qt_ref.shape[1]
    assert tq == tk
    heads = range(ATTN_HEADS)
    half = tk // 2
    n_items = qi + 1
    causal = (lax.broadcasted_iota(jnp.int32, (tk, tq), 0) <= lax.broadcasted_iota(jnp.int32, (tk, tq), 1))
    qts = [qt_ref[h * HEAD_PAD:(h + 1) * HEAD_PAD, :] for h in heads]

    def scores_to(slot, kc, diagonal, hs=heads):
        ms = []
        for h in hs:
            k = k_ref[kc, :, h * HEAD_PAD:(h + 1) * HEAD_PAD]
            if not diagonal:
                s = _dot(k, qts[h])
                s_scr[slot, h] = s
                ms.append(jnp.max(s, axis=0, keepdims=True))
            else:
                top = jnp.where(causal[:half], _dot(k[:half], qts[h]), MASK_VALUE)
                low = jnp.where(causal[half:, half:], _dot(k[half:], qts[h][:, half:]), MASK_VALUE)
                s_scr[slot, h, :half, :] = top
                s_scr[slot, h, half:, half:] = low
                s_scr[slot, h, half:, :half] = jnp.full((tk - half, half), MASK_VALUE, F32)
                low_max = jnp.concatenate([jnp.full((1, half), MASK_VALUE, F32),
                                           jnp.max(low, axis=0, keepdims=True)], axis=1)
                ms.append(jnp.maximum(jnp.max(top, axis=0, keepdims=True), low_max))
        return tuple(ms)

    def accumulate(m_run, slot, kc, ms, hs=heads):
        new = []
        for h, m_old, m_chunk in zip(hs, m_run, ms):
            p = jnp.concatenate([jnp.exp2(s_scr[slot, h, r:r + BF16_ROWS, :] - m_chunk).astype(BF16)
                                 for r in range(0, tk, BF16_ROWS)], axis=0)
            pv = _dot(vt_ref[kc, h * V_ROWS:(h + 1) * V_ROWS, :], p)
            m_new = jnp.maximum(m_old, m_chunk)
            a = jnp.exp2(m_old - m_new)
            b = jnp.exp2(m_chunk - m_new)
            acc_scr[h] = a * acc_scr[h] + b * pv
            new.append(m_new)
        return tuple(new)

    def step(cur, look_ahead, j, carry):
        m_run, kc, ms = carry
        new_run, ahead_m = [], []
        for h in heads:
            if look_ahead:
                ahead_m += scores_to(1 - cur, j, False, (h,))
            new_run += accumulate(m_run[h:h + 1], cur, kc, ms[h:h + 1], (h,))
        return tuple(new_run), j, (tuple(ahead_m) if look_ahead else ms)

    branches = [functools.partial(step, cur, look_ahead) for look_ahead in (True, False) for cur in (0, 1)]

    def body(j, carry):
        return lax.switch(2 * (j == n_items - 1).astype(jnp.int32) + j % 2, branches, j, carry)

    acc_scr[...] = jnp.zeros(acc_scr.shape, F32)
    m_init = tuple(jnp.full((1, tq), MASK_VALUE, F32) for _ in heads)
    lax.fori_loop(0, n_items, body, (m_init, qi, scores_to(0, qi, True)))
    outs = [acc_scr[h, :V_DIM, :] / acc_scr[h, V_DIM:V_DIM + 1, :] for h in heads]
    o_ref[...] = jnp.concatenate(outs, axis=0).astype(BF16)


def _attention(qt, k3, vt3, batch, seq):
    tq = Q_TILE
    nq = seq // tq
    nk = seq // KV_CHUNK
    t = batch * seq
    return pl.pallas_call(
        _attn_kernel,
        grid=(batch, MLA_HEADS // ATTN_HEADS, nq),
        in_specs=[pl.BlockSpec((ATTN_HEADS * HEAD_PAD, tq), lambda b, hg, qi: (hg, b * nq + qi)),
                  pl.BlockSpec((nk, KV_CHUNK, ATTN_HEADS * HEAD_PAD), lambda b, hg, qi: (b, 0, hg),
                               pipeline_mode=pl.Buffered(1)),
                  pl.BlockSpec((nk, ATTN_HEADS * V_ROWS, KV_CHUNK), lambda b, hg, qi: (b, hg, 0),
                               pipeline_mode=pl.Buffered(1))],
        out_specs=pl.BlockSpec((ATTN_HEADS * V_DIM, tq), lambda b, hg, qi: (hg, b * nq + qi)),
        out_shape=jax.ShapeDtypeStruct((MLA_HEADS * V_DIM, t), BF16),
        scratch_shapes=[pltpu.VMEM((2, ATTN_HEADS, KV_CHUNK, tq), F32),
                        pltpu.VMEM((ATTN_HEADS, V_ROWS, tq), F32)],
        compiler_params=pltpu.CompilerParams(
            dimension_semantics=("parallel", "parallel", "arbitrary"),
            vmem_limit_bytes=VMEM_LIMIT),
        name="mla_attention",
    )(qt, k3, vt3)


def _mem_kernel(mem_ref, g_ref, wkt_ref, wv_ref, kt_ref, v_ref):
    mn = _rms(mem_ref[...], g_ref[...]).astype(BF16)
    kt_ref[0] = _dot_nt(wkt_ref[...], mn).astype(BF16)
    v_ref[0] = _dot(mn, wv_ref[...]).astype(BF16)


def _mem_kv(mem2, g, wkt, wv, batch, mem_len):
    xw = X_HEADS * X_HEAD_DIM
    full = lambda arr: pl.BlockSpec(arr.shape, lambda b: (0,) * arr.ndim)
    return pl.pallas_call(
        _mem_kernel,
        grid=(batch,),
        in_specs=[pl.BlockSpec((mem_len, D_MODEL), lambda b: (b, 0)), full(g), full(wkt), full(wv)],
        out_specs=[pl.BlockSpec((1, xw, mem_len), lambda b: (b, 0, 0)),
                   pl.BlockSpec((1, mem_len, xw), lambda b: (b, 0, 0))],
        out_shape=[jax.ShapeDtypeStruct((batch, xw, mem_len), BF16),
                   jax.ShapeDtypeStruct((batch, mem_len, xw), BF16)],
        compiler_params=pltpu.CompilerParams(dimension_semantics=("parallel",),
                                             vmem_limit_bytes=VMEM_LIMIT),
        name="mem_kv",
    )(mem2, g, wkt, wv)


def _mixer_kernel(tiles_per_seq, x_ref, z_ref, halo_ref, gates_ref, o_ref,
                  cw_ref, cb_ref, lng_ref, lnb_ref, wco_ref, wmo_ref, wout_ref,
                  xg_ref, wxq_ref, mkt_ref, mv_ref, wxo_ref,
                  h_ref, zext_ref, shift_ref, conv_ref):
    tm = x_ref.shape[0]
    first = (pl.program_id(0) % tiles_per_seq) == 0
    zext_ref[0:HALO, :] = jnp.where(first, 0.0, halo_ref[...].astype(F32))
    zext_ref[HALO:, :] = z_ref[...].astype(F32)

    rows = 64
    off = HALO - (CONV_WIDTH - 1)
    win = CONV_CHUNK + HALO
    for r0 in range(0, tm, CONV_CHUNK):
        for s in range(1, SUBLANES):
            shift_ref[s - 1, 0:win - SUBLANES, :] = zext_ref[r0 + s:r0 + s + win - SUBLANES, :]
        for c in range(CONV_CH // LANES):
            cs = slice(c * LANES, (c + 1) * LANES)
            for r in range(r0, r0 + CONV_CHUNK, rows):
                acc = jnp.broadcast_to(cb_ref[:, cs], (rows, LANES))
                for w in range(CONV_WIDTH):
                    shift, base = (off + w) % SUBLANES, (off + w) // SUBLANES * SUBLANES
                    if shift == 0:
                        window = zext_ref[r + base:r + base + rows, cs]
                    else:
                        window = shift_ref[shift - 1, r - r0 + base:r - r0 + base + rows, cs]
                    acc = acc + window * cw_ref[w:w + 1, cs]
                conv_ref[r:r + rows, cs] = acc

    y = conv_ref[...]
    mu = jnp.mean(y, axis=-1, keepdims=True)
    yc = y - mu
    var = jnp.mean(yc * yc, axis=-1, keepdims=True)
    y = yc * lax.rsqrt(var + EPS) * lng_ref[...] + lnb_ref[...]
    y = y * _sigmoid(y)
    conv_out = _dot(y.astype(BF16), wco_ref[...].astype(BF16))
    mla_out = lax.dot_general(o_ref[...], wmo_ref[...].astype(BF16), (((0,), (0,)), ((), ())),
                              preferred_element_type=F32)
    gates = gates_ref[...].astype(F32)
    merged = gates[:, :D_MODEL] * conv_out + gates[:, D_MODEL:] * mla_out
    h1 = x_ref[...] + _dot(merged.astype(BF16), wout_ref[...].astype(BF16))

    u = _rms(h1, xg_ref[...]).astype(BF16)
    q = (_dot(u, wxq_ref[...].astype(BF16)) * (X_HEAD_DIM ** -0.5)).astype(BF16)
    heads = []
    for h in range(X_HEADS):
        hs = slice(h * X_HEAD_DIM, (h + 1) * X_HEAD_DIM)
        s = _dot(q[:, hs], mkt_ref[0, hs, :])
        p = jnp.exp(s - jnp.max(s, axis=-1, keepdims=True))
        p = p / jnp.sum(p, axis=-1, keepdims=True)
        heads.append(_dot(p.astype(BF16), mv_ref[0, :, hs]))
    xo = jnp.concatenate(heads, axis=1).astype(BF16)
    h_ref[...] = h1 + _dot(xo, wxo_ref[...].astype(BF16))


def _mixer(x2, z, gates, o, cw, cb, lng, lnb, wco, wmo, wout, xg, wxq, mkt, mv, wxo, seq):
    t = x2.shape[0]
    tm = TOKEN_TILE
    tiles_per_seq = seq // tm
    halo_per_tile = tm // HALO
    mem_len = mv.shape[1]
    xw = X_HEADS * X_HEAD_DIM
    full = lambda arr: pl.BlockSpec(arr.shape, lambda i: (0,) * arr.ndim)
    row = lambda w: pl.BlockSpec((tm, w), lambda i: (i, 0))
    in_specs = [row(D_MODEL), row(CONV_CH),
                pl.BlockSpec((HALO, CONV_CH), lambda i: (jnp.maximum(i * halo_per_tile - 1, 0), 0)),
                row(2 * D_MODEL), pl.BlockSpec((MLA_HEADS * V_DIM, tm), lambda i: (0, i)),
                full(cw), full(cb), full(lng), full(lnb), full(wco), full(wmo), full(wout),
                full(xg), full(wxq),
                pl.BlockSpec((1, xw, mem_len), lambda i: (i // tiles_per_seq, 0, 0)),
                pl.BlockSpec((1, mem_len, xw), lambda i: (i // tiles_per_seq, 0, 0)),
                full(wxo)]
    return pl.pallas_call(
        functools.partial(_mixer_kernel, tiles_per_seq),
        grid=(t // tm,),
        in_specs=in_specs,
        out_specs=row(D_MODEL),
        out_shape=jax.ShapeDtypeStruct((t, D_MODEL), F32),
        scratch_shapes=[pltpu.VMEM((tm + HALO, CONV_CH), F32),
                        pltpu.VMEM((SUBLANES - 1, CONV_CHUNK + HALO, CONV_CH), F32),
                        pltpu.VMEM((tm, CONV_CH), F32)],
        compiler_params=pltpu.CompilerParams(dimension_semantics=("parallel",),
                                             vmem_limit_bytes=VMEM_LIMIT),
        name="mixer",
    )(x2, z, z, gates, o, cw, cb, lng, lnb, wco, wmo, wout, xg, wxq, mkt, mv, wxo)


def _mlp_kernel(h_ref, g_ref, w1_ref, w2_ref, fg_ref, y_ref):
    h = h_ref[...]
    u = _rms(h, g_ref[...]).astype(BF16)
    chunk = 1024
    acc = h
    for c in range(0, D_FF, chunk):
        a = jnp.maximum(_dot(u, w1_ref[:, c:c + chunk].astype(BF16)), 0.0)
        acc = acc + _dot((a * a).astype(BF16), w2_ref[c:c + chunk, :].astype(BF16))
    y_ref[...] = _rms(acc, fg_ref[...])


def _mlp(h, g, w1, w2, fg):
    t = h.shape[0]
    tm = MLP_TILE
    full = lambda arr: pl.BlockSpec(arr.shape, lambda i: (0,) * arr.ndim, pipeline_mode=pl.Buffered(1))
    row = pl.BlockSpec((tm, D_MODEL), lambda i: (i, 0))
    return pl.pallas_call(
        _mlp_kernel,
        grid=(t // tm,),
        in_specs=[row, full(g), full(w1), full(w2), full(fg)],
        out_specs=row,
        out_shape=jax.ShapeDtypeStruct((t, D_MODEL), F32),
        compiler_params=pltpu.CompilerParams(dimension_semantics=("parallel",),
                                             vmem_limit_bytes=VMEM_LIMIT),
        name="mlp",
    )(h, g, w1, w2, fg)


def _rope_placement():
    e = np.zeros((HEAD_PAD, MLA_HEADS * HEAD_PAD), np.float32)
    for h in range(MLA_HEADS):
        for j in range(QK_ROPE):
            e[j, h * HEAD_PAD + QK_NOPE + j] = 1.0
    return jnp.asarray(e, BF16)


def kernel(x, mem, positions, norm_mix_g, w_in, conv_w, conv_b, conv_ln_g, conv_ln_b, w_conv_out, q_norm_g, w_uq, kv_norm_g, w_ukv, w_mla_out, w_out, norm_xattn_g, norm_mem_g, w_xq, w_xkv, w_xo, norm_mlp_g, w_mlp1, w_mlp2, final_norm_g):
    batch, seq, _ = x.shape
    mem_len = mem.shape[1]
    t = batch * seq
    assert w_in.shape[0] == 1, "single-layer block (the final rmsnorm is fused into the MLP kernel)"
    assert seq % IN_TILE == 0 and seq % KV_CHUNK == 0 and seq % TOKEN_TILE == 0 and t % ROPE_TILE == 0 and t % MLP_TILE == 0

    inv_freq = ROPE_THETA ** (-jnp.arange(ROPE_HALF, dtype=F32) / ROPE_HALF)
    cos_t, sin_t, cs = _rope_tables(positions.reshape(1, t), inv_freq.reshape(ROPE_HALF, 1))
    place = _rope_placement()
    row = lambda v: v.reshape(1, -1)

    h = x.reshape(t, D_MODEL)
    wi = jnp.swapaxes(w_in, 1, 2)
    wuq = w_uq[0].astype(BF16).reshape(Q_LORA, MLA_HEADS, QK_NOPE + QK_ROPE)
    wqt = jnp.pad(wuq, ((0, 0), (0, 0), (0, HEAD_PAD - QK_NOPE - QK_ROPE))).reshape(Q_LORA, -1).T
    wukv = w_ukv[0].astype(BF16).reshape(KV_LORA, MLA_HEADS, QK_NOPE + V_DIM)
    wk = jnp.pad(wukv[:, :, :QK_NOPE], ((0, 0), (0, 0), (0, HEAD_PAD - QK_NOPE))).reshape(KV_LORA, -1)
    wvt = jnp.pad(wukv[:, :, QK_NOPE:], ((0, 0), (0, 0), (0, V_ROWS - V_DIM))).reshape(KV_LORA, -1).T
    ones = np.zeros((MLA_HEADS, V_ROWS, 1), np.float32)
    ones[:, V_DIM] = 1.0
    ones = jnp.asarray(ones.reshape(MLA_HEADS * V_ROWS, 1))

    z, gates, qt, kp, vt3 = _in_proj(
        h, row(norm_mix_g[0]), wi, row(q_norm_g[0]), row(kv_norm_g[0]), wqt, wk, wvt, ones, place,
        cos_t, sin_t, cs)

    k3 = kp.reshape(t // KV_CHUNK, KV_CHUNK, MLA_HEADS * HEAD_PAD)
    o = _attention(qt, k3, vt3, batch, seq)

    xw = X_HEADS * X_HEAD_DIM
    wxkv = w_xkv[0].astype(BF16)
    mkt, mv = _mem_kv(mem.reshape(batch * mem_len, D_MODEL), row(norm_mem_g[0]),
                      wxkv[:, :xw].T, wxkv[:, xw:], batch, mem_len)

    cw = jnp.pad(conv_w[0], ((0, HALO - CONV_WIDTH), (0, 0)))
    h = _mixer(h, z, gates, o, cw, row(conv_b[0]), row(conv_ln_g[0]), row(conv_ln_b[0]),
               w_conv_out[0], w_mla_out[0], w_out[0],
               row(norm_xattn_g[0]), w_xq[0], mkt, mv, w_xo[0], seq)
    h = _mlp(h, row(norm_mlp_g[0]), w_mlp1[0], w_mlp2[0],
             row(final_norm_g))
    return h.reshape(batch, seq, D_MODEL)
```

```python
import functools

import numpy as np
import jax
import jax.numpy as jnp
from jax import lax
from jax.experimental import pallas as pl
from jax.experimental.pallas import tpu as pltpu

D_MODEL = 1024
CONV_CH = D_MODEL // 2
CONV_WIDTH = 31
MLA_HEADS = 8
QK_NOPE = D_MODEL // 16
QK_ROPE = D_MODEL // 32
V_DIM = D_MODEL // 16
Q_LORA = 3 * D_MODEL // 8
KV_LORA = D_MODEL // 4
X_HEADS = 4
X_HEAD_DIM = D_MODEL // 8
D_FF = 4 * D_MODEL
ROPE_THETA = 10000.0
EPS = 1e-6

LANES = 128
SUBLANES = 8
BF16_ROWS = 2 * SUBLANES
HEAD_PAD = LANES
V_ROWS = V_DIM + BF16_ROWS
ATTN_HEADS = 8
ROPE_HALF = QK_ROPE // 2
HALO = 32
TOKEN_TILE = 1024
CONV_CHUNK = 256
MLP_TILE = 512
KV_CHUNK = 512
IN_TILE = 2 * KV_CHUNK
Q_TILE = KV_CHUNK
LOG2E = 1.4426950408889634


def _column_ranges(widths):
    out, start = {}, 0
    for name, width in widths:
        out[name] = (start, start + width)
        start += width
    return out


IN_COLS = _column_ranges([("glu_a", CONV_CH), ("glu_g", CONV_CH), ("c_q", Q_LORA), ("c_kv", KV_LORA),
                          ("k_rope", QK_ROPE), ("gates", 2 * D_MODEL)])
MASK_VALUE = -1e30
VMEM_LIMIT = 56 * 1024 * 1024

F32 = jnp.float32
BF16 = jnp.bfloat16


def _rms(x, g):
    return x * lax.rsqrt(jnp.mean(x * x, axis=-1, keepdims=True) + EPS) * g


def _sigmoid(x):
    return 1.0 / (1.0 + jnp.exp(-x))


def _dot(a, b):
    return jnp.dot(a, b, preferred_element_type=F32)


def _dot_nt(a, b):
    return lax.dot_general(a, b, (((1,), (1,)), ((), ())), preferred_element_type=F32)


def _in_proj_kernel(x_ref, g_ref, w_ref, qg_ref, kvg_ref, wqt_ref, wk_ref, wvt_ref, ones_ref, place_ref,
                    pos_ref, invf_ref,
                    z_ref, gates_ref, qt_ref, k_ref, vt_ref):
    u = _rms(x_ref[...], g_ref[...]).astype(BF16)
    proj = lambda name: _dot_nt(u, w_ref[0, IN_COLS[name][0]:IN_COLS[name][1], :].astype(BF16))
    z_ref[...] = (proj("glu_a") * _sigmoid(proj("glu_g"))).astype(BF16)
    gates_ref[...] = _sigmoid(proj("gates")).astype(BF16)

    cqn = _rms(proj("c_q"), qg_ref[...]).astype(BF16)
    qt = _dot_nt(wqt_ref[...], cqn)
    scale = (QK_NOPE + QK_ROPE) ** -0.5 * LOG2E
    ang = pos_ref[...].astype(F32) * invf_ref[...]
    c = jnp.cos(ang)
    s = jnp.sin(ang)
    for h in range(MLA_HEADS):
        b = h * HEAD_PAD
        r1 = b + QK_NOPE
        r2 = r1 + ROPE_HALF
        r3 = r2 + ROPE_HALF
        t1 = qt[r1:r2]
        t2 = qt[r2:r3]
        qt_ref[b:r1, :] = (qt[b:r1] * scale).astype(BF16)
        qt_ref[r1:r2, :] = ((t1 * c - t2 * s) * scale).astype(BF16)
        qt_ref[r2:r3, :] = ((t2 * c + t1 * s) * scale).astype(BF16)
        qt_ref[r3:b + HEAD_PAD, :] = (qt[r3:b + HEAD_PAD] * scale).astype(BF16)

    ckvn = _rms(proj("c_kv"), kvg_ref[...]).astype(BF16)
    kr = w_ref[0, IN_COLS["k_rope"][0]:IN_COLS["k_rope"][1], :]
    kr = jnp.concatenate([kr, kr[ROPE_HALF:], kr[:ROPE_HALF],
                          jnp.zeros((HEAD_PAD - 2 * QK_ROPE, kr.shape[1]), F32)], axis=0).astype(BF16)
    pad = jnp.zeros((HEAD_PAD - 2 * QK_ROPE, ang.shape[1]), F32)
    cs = jnp.concatenate([c, c, -s, s, pad], axis=0).T
    t = _dot_nt(u, kr) * cs
    rot = t + pltpu.roll(t, HEAD_PAD - QK_ROPE, 1)
    k_ref[...] = (_dot(ckvn, wk_ref[...]) + _dot(rot.astype(BF16), place_ref[...])).astype(BF16)
    vt = (_dot_nt(wvt_ref[...], ckvn) + ones_ref[...]).astype(BF16)
    for c in range(vt_ref.shape[0]):
        vt_ref[c] = vt[:, c * KV_CHUNK:(c + 1) * KV_CHUNK]


def _in_proj(x2, g, w, qg, kvg, wqt, wk, wvt, ones, place, pos_row, inv_freq_col):
    t = x2.shape[0]
    tm = IN_TILE
    chunks = tm // KV_CHUNK
    n = t // tm
    full = lambda arr: pl.BlockSpec(arr.shape, lambda i: (0,) * arr.ndim)
    row = lambda w: pl.BlockSpec((tm, w), lambda i: (i, 0))
    in_specs = [row(D_MODEL), full(g), full(w), full(qg), full(kvg), full(wqt), full(wk), full(wvt),
                full(ones), full(place),
                pl.BlockSpec((1, tm), lambda i: (0, i)), full(inv_freq_col)]
    out_shape = [jax.ShapeDtypeStruct((t, CONV_CH), BF16),
                 jax.ShapeDtypeStruct((t, 2 * D_MODEL), BF16),
                 jax.ShapeDtypeStruct((MLA_HEADS * HEAD_PAD, t), BF16),
                 jax.ShapeDtypeStruct((t, MLA_HEADS * HEAD_PAD), BF16),
                 jax.ShapeDtypeStruct((n * chunks, MLA_HEADS * V_ROWS, KV_CHUNK), BF16)]
    out_specs = [row(CONV_CH), row(2 * D_MODEL),
                 pl.BlockSpec((MLA_HEADS * HEAD_PAD, tm), lambda i: (0, i)),
                 row(MLA_HEADS * HEAD_PAD),
                 pl.BlockSpec((chunks, MLA_HEADS * V_ROWS, KV_CHUNK), lambda i: (i, 0, 0))]
    return pl.pallas_call(
        _in_proj_kernel,
        grid=(n,),
        in_specs=in_specs,
        out_specs=out_specs,
        out_shape=out_shape,
        compiler_params=pltpu.CompilerParams(dimension_semantics=("parallel",),
                                             vmem_limit_bytes=VMEM_LIMIT),
        name="in_proj",
    )(x2, g, w, qg, kvg, wqt, wk, wvt, ones, place, pos_row, inv_freq_col)


def _attn_kernel(qt_ref, k_ref, vt_ref, o_ref, s_scr, acc_scr):
    qi = pl.program_id(2)
    tk = k_ref.shape[1]
    tq = qt_ref.shape[1]
    assert tq == tk
    heads = range(ATTN_HEADS)
    half = tk // 2
    n_items = qi + 1
    causal = (lax.broadcasted_iota(jnp.int32, (tk, tq), 0) <= lax.broadcasted_iota(jnp.int32, (tk, tq), 1))
    qts = [qt_ref[h * HEAD_PAD:(h + 1) * HEAD_PAD, :] for h in heads]

    def scores_to(slot, kc, diagonal, hs=heads):
        ms = []
        for h in hs:
            k = k_ref[kc, :, h * HEAD_PAD:(h + 1) * HEAD_PAD]
            if not diagonal:
                s = _dot(k, qts[h])
                s_scr[slot, h] = s
                ms.append(jnp.max(s, axis=0, keepdims=True))
            else:
                top = jnp.where(causal[:half], _dot(k[:half], qts[h]), MASK_VALUE)
                low = jnp.where(causal[half:, half:], _dot(k[half:], qts[h][:, half:]), MASK_VALUE)
                s_scr[slot, h, :half, :] = top
                s_scr[slot, h, half:, half:] = low
                s_scr[slot, h, half:, :half] = jnp.full((tk - half, half), MASK_VALUE, F32)
                low_max = jnp.concatenate([jnp.full((1, half), MASK_VALUE, F32),
                                           jnp.max(low, axis=0, keepdims=True)], axis=1)
                ms.append(jnp.maximum(jnp.max(top, axis=0, keepdims=True), low_max))
        return tuple(ms)

    def accumulate(m_run, slot, kc, ms, hs=heads):
        new = []
        for h, m_old, m_chunk in zip(hs, m_run, ms):
            p = jnp.concatenate([jnp.exp2(s_scr[slot, h, r:r + BF16_ROWS, :] - m_chunk).astype(BF16)
                                 for r in range(0, tk, BF16_ROWS)], axis=0)
            pv = _dot(vt_ref[kc, h * V_ROWS:(h + 1) * V_ROWS, :], p)
            m_new = jnp.maximum(m_old, m_chunk)
            a = jnp.exp2(m_old - m_new)
            b = jnp.exp2(m_chunk - m_new)
            acc_scr[h] = a * acc_scr[h] + b * pv
            new.append(m_new)
        return tuple(new)

    def step(cur, look_ahead, j, carry):
        m_run, kc, ms = carry
        new_run, ahead_m = [], []
        for h in heads:
            if look_ahead:
                ahead_m += scores_to(1 - cur, j, False, (h,))
            new_run += accumulate(m_run[h:h + 1], cur, kc, ms[h:h + 1], (h,))
        return tuple(new_run), j, (tuple(ahead_m) if look_ahead else ms)

    branches = [functools.partial(step, cur, look_ahead) for look_ahead in (True, False) for cur in (0, 1)]

    def body(j, carry):
        return lax.switch(2 * (j == n_items - 1).astype(jnp.int32) + j % 2, branches, j, carry)

    acc_scr[...] = jnp.zeros(acc_scr.shape, F32)
    m_init = tuple(jnp.full((1, tq), MASK_VALUE, F32) for _ in heads)
    lax.fori_loop(0, n_items, body, (m_init, qi, scores_to(0, qi, True)))
    outs = [acc_scr[h, :V_DIM, :] / acc_scr[h, V_DIM:V_DIM + 1, :] for h in heads]
    o_ref[...] = jnp.concatenate(outs, axis=0).astype(BF16)


def _attention(qt, k3, vt3, batch, seq):
    tq = Q_TILE
    nq = seq // tq
    nk = seq // KV_CHUNK
    t = batch * seq
    return pl.pallas_call(
        _attn_kernel,
        grid=(batch, MLA_HEADS // ATTN_HEADS, nq),
        in_specs=[pl.BlockSpec((ATTN_HEADS * HEAD_PAD, tq), lambda b, hg, qi: (hg, b * nq + qi)),
                  pl.BlockSpec((nk, KV_CHUNK, ATTN_HEADS * HEAD_PAD), lambda b, hg, qi: (b, 0, hg),
                               pipeline_mode=pl.Buffered(1)),
                  pl.BlockSpec((nk, ATTN_HEADS * V_ROWS, KV_CHUNK), lambda b, hg, qi: (b, hg, 0),
                               pipeline_mode=pl.Buffered(1))],
        out_specs=pl.BlockSpec((ATTN_HEADS * V_DIM, tq), lambda b, hg, qi: (hg, b * nq + qi)),
        out_shape=jax.ShapeDtypeStruct((MLA_HEADS * V_DIM, t), BF16),
        scratch_shapes=[pltpu.VMEM((2, ATTN_HEADS, KV_CHUNK, tq), F32),
                        pltpu.VMEM((ATTN_HEADS, V_ROWS, tq), F32)],
        compiler_params=pltpu.CompilerParams(
            dimension_semantics=("parallel", "parallel", "arbitrary"),
            vmem_limit_bytes=VMEM_LIMIT),
        name="mla_attention",
    )(qt, k3, vt3)


def _mem_kernel(mem_ref, g_ref, wkt_ref, wv_ref, kt_ref, v_ref):
    mn = _rms(mem_ref[...], g_ref[...]).astype(BF16)
    kt_ref[0] = _dot_nt(wkt_ref[...], mn).astype(BF16)
    v_ref[0] = _dot(mn, wv_ref[...]).astype(BF16)


def _mem_kv(mem2, g, wkt, wv, batch, mem_len):
    xw = X_HEADS * X_HEAD_DIM
    full = lambda arr: pl.BlockSpec(arr.shape, lambda b: (0,) * arr.ndim)
    return pl.pallas_call(
        _mem_kernel,
        grid=(batch,),
        in_specs=[pl.BlockSpec((mem_len, D_MODEL), lambda b: (b, 0)), full(g), full(wkt), full(wv)],
        out_specs=[pl.BlockSpec((1, xw, mem_len), lambda b: (b, 0, 0)),
                   pl.BlockSpec((1, mem_len, xw), lambda b: (b, 0, 0))],
        out_shape=[jax.ShapeDtypeStruct((batch, xw, mem_len), BF16),
                   jax.ShapeDtypeStruct((batch, mem_len, xw), BF16)],
        compiler_params=pltpu.CompilerParams(dimension_semantics=("parallel",),
                                             vmem_limit_bytes=VMEM_LIMIT),
        name="mem_kv",
    )(mem2, g, wkt, wv)


def _mixer_kernel(tiles_per_seq, x_ref, z_ref, halo_ref, gates_ref, o_ref,
                  cw_ref, cb_ref, lng_ref, lnb_ref, wco_ref, wmo_ref, wout_ref,
                  xg_ref, wxq_ref, mkt_ref, mv_ref, wxo_ref,
                  h_ref, zext_ref, shift_ref, conv_ref):
    tm = x_ref.shape[0]
    first = (pl.program_id(0) % tiles_per_seq) == 0
    zext_ref[0:HALO, :] = jnp.where(first, 0.0, halo_ref[...].astype(F32))
    zext_ref[HALO:, :] = z_ref[...].astype(F32)

    rows = 64
    off = HALO - (CONV_WIDTH - 1)
    win = CONV_CHUNK + HALO
    for r0 in range(0, tm, CONV_CHUNK):
        for s in range(1, SUBLANES):
            shift_ref[s - 1, 0:win - SUBLANES, :] = zext_ref[r0 + s:r0 + s + win - SUBLANES, :]
        for c in range(CONV_CH // LANES):
            cs = slice(c * LANES, (c + 1) * LANES)
            for r in range(r0, r0 + CONV_CHUNK, rows):
                acc = jnp.broadcast_to(cb_ref[:, cs], (rows, LANES))
                for w in range(CONV_WIDTH):
                    shift, base = (off + w) % SUBLANES, (off + w) // SUBLANES * SUBLANES
                    if shift == 0:
                        window = zext_ref[r + base:r + base + rows, cs]
                    else:
                        window = shift_ref[shift - 1, r - r0 + base:r - r0 + base + rows, cs]
                    acc = acc + window * cw_ref[w:w + 1, cs]
                conv_ref[r:r + rows, cs] = acc

    y = conv_ref[...]
    mu = jnp.mean(y, axis=-1, keepdims=True)
    yc = y - mu
    var = jnp.mean(yc * yc, axis=-1, keepdims=True)
    y = yc * lax.rsqrt(var + EPS) * lng_ref[...] + lnb_ref[...]
    y = y * _sigmoid(y)
    conv_out = _dot(y.astype(BF16), wco_ref[...])
    mla_out = lax.dot_general(o_ref[...], wmo_ref[...], (((0,), (0,)), ((), ())),
                              preferred_element_type=F32)
    gates = gates_ref[...].astype(F32)
    merged = gates[:, :D_MODEL] * conv_out + gates[:, D_MODEL:] * mla_out
    h1 = x_ref[...] + _dot(merged.astype(BF16), wout_ref[...])

    u = _rms(h1, xg_ref[...]).astype(BF16)
    q = (_dot(u, wxq_ref[...]) * (X_HEAD_DIM ** -0.5)).astype(BF16)
    heads = []
    for h in range(X_HEADS):
        hs = slice(h * X_HEAD_DIM, (h + 1) * X_HEAD_DIM)
        s = _dot(q[:, hs], mkt_ref[0, hs, :])
        p = jnp.exp(s - jnp.max(s, axis=-1, keepdims=True))
        p = p / jnp.sum(p, axis=-1, keepdims=True)
        heads.append(_dot(p.astype(BF16), mv_ref[0, :, hs]))
    xo = jnp.concatenate(heads, axis=1).astype(BF16)
    h_ref[...] = h1 + _dot(xo, wxo_ref[...])


def _mixer(x2, z, gates, o, cw, cb, lng, lnb, wco, wmo, wout, xg, wxq, mkt, mv, wxo, seq):
    t = x2.shape[0]
    tm = TOKEN_TILE
    tiles_per_seq = seq // tm
    halo_per_tile = tm // HALO
    mem_len = mv.shape[1]
    xw = X_HEADS * X_HEAD_DIM
    full = lambda arr: pl.BlockSpec(arr.shape, lambda i: (0,) * arr.ndim)
    row = lambda w: pl.BlockSpec((tm, w), lambda i: (i, 0))
    in_specs = [row(D_MODEL), row(CONV_CH),
                pl.BlockSpec((HALO, CONV_CH), lambda i: (jnp.maximum(i * halo_per_tile - 1, 0), 0)),
                row(2 * D_MODEL), pl.BlockSpec((MLA_HEADS * V_DIM, tm), lambda i: (0, i)),
                full(cw), full(cb), full(lng), full(lnb), full(wco), full(wmo), full(wout),
                full(xg), full(wxq),
                pl.BlockSpec((1, xw, mem_len), lambda i: (i // tiles_per_seq, 0, 0)),
                pl.BlockSpec((1, mem_len, xw), lambda i: (i // tiles_per_seq, 0, 0)),
                full(wxo)]
    return pl.pallas_call(
        functools.partial(_mixer_kernel, tiles_per_seq),
        grid=(t // tm,),
        in_specs=in_specs,
        out_specs=row(D_MODEL),
        out_shape=jax.ShapeDtypeStruct((t, D_MODEL), F32),
        scratch_shapes=[pltpu.VMEM((tm + HALO, CONV_CH), F32),
                        pltpu.VMEM((SUBLANES - 1, CONV_CHUNK + HALO, CONV_CH), F32),
                        pltpu.VMEM((tm, CONV_CH), F32)],
        compiler_params=pltpu.CompilerParams(dimension_semantics=("parallel",),
                                             vmem_limit_bytes=VMEM_LIMIT),
        name="mixer",
    )(x2, z, z, gates, o, cw, cb, lng, lnb, wco, wmo, wout, xg, wxq, mkt, mv, wxo)


def _mlp_kernel(h_ref, g_ref, w1_ref, w2_ref, fg_ref, y_ref):
    h = h_ref[...]
    u = _rms(h, g_ref[...]).astype(BF16)
    chunk = 1024
    acc = h
    for c in range(0, D_FF, chunk):
        a = jnp.maximum(_dot(u, w1_ref[:, c:c + chunk].astype(BF16)), 0.0)
        acc = acc + _dot((a * a).astype(BF16), w2_ref[c:c + chunk, :].astype(BF16))
    y_ref[...] = _rms(acc, fg_ref[...])


def _mlp(h, g, w1, w2, fg):
    t = h.shape[0]
    tm = MLP_TILE
    full = lambda arr: pl.BlockSpec(arr.shape, lambda i: (0,) * arr.ndim, pipeline_mode=pl.Buffered(1))
    row = pl.BlockSpec((tm, D_MODEL), lambda i: (i, 0))
    return pl.pallas_call(
        _mlp_kernel,
        grid=(t // tm,),
        in_specs=[row, full(g), full(w1), full(w2), full(fg)],
        out_specs=row,
        out_shape=jax.ShapeDtypeStruct((t, D_MODEL), F32),
        compiler_params=pltpu.CompilerParams(dimension_semantics=("parallel",),
                                             vmem_limit_bytes=VMEM_LIMIT),
        name="mlp",
    )(h, g, w1, w2, fg)


def _rope_placement():
    e = np.zeros((HEAD_PAD, MLA_HEADS * HEAD_PAD), np.float32)
    for h in range(MLA_HEADS):
        for j in range(QK_ROPE):
            e[j, h * HEAD_PAD + QK_NOPE + j] = 1.0
    return jnp.asarray(e, BF16)


def kernel(x, mem, positions, norm_mix_g, w_in, conv_w, conv_b, conv_ln_g, conv_ln_b, w_conv_out, q_norm_g, w_uq, kv_norm_g, w_ukv, w_mla_out, w_out, norm_xattn_g, norm_mem_g, w_xq, w_xkv, w_xo, norm_mlp_g, w_mlp1, w_mlp2, final_norm_g):
    batch, seq, _ = x.shape
    mem_len = mem.shape[1]
    t = batch * seq
    assert w_in.shape[0] == 1, "single-layer block (the final rmsnorm is fused into the MLP kernel)"
    assert seq % IN_TILE == 0 and seq % KV_CHUNK == 0 and seq % TOKEN_TILE == 0 and t % MLP_TILE == 0

    inv_freq = ROPE_THETA ** (-jnp.arange(ROPE_HALF, dtype=F32) / ROPE_HALF)
    place = _rope_placement()
    row = lambda v: v.reshape(1, -1)

    h = x.reshape(t, D_MODEL)
    wi = jnp.swapaxes(w_in, 1, 2)
    wuq = w_uq[0].astype(BF16).reshape(Q_LORA, MLA_HEADS, QK_NOPE + QK_ROPE)
    wqt = jnp.pad(wuq, ((0, 0), (0, 0), (0, HEAD_PAD - QK_NOPE - QK_ROPE))).reshape(Q_LORA, -1).T
    wukv = w_ukv[0].astype(BF16).reshape(KV_LORA, MLA_HEADS, QK_NOPE + V_DIM)
    wk = jnp.pad(wukv[:, :, :QK_NOPE], ((0, 0), (0, 0), (0, HEAD_PAD - QK_NOPE))).reshape(KV_LORA, -1)
    wvt = jnp.pad(wukv[:, :, QK_NOPE:], ((0, 0), (0, 0), (0, V_ROWS - V_DIM))).reshape(KV_LORA, -1).T
    ones = np.zeros((MLA_HEADS, V_ROWS, 1), np.float32)
    ones[:, V_DIM] = 1.0
    ones = jnp.asarray(ones.reshape(MLA_HEADS * V_ROWS, 1))

    z, gates, qt, kp, vt3 = _in_proj(
        h, row(norm_mix_g[0]), wi, row(q_norm_g[0]), row(kv_norm_g[0]), wqt, wk, wvt, ones, place,
        positions.reshape(1, t), inv_freq.reshape(ROPE_HALF, 1))

    k3 = kp.reshape(t // KV_CHUNK, KV_CHUNK, MLA_HEADS * HEAD_PAD)
    o = _attention(qt, k3, vt3, batch, seq)

    xw = X_HEADS * X_HEAD_DIM
    wxkv = w_xkv[0].astype(BF16)
    mkt, mv = _mem_kv(mem.reshape(batch * mem_len, D_MODEL), row(norm_mem_g[0]),
                      wxkv[:, :xw].T, wxkv[:, xw:], batch, mem_len)

    cw = jnp.pad(conv_w[0], ((0, HALO - CONV_WIDTH), (0, 0)))
    h = _mixer(h, z, gates, o, cw, row(conv_b[0]), row(conv_ln_g[0]), row(conv_ln_b[0]),
               w_conv_out[0].astype(BF16), w_mla_out[0].astype(BF16), w_out[0].astype(BF16),
               row(norm_xattn_g[0]), w_xq[0].astype(BF16), mkt, mv, w_xo[0].astype(BF16), seq)
    h = _mlp(h, row(norm_mlp_g[0]), w_mlp1[0], w_mlp2[0],
             row(final_norm_g))
    return h.reshape(batch, seq, D_MODEL)
```

```python
import functools

import numpy as np
import jax
import jax.numpy as jnp
from jax import lax
from jax.experimental import pallas as pl
from jax.experimental.pallas import tpu as pltpu

D_MODEL = 1024
CONV_CH = D_MODEL // 2
CONV_WIDTH = 31
MLA_HEADS = 8
QK_NOPE = D_MODEL // 16
QK_ROPE = D_MODEL // 32
V_DIM = D_MODEL // 16
Q_LORA = 3 * D_MODEL // 8
KV_LORA = D_MODEL // 4
X_HEADS = 4
X_HEAD_DIM = D_MODEL // 8
D_FF = 4 * D_MODEL
ROPE_THETA = 10000.0
EPS = 1e-6

LANES = 128
SUBLANES = 8
BF16_ROWS = 2 * SUBLANES
HEAD_PAD = LANES
V_ROWS = V_DIM + BF16_ROWS
ATTN_HEADS = 8
ROPE_HALF = QK_ROPE // 2
HALO = 32
TOKEN_TILE = 1024
CONV_CHUNK = 256
MLP_TILE = 512
KV_CHUNK = 512
IN_TILE = 2 * KV_CHUNK
Q_TILE = KV_CHUNK
LOG2E = 1.4426950408889634


def _column_ranges(widths):
    out, start = {}, 0
    for name, width in widths:
        out[name] = (start, start + width)
        start += width
    return out


IN_COLS = _column_ranges([("glu_a", CONV_CH), ("glu_g", CONV_CH), ("c_q", Q_LORA), ("c_kv", KV_LORA),
                          ("k_rope", QK_ROPE), ("gates", 2 * D_MODEL)])
MASK_VALUE = -1e30
VMEM_LIMIT = 56 * 1024 * 1024

F32 = jnp.float32
BF16 = jnp.bfloat16


def _rms(x, g):
    return x * lax.rsqrt(jnp.mean(x * x, axis=-1, keepdims=True) + EPS) * g


def _sigmoid(x):
    return 1.0 / (1.0 + jnp.exp(-x))


def _dot(a, b):
    return jnp.dot(a, b, preferred_element_type=F32)


def _dot_nt(a, b):
    return lax.dot_general(a, b, (((1,), (1,)), ((), ())), preferred_element_type=F32)


def _in_proj_kernel(x_ref, g_ref, w_ref, qg_ref, kvg_ref, wqt_ref, wk_ref, wvt_ref, ones_ref, place_ref,
                    pos_ref, invf_ref,
                    z_ref, gates_ref, qt_ref, k_ref, vt_ref):
    u = _rms(x_ref[...], g_ref[...]).astype(BF16)
    proj = lambda name: _dot_nt(u, w_ref[0, IN_COLS[name][0]:IN_COLS[name][1], :].astype(BF16))
    z_ref[...] = (proj("glu_a") * _sigmoid(proj("glu_g"))).astype(BF16)
    gates_ref[...] = _sigmoid(proj("gates")).astype(BF16)

    cqn = _rms(proj("c_q"), qg_ref[...]).astype(BF16)
    qt = _dot_nt(wqt_ref[...], cqn)
    scale = (QK_NOPE + QK_ROPE) ** -0.5 * LOG2E
    ang = pos_ref[...].astype(F32) * invf_ref[...]
    c = jnp.cos(ang)
    s = jnp.sin(ang)
    for h in range(MLA_HEADS):
        b = h * HEAD_PAD
        r1 = b + QK_NOPE
        r2 = r1 + ROPE_HALF
        r3 = r2 + ROPE_HALF
        t1 = qt[r1:r2]
        t2 = qt[r2:r3]
        qt_ref[b:r1, :] = (qt[b:r1] * scale).astype(BF16)
        qt_ref[r1:r2, :] = ((t1 * c - t2 * s) * scale).astype(BF16)
        qt_ref[r2:r3, :] = ((t2 * c + t1 * s) * scale).astype(BF16)
        qt_ref[r3:b + HEAD_PAD, :] = (qt[r3:b + HEAD_PAD] * scale).astype(BF16)

    ckvn = _rms(proj("c_kv"), kvg_ref[...]).astype(BF16)
    kr = w_ref[0, IN_COLS["k_rope"][0]:IN_COLS["k_rope"][1], :]
    kr = jnp.concatenate([kr, kr[ROPE_HALF:], kr[:ROPE_HALF],
                          jnp.zeros((HEAD_PAD - 2 * QK_ROPE, kr.shape[1]), F32)], axis=0).astype(BF16)
    pad = jnp.zeros((HEAD_PAD - 2 * QK_ROPE, ang.shape[1]), F32)
    cs = jnp.concatenate([c, c, -s, s, pad], axis=0).T
    t = _dot_nt(u, kr) * cs
    rot = t + pltpu.roll(t, HEAD_PAD - QK_ROPE, 1)
    k_ref[...] = (_dot(ckvn, wk_ref[...]) + _dot(rot.astype(BF16), place_ref[...])).astype(BF16)
    vt = (_dot_nt(wvt_ref[...], ckvn) + ones_ref[...]).astype(BF16)
    for c in range(vt_ref.shape[0]):
        vt_ref[c] = vt[:, c * KV_CHUNK:(c + 1) * KV_CHUNK]


def _in_proj(x2, g, w, qg, kvg, wqt, wk, wvt, ones, place, pos_row, inv_freq_col):
    t = x2.shape[0]
    tm = IN_TILE
    chunks = tm // KV_CHUNK
    n = t // tm
    full = lambda arr: pl.BlockSpec(arr.shape, lambda i: (0,) * arr.ndim)
    row = lambda w: pl.BlockSpec((tm, w), lambda i: (i, 0))
    in_specs = [row(D_MODEL), full(g), full(w), full(qg), full(kvg), full(wqt), full(wk), full(wvt),
                full(ones), full(place),
                pl.BlockSpec((1, tm), lambda i: (0, i)), full(inv_freq_col)]
    out_shape = [jax.ShapeDtypeStruct((t, CONV_CH), BF16),
                 jax.ShapeDtypeStruct((t, 2 * D_MODEL), BF16),
                 jax.ShapeDtypeStruct((MLA_HEADS * HEAD_PAD, t), BF16),
                 jax.ShapeDtypeStruct((t, MLA_HEADS * HEAD_PAD), BF16),
                 jax.ShapeDtypeStruct((n * chunks, MLA_HEADS * V_ROWS, KV_CHUNK), BF16)]
    out_specs = [row(CONV_CH), row(2 * D_MODEL),
                 pl.BlockSpec((MLA_HEADS * HEAD_PAD, tm), lambda i: (0, i)),
                 row(MLA_HEADS * HEAD_PAD),
                 pl.BlockSpec((chunks, MLA_HEADS * V_ROWS, KV_CHUNK), lambda i: (i, 0, 0))]
    return pl.pallas_call(
        _in_proj_kernel,
        grid=(n,),
        in_specs=in_specs,
        out_specs=out_specs,
        out_shape=out_shape,
        compiler_params=pltpu.CompilerParams(dimension_semantics=("parallel",),
                                             vmem_limit_bytes=VMEM_LIMIT),
        name="in_proj",
    )(x2, g, w, qg, kvg, wqt, wk, wvt, ones, place, pos_row, inv_freq_col)


def _attn_kernel(qt_ref, k_hbm, vt_hbm, o_ref, k_ref, vt_ref, kv_sem, s_scr, acc_scr):
    qi = pl.program_id(2)
    nk = k_ref.shape[0]
    first_chunk = pl.program_id(0) * nk

    def chunk_copies(c):
        return (pltpu.make_async_copy(k_hbm.at[first_chunk + c], k_ref.at[c], kv_sem.at[0, c]),
                pltpu.make_async_copy(vt_hbm.at[first_chunk + c], vt_ref.at[c], kv_sem.at[1, c]))

    @pl.when(qi == 0)
    def _():
        for c in range(nk):
            for copy in chunk_copies(c):
                copy.start()

    for copy in chunk_copies(qi):
        copy.wait()
    tk = k_ref.shape[1]
    tq = qt_ref.shape[1]
    assert tq == tk
    heads = range(ATTN_HEADS)
    half = tk // 2
    n_items = qi + 1
    causal = (lax.broadcasted_iota(jnp.int32, (tk, tq), 0) <= lax.broadcasted_iota(jnp.int32, (tk, tq), 1))
    qts = [qt_ref[h * HEAD_PAD:(h + 1) * HEAD_PAD, :] for h in heads]

    def scores_to(slot, kc, diagonal, hs=heads):
        ms = []
        for h in hs:
            k = k_ref[kc, :, h * HEAD_PAD:(h + 1) * HEAD_PAD]
            if not diagonal:
                s = _dot(k, qts[h])
                s_scr[slot, h] = s
                ms.append(jnp.max(s, axis=0, keepdims=True))
            else:
                top = jnp.where(causal[:half], _dot(k[:half], qts[h]), MASK_VALUE)
                low = jnp.where(causal[half:, half:], _dot(k[half:], qts[h][:, half:]), MASK_VALUE)
                s_scr[slot, h, :half, :] = top
                s_scr[slot, h, half:, half:] = low
                s_scr[slot, h, half:, :half] = jnp.full((tk - half, half), MASK_VALUE, F32)
                low_max = jnp.concatenate([jnp.full((1, half), MASK_VALUE, F32),
                                           jnp.max(low, axis=0, keepdims=True)], axis=1)
                ms.append(jnp.maximum(jnp.max(top, axis=0, keepdims=True), low_max))
        return tuple(ms)

    def accumulate(m_run, slot, kc, ms, hs=heads):
        new = []
        for h, m_old, m_chunk in zip(hs, m_run, ms):
            p = jnp.concatenate([jnp.exp2(s_scr[slot, h, r:r + BF16_ROWS, :] - m_chunk).astype(BF16)
                                 for r in range(0, tk, BF16_ROWS)], axis=0)
            pv = _dot(vt_ref[kc, h * V_ROWS:(h + 1) * V_ROWS, :], p)
            m_new = jnp.maximum(m_old, m_chunk)
            a = jnp.exp2(m_old - m_new)
            b = jnp.exp2(m_chunk - m_new)
            acc_scr[h] = a * acc_scr[h] + b * pv
            new.append(m_new)
        return tuple(new)

    def step(cur, look_ahead, j, carry):
        m_run, kc, ms = carry
        new_run, ahead_m = [], []
        for h in heads:
            if look_ahead:
                ahead_m += scores_to(1 - cur, j, False, (h,))
            new_run += accumulate(m_run[h:h + 1], cur, kc, ms[h:h + 1], (h,))
        return tuple(new_run), j, (tuple(ahead_m) if look_ahead else ms)

    branches = [functools.partial(step, cur, look_ahead) for look_ahead in (True, False) for cur in (0, 1)]

    def body(j, carry):
        return lax.switch(2 * (j == n_items - 1).astype(jnp.int32) + j % 2, branches, j, carry)

    acc_scr[...] = jnp.zeros(acc_scr.shape, F32)
    m_init = tuple(jnp.full((1, tq), MASK_VALUE, F32) for _ in heads)
    lax.fori_loop(0, n_items, body, (m_init, qi, scores_to(0, qi, True)))
    outs = [acc_scr[h, :V_DIM, :] / acc_scr[h, V_DIM:V_DIM + 1, :] for h in heads]
    o_ref[...] = jnp.concatenate(outs, axis=0).astype(BF16)


def _attention(qt, k3, vt3, batch, seq):
    tq = Q_TILE
    nq = seq // tq
    nk = seq // KV_CHUNK
    t = batch * seq
    assert ATTN_HEADS == MLA_HEADS
    return pl.pallas_call(
        _attn_kernel,
        grid=(batch, MLA_HEADS // ATTN_HEADS, nq),
        in_specs=[pl.BlockSpec((ATTN_HEADS * HEAD_PAD, tq), lambda b, hg, qi: (hg, b * nq + qi)),
                  pl.BlockSpec(memory_space=pl.ANY), pl.BlockSpec(memory_space=pl.ANY)],
        out_specs=pl.BlockSpec((ATTN_HEADS * V_DIM, tq), lambda b, hg, qi: (hg, b * nq + qi)),
        out_shape=jax.ShapeDtypeStruct((MLA_HEADS * V_DIM, t), BF16),
        scratch_shapes=[pltpu.VMEM((nk, KV_CHUNK, MLA_HEADS * HEAD_PAD), BF16),
                        pltpu.VMEM((nk, MLA_HEADS * V_ROWS, KV_CHUNK), BF16),
                        pltpu.SemaphoreType.DMA((2, nk)),
                        pltpu.VMEM((2, ATTN_HEADS, KV_CHUNK, tq), F32),
                        pltpu.VMEM((ATTN_HEADS, V_ROWS, tq), F32)],
        compiler_params=pltpu.CompilerParams(
            dimension_semantics=("arbitrary", "arbitrary", "arbitrary"),
            vmem_limit_bytes=VMEM_LIMIT),
        name="mla_attention",
    )(qt, k3, vt3)


def _mem_kernel(mem_ref, g_ref, wkt_ref, wv_ref, kt_ref, v_ref):
    mn = _rms(mem_ref[...], g_ref[...]).astype(BF16)
    kt_ref[0] = _dot_nt(wkt_ref[...], mn).astype(BF16)
    v_ref[0] = _dot(mn, wv_ref[...]).astype(BF16)


def _mem_kv(mem2, g, wkt, wv, batch, mem_len):
    xw = X_HEADS * X_HEAD_DIM
    full = lambda arr: pl.BlockSpec(arr.shape, lambda b: (0,) * arr.ndim)
    return pl.pallas_call(
        _mem_kernel,
        grid=(batch,),
        in_specs=[pl.BlockSpec((mem_len, D_MODEL), lambda b: (b, 0)), full(g), full(wkt), full(wv)],
        out_specs=[pl.BlockSpec((1, xw, mem_len), lambda b: (b, 0, 0)),
                   pl.BlockSpec((1, mem_len, xw), lambda b: (b, 0, 0))],
        out_shape=[jax.ShapeDtypeStruct((batch, xw, mem_len), BF16),
                   jax.ShapeDtypeStruct((batch, mem_len, xw), BF16)],
        compiler_params=pltpu.CompilerParams(dimension_semantics=("parallel",),
                                             vmem_limit_bytes=VMEM_LIMIT),
        name="mem_kv",
    )(mem2, g, wkt, wv)


def _mixer_kernel(tiles_per_seq, x_ref, z_ref, halo_ref, gates_ref, o_ref,
                  cw_ref, cb_ref, lng_ref, lnb_ref, wco_ref, wmo_ref, wout_ref,
                  xg_ref, wxq_ref, mkt_ref, mv_ref, wxo_ref,
                  h_ref, zext_ref, shift_ref, conv_ref):
    tm = x_ref.shape[0]
    first = (pl.program_id(0) % tiles_per_seq) == 0
    zext_ref[0:HALO, :] = jnp.where(first, 0.0, halo_ref[...].astype(F32))
    zext_ref[HALO:, :] = z_ref[...].astype(F32)

    rows = 64
    off = HALO - (CONV_WIDTH - 1)
    win = CONV_CHUNK + HALO
    for r0 in range(0, tm, CONV_CHUNK):
        for s in range(1, SUBLANES):
            shift_ref[s - 1, 0:win - SUBLANES, :] = zext_ref[r0 + s:r0 + s + win - SUBLANES, :]
        for c in range(CONV_CH // LANES):
            cs = slice(c * LANES, (c + 1) * LANES)
            for r in range(r0, r0 + CONV_CHUNK, rows):
                acc = jnp.broadcast_to(cb_ref[:, cs], (rows, LANES))
                for w in range(CONV_WIDTH):
                    shift, base = (off + w) % SUBLANES, (off + w) // SUBLANES * SUBLANES
                    if shift == 0:
                        window = zext_ref[r + base:r + base + rows, cs]
                    else:
                        window = shift_ref[shift - 1, r - r0 + base:r - r0 + base + rows, cs]
                    acc = acc + window * cw_ref[w:w + 1, cs]
                conv_ref[r:r + rows, cs] = acc

    y = conv_ref[...]
    mu = jnp.mean(y, axis=-1, keepdims=True)
    yc = y - mu
    var = jnp.mean(yc * yc, axis=-1, keepdims=True)
    y = yc * lax.rsqrt(var + EPS) * lng_ref[...] + lnb_ref[...]
    y = y * _sigmoid(y)
    conv_out = _dot(y.astype(BF16), wco_ref[...])
    mla_out = lax.dot_general(o_ref[...], wmo_ref[...], (((0,), (0,)), ((), ())),
                              preferred_element_type=F32)
    gates = gates_ref[...].astype(F32)
    merged = gates[:, :D_MODEL] * conv_out + gates[:, D_MODEL:] * mla_out
    h1 = x_ref[...] + _dot(merged.astype(BF16), wout_ref[...])

    u = _rms(h1, xg_ref[...]).astype(BF16)
    q = (_dot(u, wxq_ref[...]) * (X_HEAD_DIM ** -0.5)).astype(BF16)
    heads = []
    for h in range(X_HEADS):
        hs = slice(h * X_HEAD_DIM, (h + 1) * X_HEAD_DIM)
        s = _dot(q[:, hs], mkt_ref[0, hs, :])
        p = jnp.exp(s - jnp.max(s, axis=-1, keepdims=True))
        p = p / jnp.sum(p, axis=-1, keepdims=True)
        heads.append(_dot(p.astype(BF16), mv_ref[0, :, hs]))
    xo = jnp.concatenate(heads, axis=1).astype(BF16)
    h_ref[...] = h1 + _dot(xo, wxo_ref[...])


def _mixer(x2, z, gates, o, cw, cb, lng, lnb, wco, wmo, wout, xg, wxq, mkt, mv, wxo, seq):
    t = x2.shape[0]
    tm = TOKEN_TILE
    tiles_per_seq = seq // tm
    halo_per_tile = tm // HALO
    mem_len = mv.shape[1]
    xw = X_HEADS * X_HEAD_DIM
    full = lambda arr: pl.BlockSpec(arr.shape, lambda i: (0,) * arr.ndim)
    row = lambda w: pl.BlockSpec((tm, w), lambda i: (i, 0))
    in_specs = [row(D_MODEL), row(CONV_CH),
                pl.BlockSpec((HALO, CONV_CH), lambda i: (jnp.maximum(i * halo_per_tile - 1, 0), 0)),
                row(2 * D_MODEL), pl.BlockSpec((MLA_HEADS * V_DIM, tm), lambda i: (0, i)),
                full(cw), full(cb), full(lng), full(lnb), full(wco), full(wmo), full(wout),
                full(xg), full(wxq),
                pl.BlockSpec((1, xw, mem_len), lambda i: (i // tiles_per_seq, 0, 0)),
                pl.BlockSpec((1, mem_len, xw), lambda i: (i // tiles_per_seq, 0, 0)),
                full(wxo)]
    return pl.pallas_call(
        functools.partial(_mixer_kernel, tiles_per_seq),
        grid=(t // tm,),
        in_specs=in_specs,
        out_specs=row(D_MODEL),
        out_shape=jax.ShapeDtypeStruct((t, D_MODEL), F32),
        scratch_shapes=[pltpu.VMEM((tm + HALO, CONV_CH), F32),
                        pltpu.VMEM((SUBLANES - 1, CONV_CHUNK + HALO, CONV_CH), F32),
                        pltpu.VMEM((tm, CONV_CH), F32)],
        compiler_params=pltpu.CompilerParams(dimension_semantics=("parallel",),
                                             vmem_limit_bytes=VMEM_LIMIT),
        name="mixer",
    )(x2, z, z, gates, o, cw, cb, lng, lnb, wco, wmo, wout, xg, wxq, mkt, mv, wxo)


def _mlp_kernel(h_ref, g_ref, w1_ref, w2_ref, fg_ref, y_ref):
    h = h_ref[...]
    u = _rms(h, g_ref[...]).astype(BF16)
    chunk = 1024
    acc = h
    for c in range(0, D_FF, chunk):
        a = jnp.maximum(_dot(u, w1_ref[:, c:c + chunk].astype(BF16)), 0.0)
        acc = acc + _dot((a * a).astype(BF16), w2_ref[c:c + chunk, :].astype(BF16))
    y_ref[...] = _rms(acc, fg_ref[...])


def _mlp(h, g, w1, w2, fg):
    t = h.shape[0]
    tm = MLP_TILE
    full = lambda arr: pl.BlockSpec(arr.shape, lambda i: (0,) * arr.ndim, pipeline_mode=pl.Buffered(1))
    row = pl.BlockSpec((tm, D_MODEL), lambda i: (i, 0))
    return pl.pallas_call(
        _mlp_kernel,
        grid=(t // tm,),
        in_specs=[row, full(g), full(w1), full(w2), full(fg)],
        out_specs=row,
        out_shape=jax.ShapeDtypeStruct((t, D_MODEL), F32),
        compiler_params=pltpu.CompilerParams(dimension_semantics=("parallel",),
                                             vmem_limit_bytes=VMEM_LIMIT),
        name="mlp",
    )(h, g, w1, w2, fg)


def _rope_placement():
    e = np.zeros((HEAD_PAD, MLA_HEADS * HEAD_PAD), np.float32)
    for h in range(MLA_HEADS):
        for j in range(QK_ROPE):
            e[j, h * HEAD_PAD + QK_NOPE + j] = 1.0
    return jnp.asarray(e, BF16)


def kernel(x, mem, positions, norm_mix_g, w_in, conv_w, conv_b, conv_ln_g, conv_ln_b, w_conv_out, q_norm_g, w_uq, kv_norm_g, w_ukv, w_mla_out, w_out, norm_xattn_g, norm_mem_g, w_xq, w_xkv, w_xo, norm_mlp_g, w_mlp1, w_mlp2, final_norm_g):
    batch, seq, _ = x.shape
    mem_len = mem.shape[1]
    t = batch * seq
    assert w_in.shape[0] == 1, "single-layer block (the final rmsnorm is fused into the MLP kernel)"
    assert seq % IN_TILE == 0 and seq % KV_CHUNK == 0 and seq % TOKEN_TILE == 0 and t % MLP_TILE == 0

    inv_freq = ROPE_THETA ** (-jnp.arange(ROPE_HALF, dtype=F32) / ROPE_HALF)
    place = _rope_placement()
    row = lambda v: v.reshape(1, -1)

    h = x.reshape(t, D_MODEL)
    wi = jnp.swapaxes(w_in, 1, 2)
    wuq = w_uq[0].astype(BF16).reshape(Q_LORA, MLA_HEADS, QK_NOPE + QK_ROPE)
    wqt = jnp.pad(wuq, ((0, 0), (0, 0), (0, HEAD_PAD - QK_NOPE - QK_ROPE))).reshape(Q_LORA, -1).T
    wukv = w_ukv[0].astype(BF16).reshape(KV_LORA, MLA_HEADS, QK_NOPE + V_DIM)
    wk = jnp.pad(wukv[:, :, :QK_NOPE], ((0, 0), (0, 0), (0, HEAD_PAD - QK_NOPE))).reshape(KV_LORA, -1)
    wvt = jnp.pad(wukv[:, :, QK_NOPE:], ((0, 0), (0, 0), (0, V_ROWS - V_DIM))).reshape(KV_LORA, -1).T
    ones = np.zeros((MLA_HEADS, V_ROWS, 1), np.float32)
    ones[:, V_DIM] = 1.0
    ones = jnp.asarray(ones.reshape(MLA_HEADS * V_ROWS, 1))

    z, gates, qt, kp, vt3 = _in_proj(
        h, row(norm_mix_g[0]), wi, row(q_norm_g[0]), row(kv_norm_g[0]), wqt, wk, wvt, ones, place,
        positions.reshape(1, t), inv_freq.reshape(ROPE_HALF, 1))

    k3 = kp.reshape(t // KV_CHUNK, KV_CHUNK, MLA_HEADS * HEAD_PAD)
    o = _attention(qt, k3, vt3, batch, seq)

    xw = X_HEADS * X_HEAD_DIM
    wxkv = w_xkv[0].astype(BF16)
    mkt, mv = _mem_kv(mem.reshape(batch * mem_len, D_MODEL), row(norm_mem_g[0]),
                      wxkv[:, :xw].T, wxkv[:, xw:], batch, mem_len)

    cw = jnp.pad(conv_w[0], ((0, HALO - CONV_WIDTH), (0, 0)))
    h = _mixer(h, z, gates, o, cw, row(conv_b[0]), row(conv_ln_g[0]), row(conv_ln_b[0]),
               w_conv_out[0].astype(BF16), w_mla_out[0].astype(BF16), w_out[0].astype(BF16),
               row(norm_xattn_g[0]), w_xq[0].astype(BF16), mkt, mv, w_xo[0].astype(BF16), seq)
    h = _mlp(h, row(norm_mlp_g[0]), w_mlp1[0], w_mlp2[0],
             row(final_norm_g))
    return h.reshape(batch, seq, D_MODEL)
```
